```python
import math
import jax, jax.numpy as jnp
from jax import lax
import numpy as np

D_MODEL = 2048
BATCH = 2
SEQ = 8192
DEPTH = 1

EPS = 1e-5
SSD_HEADS = 32
SSD_HEAD_DIM = 64
SSD_INNER = SSD_HEADS * SSD_HEAD_DIM
SSD_GROUPS = 4
SSD_HEADS_PER_GROUP = SSD_HEADS // SSD_GROUPS
SSD_STATE = 128
SSD_CONV = 4
SSD_CHUNK = 128
SSD_CONV_DIM = SSD_INNER + 2 * SSD_GROUPS * SSD_STATE
ATTN_HEADS = 16
ATTN_HEAD_DIM = 128
ATTN_INNER = ATTN_HEADS * ATTN_HEAD_DIM
ROPE_THETA = 500000.0
ROT_DIM = ATTN_HEAD_DIM // 4
IDX_HEADS = 8
IDX_DIM = 64
IDX_ROT_DIM = IDX_DIM // 4
IDX_TOPK = 256
Q_BLOCK = 128
N_EXPERTS = 32
TOP_K = 4
D_EXPERT = 2048
SWIGLU_ALPHA = 1.702
SWIGLU_LIMIT = 7.0
IN_SPLIT_SIZES = (SSD_INNER, SSD_CONV_DIM, SSD_HEADS,
                  ATTN_INNER, ATTN_INNER, ATTN_INNER,
                  IDX_HEADS * IDX_DIM, IDX_DIM, IDX_HEADS,
                  D_MODEL, D_MODEL)
D_IN_PROJ = sum(IN_SPLIT_SIZES)

kernel_name = 'hybrid_ssd_dsa_moe_block'


def rmsnorm(x, w):
    xf = x.astype(jnp.float32)
    xf = xf * lax.rsqrt(jnp.mean(xf * xf, axis=-1, keepdims=True) + EPS)
    return xf.astype(x.dtype) * w


def split_cols(u, sizes):
    offs, acc = [], 0
    for s in sizes[:-1]:
        acc += s
        offs.append(acc)
    return jnp.split(u, offs, axis=-1)


def rope_tables(length, rot_dim):
    inv = ROPE_THETA ** (-jnp.arange(0, rot_dim, 2, dtype=jnp.float32) / rot_dim)
    ang = jnp.arange(length, dtype=jnp.float32)[:, None] * inv[None, :]
    return jnp.cos(ang), jnp.sin(ang)


def partial_rotary(x, cos, sin, rot_dim):
    half = rot_dim // 2
    x1, x2, rest = x[..., :half], x[..., half:rot_dim], x[..., rot_dim:]
    out = jnp.concatenate([x1 * cos - x2 * sin, x2 * cos + x1 * sin, rest], axis=-1)
    return out.astype(x.dtype)


def ssd_branch(xbc_raw, z, dt_raw, conv_w, conv_b, dt_bias, a_log, d_skip, norm_w):
    b, L, _ = xbc_raw.shape
    G, R, P, N, CH = SSD_GROUPS, SSD_HEADS_PER_GROUP, SSD_HEAD_DIM, SSD_STATE, SSD_CHUNK
    nc = L // CH
    xpad = jnp.pad(xbc_raw, ((0, 0), (SSD_CONV - 1, 0), (0, 0)))
    xbc = sum(xpad[:, j:j + L] * conv_w[j] for j in range(SSD_CONV)) + conv_b
    xbc = jax.nn.silu(xbc)
    xs, bm, cm = split_cols(xbc, (SSD_INNER, G * N, G * N))
    x = xs.reshape(b, nc, CH, G, R, P)
    bm = bm.reshape(b, nc, CH, G, N)
    cm = cm.reshape(b, nc, CH, G, N)
    dt = jax.nn.softplus((dt_raw + dt_bias).astype(jnp.float32)).reshape(b, nc, CH, G, R)
    a = -jnp.exp(a_log.astype(jnp.float32)).reshape(G, R)
    da_cs = jnp.cumsum(dt * a, axis=2)
    xdt = x * dt[..., None]
    causal = jnp.tril(jnp.ones((CH, CH), dtype=bool))[None, None, :, :, None, None]
    seg = da_cs[:, :, :, None] - da_cs[:, :, None, :]
    decay = jnp.exp(jnp.where(causal, seg, -jnp.inf))
    cb = jnp.einsum('bclgn,bcsgn->bclsg', cm, bm)
    y_diag = jnp.einsum('bclsg,bclsgr,bcsgrp->bclgrp', cb, decay, xdt)
    decay_to_end = jnp.exp(da_cs[:, :, -1:] - da_cs)
    states = jnp.einsum('bclgn,bclgr,bclgrp->bcgrpn', bm, decay_to_end, xdt)
    chunk_decay = jnp.exp(da_cs[:, :, -1])

    def step(h, inp):
        st, dec = inp
        return h * dec[..., None, None] + st, h

    _, prev = lax.scan(step, jnp.zeros_like(states[:, 0]),
                       (jnp.moveaxis(states, 1, 0), jnp.moveaxis(chunk_decay, 1, 0)))
    prev = jnp.moveaxis(prev, 0, 1)
    y_off = jnp.einsum('bclgn,bcgrpn,bclgr->bclgrp', cm, prev, jnp.exp(da_cs))
    y = y_diag + y_off + x * d_skip.reshape(G, R)[:, :, None]
    y = y.reshape(b, L, SSD_INNER).astype(xbc_raw.dtype)
    return rmsnorm(y * jax.nn.silu(z), norm_w)


def dsa_branch(q, k, v, q_idx, k_idx, w_idx):
    b, L, _ = q.shape
    q = q.reshape(b, L, ATTN_HEADS, ATTN_HEAD_DIM)
    k = k.reshape(b, L, ATTN_HEADS, ATTN_HEAD_DIM)
    v = v.reshape(b, L, ATTN_HEADS, ATTN_HEAD_DIM)
    q_idx = q_idx.reshape(b, L, IDX_HEADS, IDX_DIM)
    cos, sin = rope_tables(L, ROT_DIM)
    q = partial_rotary(q, cos[:, None, :], sin[:, None, :], ROT_DIM)
    k = partial_rotary(k, cos[:, None, :], sin[:, None, :], ROT_DIM)
    icos, isin = rope_tables(L, IDX_ROT_DIM)
    q_idx = partial_rotary(q_idx, icos[:, None, :], isin[:, None, :], IDX_ROT_DIM)
    k_idx = partial_rotary(k_idx, icos, isin, IDX_ROT_DIM)
    topk = min(IDX_TOPK, L // 4)
    scale = ATTN_HEAD_DIM ** -0.5
    idx_scale = (IDX_DIM ** -0.5) * (IDX_HEADS ** -0.5)
    key_pos = jnp.arange(L)
    bidx = jnp.arange(b)[:, None, None]

    def block(i):
        start = i * Q_BLOCK
        qb = lax.dynamic_slice_in_dim(q, start, Q_BLOCK, axis=1)
        qib = lax.dynamic_slice_in_dim(q_idx, start, Q_BLOCK, axis=1)
        wb = lax.dynamic_slice_in_dim(w_idx, start, Q_BLOCK, axis=1)
        qpos = start + jnp.arange(Q_BLOCK)
        rel = jax.nn.relu(jnp.einsum('bqhd,bsd->bqhs', qib, k_idx).astype(jnp.float32))
        score = jnp.einsum('bqh,bqhs->bqs', wb.astype(jnp.float32), rel) * idx_scale
        admissible = key_pos[None, :] <= qpos[:, None]
        score = jnp.where(admissible[None], score, -jnp.inf)
        _, sel = lax.top_k(score, topk)
        valid = sel <= qpos[None, :, None]
        ksel = k[bidx, sel]
        vsel = v[bidx, sel]
        logits = jnp.einsum('bqhd,bqjhd->bqhj', qb, ksel).astype(jnp.float32) * scale
        logits = jnp.where(valid[:, :, None, :], logits, -jnp.inf)
        p = jax.nn.softmax(logits, axis=-1).astype(v.dtype)
        return jnp.einsum('bqhj,bqjhd->bqhd', p, vsel)

    out = lax.map(block, jnp.arange(L // Q_BLOCK))
    return jnp.moveaxis(out, 0, 1).reshape(b, L, ATTN_INNER)


def mixer(xn, w_in, conv_w, conv_b, dt_bias, a_log, d_skip, ssd_norm_w,
          w_ssd_branch, w_attn_branch, w_out):
    u = xn @ w_in
    z, xbc, dt_raw, q, k, v, q_idx, k_idx, w_idx, g_ssd, g_attn = split_cols(u, IN_SPLIT_SIZES)
    y_ssd = ssd_branch(xbc, z, dt_raw, conv_w, conv_b, dt_bias, a_log, d_skip, ssd_norm_w) @ w_ssd_branch
    y_attn = dsa_branch(q, k, v, q_idx, k_idx, w_idx) @ w_attn_branch
    merged = jax.nn.sigmoid(g_ssd) * y_ssd + jax.nn.sigmoid(g_attn) * y_attn
    return merged @ w_out


def moe(xn, router_w, router_b, w1, b1, w2, b2):
    b, L, D = xn.shape
    t = xn.reshape(b * L, D)
    logits = (t @ router_w + router_b).astype(jnp.float32)
    vals, idx = lax.top_k(logits, TOP_K)
    wts = jax.nn.softmax(vals, axis=-1)
    gate = jnp.sum(jax.nn.one_hot(idx, N_EXPERTS, dtype=wts.dtype) * wts[..., None], axis=1)
    gate = gate.astype(t.dtype)
    out = jnp.zeros_like(t)
    for e in range(N_EXPERTS):
        h = t @ w1[e] + b1[e]
        glu = jnp.minimum(h[:, :D_EXPERT], SWIGLU_LIMIT)
        lin = jnp.clip(h[:, D_EXPERT:], -SWIGLU_LIMIT, SWIGLU_LIMIT)
        act = glu * jax.nn.sigmoid(SWIGLU_ALPHA * glu) * (lin + 1.0)
        out = out + gate[:, e:e + 1] * (act @ w2[e] + b2[e])
    return out.reshape(b, L, D)


def setup_inputs(seed: int = 0) -> dict:
    key = jax.random.key(seed)
    ks = jax.random.split(key, 20)
    f32 = jnp.float32

    def nrm(k, shape, scale):
        return jax.random.normal(k, shape, f32) * scale

    dt0 = jnp.exp(jax.random.uniform(ks[4], (DEPTH, SSD_HEADS), f32, math.log(1e-3), math.log(1e-1)))
    return {
        'x': nrm(ks[0], (BATCH, SEQ, D_MODEL), 1.0),
        'w_in': nrm(ks[1], (DEPTH, D_MODEL, D_IN_PROJ), D_MODEL ** -0.5),
        'conv_w': nrm(ks[2], (DEPTH, SSD_CONV, SSD_CONV_DIM), SSD_CONV ** -0.5),
        'conv_b': nrm(ks[3], (DEPTH, SSD_CONV_DIM), 0.01),
        'dt_bias': dt0 + jnp.log(-jnp.expm1(-dt0)),
        'a_log': jnp.log(jax.random.uniform(ks[5], (DEPTH, SSD_HEADS), f32, 1.0, 16.0)),
        'd_skip': 1.0 + nrm(ks[6], (DEPTH, SSD_HEADS), 0.01),
        'ssd_norm_w': 1.0 + nrm(ks[7], (DEPTH, SSD_INNER), 0.01),
        'w_ssd_branch': nrm(ks[8], (DEPTH, SSD_INNER, D_MODEL), SSD_INNER ** -0.5),
        'w_attn_branch': nrm(ks[9], (DEPTH, ATTN_INNER, D_MODEL), ATTN_INNER ** -0.5),
        'w_out': nrm(ks[10], (DEPTH, D_MODEL, D_MODEL), D_MODEL ** -0.5),
        'norm_mix_w': 1.0 + nrm(ks[11], (DEPTH, D_MODEL), 0.01),
        'norm_ffn_w': 1.0 + nrm(ks[12], (DEPTH, D_MODEL), 0.01),
        'router_w': nrm(ks[13], (DEPTH, D_MODEL, N_EXPERTS), D_MODEL ** -0.5),
        'router_b': nrm(ks[14], (DEPTH, N_EXPERTS), 0.01),
        'w_expert_in': nrm(ks[15], (DEPTH, N_EXPERTS, D_MODEL, 2 * D_EXPERT), D_MODEL ** -0.5),
        'b_expert_in': nrm(ks[16], (DEPTH, N_EXPERTS, 2 * D_EXPERT), 0.01),
        'w_expert_out': nrm(ks[17], (DEPTH, N_EXPERTS, D_EXPERT, D_MODEL), D_EXPERT ** -0.5),
        'b_expert_out': nrm(ks[18], (DEPTH, N_EXPERTS, D_MODEL), 0.01),
        'final_norm_w': 1.0 + nrm(ks[19], (D_MODEL,), 0.01),
    }


def reference(x, w_in, conv_w, conv_b, dt_bias, a_log, d_skip, ssd_norm_w, w_ssd_branch,
              w_attn_branch, w_out, norm_mix_w, norm_ffn_w, router_w, router_b,
              w_expert_in, b_expert_in, w_expert_out, b_expert_out, final_norm_w):
    h = x
    for l in range(DEPTH):
        h = h + mixer(rmsnorm(h, norm_mix_w[l]), w_in[l], conv_w[l], conv_b[l], dt_bias[l],
                      a_log[l], d_skip[l], ssd_norm_w[l], w_ssd_branch[l], w_attn_branch[l], w_out[l])
        h = h + moe(rmsnorm(h, norm_ffn_w[l]), router_w[l], router_b[l], w_expert_in[l],
                    b_expert_in[l], w_expert_out[l], b_expert_out[l])
    return rmsnorm(h, final_norm_w)
```

```python
import functools
import math

import jax
import jax.numpy as jnp
from jax import lax
from jax.experimental import pallas as pl
from jax.experimental.pallas import tpu as pltpu

EPS = 1e-5
SSD_HEADS = 32
SSD_HEAD_DIM = 64
SSD_GROUPS = 4
SSD_STATE = 128
SSD_CONV = 4
SSD_CHUNK = 128
ATTN_HEADS = 16
ATTN_HEAD_DIM = 128
ROPE_THETA = 500000.0
IDX_HEADS = 8
IDX_DIM = 64
IDX_TOPK = 256
Q_BLOCK = 128
N_EXPERTS = 32
TOP_K = 4
D_EXPERT = 2048
SWIGLU_ALPHA = 1.702
SWIGLU_LIMIT = 7.0

LANES = 128
SUBLANES = 8
VMEM_LIMIT_BYTES = 56 * 2**20
INT32_MIN = -2**31
MASKED_LOGIT = -1e30

F32 = jnp.float32
BF16 = jnp.bfloat16
I32 = jnp.int32


def _params(*sem):
    return pltpu.CompilerParams(dimension_semantics=sem, vmem_limit_bytes=VMEM_LIMIT_BYTES)


def _sigmoid(x):
    return 1.0 / (1.0 + jnp.exp(-x))


def _dot(a, b):
    return jnp.dot(a, b, preferred_element_type=F32)


def _dot_nt(a, b):
    return lax.dot_general(a, b, (((1,), (1,)), ((), ())), preferred_element_type=F32)


def _split3(x):
    hi = x.astype(BF16)
    r1 = x - hi.astype(F32)
    mid = r1.astype(BF16)
    lo = (r1 - mid.astype(F32)).astype(BF16)
    return hi, mid, lo


def _rmsnorm_body(x_ref, w_ref, o_ref):
    x = x_ref[...]
    ms = jnp.mean(x * x, axis=-1, keepdims=True)
    o_ref[...] = (x * lax.rsqrt(ms + EPS) * w_ref[...]).astype(o_ref.dtype)


def _rmsnorm(x, w, out_dtype, tm):
    t, d = x.shape
    return pl.pallas_call(
        _rmsnorm_body,
        out_shape=jax.ShapeDtypeStruct((t, d), out_dtype),
        grid=(t // tm,),
        in_specs=[pl.BlockSpec((tm, d), lambda i: (i, 0)), pl.BlockSpec((1, d), lambda i: (0, 0))],
        out_specs=pl.BlockSpec((tm, d), lambda i: (i, 0)),
        compiler_params=_params("parallel"),
        name="rmsnorm",
    )(x, w.reshape(1, d))


def _ep_identity(acc):
    return acc


def _ep_rope(acc, c, s1, s2, *, shift):
    outs = []
    for g in range(acc.shape[1] // LANES):
        o = acc[:, g * LANES:(g + 1) * LANES]
        outs.append(o * c + pltpu.roll(o, LANES - shift, 1) * s1 + pltpu.roll(o, shift, 1) * s2)
    return outs[0] if len(outs) == 1 else jnp.concatenate(outs, axis=1)


def _ep_gate(acc, g):
    return _sigmoid(g) * acc


def _ep_gate_add(acc, g, p):
    return p + _sigmoid(g) * acc


def _ep_residual(acc, x):
    return x + acc


def _mm_body(*refs, epilogue):
    a_ref, w_ref = refs[0], refs[1]
    o_ref = refs[-1]
    acc = _dot(a_ref[...], w_ref[...])
    o_ref[...] = epilogue(acc, *[r[...] for r in refs[2:-1]]).astype(o_ref.dtype)


def _matmul(a, w, out_dtype, name, *, tm, tn, epilogue=_ep_identity, extras=(), extra_specs=()):
    m, k = a.shape
    n = w.shape[1]
    tn = min(tn, n)
    return pl.pallas_call(
        functools.partial(_mm_body, epilogue=epilogue),
        out_shape=jax.ShapeDtypeStruct((m, n), out_dtype),
        grid=(m // tm, n // tn),
        in_specs=[pl.BlockSpec((tm, k), lambda i, j: (i, 0)),
                  pl.BlockSpec((k, tn), lambda i, j: (0, j)), *extra_specs],
        out_specs=pl.BlockSpec((tm, tn), lambda i, j: (i, j)),
        compiler_params=_params("parallel", "arbitrary"),
        name=name,
    )(a, w, *extras)


def _rope_tables(length, rot_dim, head_dim):
    half = rot_dim // 2
    inv = ROPE_THETA ** (-jnp.arange(0, rot_dim, 2, dtype=F32) / rot_dim)
    ang = jnp.arange(length, dtype=F32)[:, None] * inv[None, :]
    cos, sin = jnp.cos(ang), jnp.sin(ang)
    zeros = lambda n: jnp.zeros((length, n), F32)
    c = jnp.concatenate([cos, cos, jnp.ones((length, head_dim - rot_dim), F32)], axis=1)
    s1 = jnp.concatenate([-sin, zeros(head_dim - half)], axis=1)
    s2 = jnp.concatenate([zeros(half), sin, zeros(head_dim - rot_dim)], axis=1)
    reps = LANES // head_dim
    return tuple(jnp.tile(t, (1, reps)) for t in (c, s1, s2))


def _rope_matmul(a, w, tables, rot_dim, seq, name, *, tm, tn):
    nblk = seq // tm
    spec = pl.BlockSpec((tm, LANES), lambda i, j: (i % nblk, 0))
    return _matmul(a, w, BF16, name, tm=tm, tn=tn,
                   epilogue=functools.partial(_ep_rope, shift=rot_dim // 2),
                   extras=tables, extra_specs=(spec, spec, spec))


def _ssd_body(xbc_ref, z_ref, misc_ref, cw_ref, cb_ref, dtb_ref, alog_ref, dsk_ref, nw_ref,
              ltri_ref, exp_ref, o_ref, xext_ref, state_ref, *, inner, groups, heads):
    ch = SSD_CHUNK
    n = SSD_STATE
    gw = inner // groups
    c_idx = pl.program_id(1)
    tail = SSD_CONV - 1

    @pl.when(c_idx == 0)
    def _():
        xext_ref[0:SUBLANES, :] = jnp.zeros((SUBLANES, xext_ref.shape[1]), F32)
        state_ref[...] = jnp.zeros(state_ref.shape, F32)

    @pl.when(c_idx > 0)
    def _():
        xext_ref[0:SUBLANES, :] = xext_ref[ch:ch + SUBLANES, :]

    xext_ref[SUBLANES:SUBLANES + ch, :] = xbc_ref[...]

    conv = cb_ref[...]
    for j in range(SSD_CONV):
        conv = conv + xext_ref[pl.ds(SUBLANES - tail + j, ch), :] * cw_ref[j:j + 1, :]
    xbc = conv * _sigmoid(conv)
    xs = xbc[:, :inner]
    bm = xbc[:, inner:inner + groups * n]
    cm = xbc[:, inner + groups * n:]

    lane = lax.broadcasted_iota(I32, (1, LANES), 1)
    head_lane = lane < heads
    a = jnp.where(head_lane, -jnp.exp(alog_ref[...]), 0.0)
    dtr = misc_ref[...] + dtb_ref[...]
    dt = jnp.maximum(dtr, 0.0) + jnp.log1p(jnp.exp(-jnp.abs(dtr)))
    da = dt * a

    ltri = ltri_ref[...]
    cs = sum(_dot(ltri, p) for p in _split3(da))
    expand = exp_ref[...]
    dt_e = sum(_dot(p, expand) for p in _split3(dt))
    cs_e = sum(_dot(p, expand) for p in _split3(cs))
    cs_last = cs_e[ch - 1:ch, :]
    ecs = jnp.exp(cs_e)
    dte = jnp.exp(cs_last - cs_e)
    chunk_decay = jnp.exp(cs_last)

    xdt = xs * dt_e
    xdt_b = xdt.astype(BF16)
    xw_b = (xdt * dte).astype(BF16)
    cs_t = cs.T

    row = lax.broadcasted_iota(I32, (ch, ch), 0)
    col = lax.broadcasted_iota(I32, (ch, ch), 1)
    causal = row >= col
    first_half = lax.broadcasted_iota(I32, (ch, LANES), 1) < SSD_HEAD_DIM
    heads_per_group = heads // groups

    diag_cols, off_cols = [], []
    for g in range(groups):
        bg = bm[:, g * n:(g + 1) * n]
        cg_b = cm[:, g * n:(g + 1) * n].astype(BF16)
        cb = _dot_nt(cg_b, bg.astype(BF16))
        bg_t = bg.T.astype(BF16)
        for pr in range(heads_per_group // 2):
            h0 = g * heads_per_group + 2 * pr
            xp = xdt_b[:, h0 * SSD_HEAD_DIM:(h0 + 2) * SSD_HEAD_DIM]
            res = []
            for h in (h0, h0 + 1):
                seg = cs[:, h:h + 1] - cs_t[h:h + 1, :]
                decay = jnp.exp(jnp.where(causal, seg, -jnp.inf))
                res.append(_dot((cb * decay).astype(BF16), xp))
            diag_cols.append(jnp.where(first_half, res[0], res[1]))
        prev = state_ref[g]
        off_cols.append(_dot(cg_b, prev.astype(BF16)) * ecs[:, g * gw:(g + 1) * gw])
        states = _dot(bg_t, xw_b[:, g * gw:(g + 1) * gw])
        state_ref[g] = prev * chunk_decay[:, g * gw:(g + 1) * gw] + states

    y = jnp.concatenate(diag_cols, axis=1) + jnp.concatenate(off_cols, axis=1) + xs * dsk_ref[...]

    z = z_ref[...]
    gated = y * (z * _sigmoid(z))
    ms = jnp.mean(gated * gated, axis=-1, keepdims=True)
    o_ref[...] = (gated * lax.rsqrt(ms + EPS) * nw_ref[...]).astype(o_ref.dtype)


def _ssd(xbc, z, misc, conv_w, conv_b, dt_bias, a_log, d_skip, norm_w, batch, seq):
    t, conv_dim = xbc.shape
    inner = z.shape[1]
    heads = dt_bias.shape[0]
    groups = SSD_GROUPS
    ch = SSD_CHUNK
    nc = seq // ch
    pad = lambda v: jnp.pad(v.astype(F32), (0, LANES - heads)).reshape(1, LANES)
    ltri = jnp.tril(jnp.ones((ch, ch), F32)).astype(BF16)
    head_of_lane = jnp.arange(inner) // SSD_HEAD_DIM
    expand = (jnp.arange(LANES)[:, None] == head_of_lane[None, :]).astype(BF16)
    d_exp = jnp.repeat(d_skip.astype(F32), SSD_HEAD_DIM).reshape(1, inner)
    row = lambda b, c: (b * nc + c, 0)
    const = lambda b, c: (0, 0)
    return pl.pallas_call(
        functools.partial(_ssd_body, inner=inner, groups=groups, heads=heads),
        out_shape=jax.ShapeDtypeStruct((t, inner), BF16),
        grid=(batch, nc),
        in_specs=[pl.BlockSpec((ch, conv_dim), row), pl.BlockSpec((ch, inner), row),
                  pl.BlockSpec((ch, LANES), row),
                  pl.BlockSpec((SSD_CONV, conv_dim), const), pl.BlockSpec((1, conv_dim), const),
                  pl.BlockSpec((1, LANES), const), pl.BlockSpec((1, LANES), const),
                  pl.BlockSpec((1, inner), const), pl.BlockSpec((1, inner), const),
                  pl.BlockSpec((ch, ch), const), pl.BlockSpec((LANES, inner), const)],
        out_specs=pl.BlockSpec((ch, inner), row),
        scratch_shapes=[pltpu.VMEM((SUBLANES + ch, conv_dim), F32),
                        pltpu.VMEM((groups, SSD_STATE, inner // groups), F32)],
        compiler_params=_params("arbitrary", "arbitrary"),
        name="ssd_scan",
    )(xbc, z, misc, conv_w, conv_b.reshape(1, conv_dim), pad(dt_bias), pad(a_log), d_exp,
      norm_w.reshape(1, inner), ltri, expand)


def _indexer_body(qi_ref, ki_ref, misc_ref, u_ref, mask_ref, keys_ref, *, seq, kc, topk, w_lane, idx_scale):
    qb = pl.program_id(1)
    tq = Q_BLOCK
    q0 = qb * tq
    n_chunks = (q0 + tq + kc - 1) // kc
    qpos = q0 + lax.broadcasted_iota(I32, (tq, kc), 0)
    kofs = lax.broadcasted_iota(I32, (tq, kc), 1)
    w = misc_ref[...]
    qi = qi_ref[...]

    def score_chunk(c, carry):
        off = pl.multiple_of(c * kc, kc)
        k_lo = ki_ref[pl.ds(off, kc), 0:LANES]
        k_hi = ki_ref[pl.ds(off, kc), LANES:2 * LANES]
        s = jnp.zeros((tq, kc), F32)
        for j in range(IDX_HEADS // 2):
            qj = qi[:, j * LANES:(j + 1) * LANES]
            w0 = w[:, w_lane + 2 * j:w_lane + 2 * j + 1]
            w1 = w[:, w_lane + 2 * j + 1:w_lane + 2 * j + 2]
            s = s + w0 * jnp.maximum(_dot_nt(qj, k_lo), 0.0) + w1 * jnp.maximum(_dot_nt(qj, k_hi), 0.0)
        s = s * idx_scale
        bits = lax.bitcast_convert_type(s, I32)
        key = jnp.where(bits >= 0, bits, bits ^ jnp.int32(0x7FFFFFFF))
        key = jnp.where(off + kofs <= qpos, key, jnp.int32(INT32_MIN))
        keys_ref[:, pl.ds(off, kc)] = key
        return carry

    lax.fori_loop(0, n_chunks, score_chunk, 0)

    def count(pred):
        def body(c, acc):
            k = keys_ref[:, pl.ds(pl.multiple_of(c * kc, kc), kc)]
            m = jnp.where(pred(k), 1, 0).astype(I32)
            for j in range(kc // LANES):
                acc = acc + m[:, j * LANES:(j + 1) * LANES]
            return acc
        acc = lax.fori_loop(0, n_chunks, body, jnp.zeros((tq, LANES), I32))
        return jnp.sum(acc, axis=1, keepdims=True)

    def bit_body(i, t_u):
        cand_u = t_u | jnp.left_shift(jnp.int32(1), 31 - i)
        cand_s = cand_u ^ jnp.int32(INT32_MIN)
        cnt = count(lambda k: k >= cand_s)
        return jnp.where(cnt >= topk, cand_u, t_u)

    t_u = lax.fori_loop(0, 32, bit_body, jnp.zeros((tq, 1), I32))
    t_s = t_u ^ jnp.int32(INT32_MIN)
    n_gt = count(lambda k: k > t_s)
    n_tie = jnp.where(t_u == 0, 0, topk - n_gt).astype(F32)

    def mask_chunk(c, seen):
        off = pl.multiple_of(c * kc, kc)
        k = keys_ref[:, pl.ds(off, kc)]
        tie = k == t_s
        tie_f = jnp.where(tie, 1.0, 0.0)
        rank = seen + _dot(tie_f.astype(BF16), u_ref[...])
        sel = (k > t_s) | (tie & (rank <= n_tie))
        mask_ref[:, pl.ds(off, kc)] = jnp.where(sel, 1, 0).astype(jnp.int8)
        return seen + jnp.sum(tie_f, axis=1, keepdims=True)

    lax.fori_loop(0, n_chunks, mask_chunk, jnp.zeros((tq, 1), F32))

    def zero_chunk(c, carry):
        mask_ref[:, pl.ds(pl.multiple_of(c * kc, kc), kc)] = jnp.zeros((tq, kc), jnp.int8)
        return carry

    lax.fori_loop(n_chunks, seq // kc, zero_chunk, 0)


def _indexer_mask(qi, ki, misc, batch, seq, w_lane):
    t = qi.shape[0]
    tq = Q_BLOCK
    nq = seq // tq
    kc = min(512, seq)
    topk = min(IDX_TOPK, seq // 4)
    upper = jnp.triu(jnp.ones((kc, kc), F32)).astype(BF16)
    idx_scale = (IDX_DIM ** -0.5) * (IDX_HEADS ** -0.5)
    return pl.pallas_call(
        functools.partial(_indexer_body, seq=seq, kc=kc, topk=topk, w_lane=w_lane, idx_scale=idx_scale),
        out_shape=jax.ShapeDtypeStruct((t, seq), jnp.int8),
        grid=(batch, nq),
        in_specs=[pl.BlockSpec((tq, qi.shape[1]), lambda b, q: (b * nq + q, 0)),
                  pl.BlockSpec((seq, ki.shape[1]), lambda b, q: (b, 0)),
                  pl.BlockSpec((tq, LANES), lambda b, q: (b * nq + q, 0)),
                  pl.BlockSpec((kc, kc), lambda b, q: (0, 0))],
        out_specs=pl.BlockSpec((tq, seq), lambda b, q: (b * nq + q, 0)),
        scratch_shapes=[pltpu.VMEM((tq, seq), I32)],
        compiler_params=_params("parallel", "arbitrary"),
        name="indexer_topk_mask",
    )(qi, ki, misc, upper)


def _attn_body(q_ref, k_ref, v_ref, mask_ref, o_ref, acc_ref, m_ref, l_ref, *, tq, tk, heads, scale):
    qb = pl.program_id(1)
    kb = pl.program_id(2)
    last = ((qb + 1) * tq - 1) // tk
    hd = ATTN_HEAD_DIM

    @pl.when(kb == 0)
    def _():
        acc_ref[...] = jnp.zeros(acc_ref.shape, F32)
        m_ref[...] = jnp.full(m_ref.shape, MASKED_LOGIT, F32)
        l_ref[...] = jnp.zeros(l_ref.shape, F32)

    @pl.when(kb <= last)
    def _():
        bias = jnp.where(mask_ref[...].astype(I32) != 0, 0.0, MASKED_LOGIT)
        for h in range(heads):
            cols = slice(h * hd, (h + 1) * hd)
            s = _dot_nt(q_ref[:, cols], k_ref[:, cols]) * scale + bias
            m_prev = m_ref[h]
            m_new = jnp.maximum(m_prev, jnp.max(s, axis=1, keepdims=True))
            alpha = jnp.exp(m_prev - m_new)
            p = jnp.exp(s - m_new[:, 0:1])
            l_ref[h] = alpha * l_ref[h] + jnp.sum(p, axis=1, keepdims=True)
            acc_ref[:, cols] = acc_ref[:, cols] * alpha + _dot(p.astype(BF16), v_ref[:, cols])
            m_ref[h] = m_new

    @pl.when(kb == pl.num_programs(2) - 1)
    def _():
        for h in range(heads):
            cols = slice(h * hd, (h + 1) * hd)
            o_ref[:, cols] = (acc_ref[:, cols] / l_ref[h]).astype(o_ref.dtype)


def _attention(qk, v, mask, batch, seq, *, tq, tk):
    t, inner = v.shape
    heads = inner // ATTN_HEAD_DIM
    tq, tk = min(tq, seq), min(tk, seq)
    nq, nk = seq // tq, seq // tk
    last = lambda q: ((q + 1) * tq - 1) // tk
    return pl.pallas_call(
        functools.partial(_attn_body, tq=tq, tk=tk, heads=heads, scale=ATTN_HEAD_DIM ** -0.5),
        out_shape=jax.ShapeDtypeStruct((t, inner), BF16),
        grid=(batch, nq, nk),
        in_specs=[pl.BlockSpec((tq, inner), lambda b, q, k: (b * nq + q, 0)),
                  pl.BlockSpec((tk, inner), lambda b, q, k: (b * nk + jnp.minimum(k, last(q)), 1)),
                  pl.BlockSpec((tk, inner), lambda b, q, k: (b * nk + jnp.minimum(k, last(q)), 0)),
                  pl.BlockSpec((tq, tk), lambda b, q, k: (b * nq + q, jnp.minimum(k, last(q))))],
        out_specs=pl.BlockSpec((tq, inner), lambda b, q, k: (b * nq + q, 0)),
        scratch_shapes=[pltpu.VMEM((tq, inner), F32),
                        pltpu.VMEM((heads, tq, LANES), F32),
                        pltpu.VMEM((heads, tq, LANES), F32)],
        compiler_params=_params("parallel", "parallel", "arbitrary"),
        name="masked_attention",
    )(qk, qk, v, mask)


def _router_body(h_ref, nw_ref, rw_ref, rb_ref, ltri_ref, tn_ref, ids_ref, gates_ref, rank_ref, cnt_ref,
                 carry_ref):
    @pl.when(pl.program_id(0) == 0)
    def _():
        carry_ref[...] = jnp.zeros(carry_ref.shape, F32)

    x = h_ref[...]
    ms = jnp.mean(x * x, axis=-1, keepdims=True)
    tn = x * lax.rsqrt(ms + EPS) * nw_ref[...]
    tn_ref[...] = tn
    tm = x.shape[0]
    lane = lax.broadcasted_iota(I32, (tm, LANES), 1)
    logits = _dot(tn.astype(BF16), rw_ref[...]) + rb_ref[...]
    work = jnp.where(lane < N_EXPERTS, logits, -jnp.inf)
    vals, hits = [], []
    for _ in range(TOP_K):
        mx = jnp.max(work, axis=1, keepdims=True)
        idx = jnp.min(jnp.where(work == mx, lane, LANES), axis=1, keepdims=True)
        hit = lane == idx
        work = jnp.where(hit, -jnp.inf, work)
        vals.append(mx)
        hits.append(hit)
    es = [jnp.exp(v - vals[0]) for v in vals]
    tot = sum(es)
    onehot = sum(jnp.where(h, 1.0, 0.0) for h in hits)
    before = _dot(ltri_ref[...], onehot.astype(BF16)) + carry_ref[...]
    lane_f = lane.astype(F32)
    ids = jnp.zeros((tm, LANES), F32)
    gates = jnp.zeros((tm, LANES), F32)
    ranks = jnp.zeros((tm, LANES), F32)
    for k in range(TOP_K):
        slot = lane == k
        ids = jnp.where(slot, jnp.sum(jnp.where(hits[k], lane_f, 0.0), axis=1, keepdims=True), ids)
        gates = jnp.where(slot, es[k] / tot, gates)
        ranks = jnp.where(slot, jnp.sum(jnp.where(hits[k], before, 0.0), axis=1, keepdims=True), ranks)
    ids_ref[...] = ids.astype(I32)
    gates_ref[...] = gates
    rank_ref[...] = ranks.astype(I32)
    carry_ref[...] = carry_ref[...] + jnp.sum(onehot, axis=0, keepdims=True)
    cnt_ref[...] = carry_ref[...]


def _router(h1, norm_w, router_w, router_b, tm):
    t, d = h1.shape
    rw = jnp.pad(router_w, ((0, 0), (0, LANES - N_EXPERTS))).astype(BF16)
    rb = jnp.pad(router_b.astype(F32), (0, LANES - N_EXPERTS)).reshape(1, LANES)
    ltri = jnp.tril(jnp.ones((tm, tm), F32), -1).astype(BF16)
    row = lambda i: (i, 0)
    const = lambda i: (0, 0)
    return pl.pallas_call(
        _router_body,
        out_shape=(jax.ShapeDtypeStruct((t, d), F32), jax.ShapeDtypeStruct((t, LANES), I32),
                   jax.ShapeDtypeStruct((t, LANES), F32), jax.ShapeDtypeStruct((t, LANES), I32),
                   jax.ShapeDtypeStruct((1, LANES), F32)),
        grid=(t // tm,),
        in_specs=[pl.BlockSpec((tm, d), row), pl.BlockSpec((1, d), const), pl.BlockSpec((d, LANES), const),
                  pl.BlockSpec((1, LANES), const), pl.BlockSpec((tm, tm), const)],
        out_specs=(pl.BlockSpec((tm, d), row), pl.BlockSpec((tm, LANES), row), pl.BlockSpec((tm, LANES), row),
                   pl.BlockSpec((tm, LANES), row), pl.BlockSpec((1, LANES), const)),
        scratch_shapes=[pltpu.VMEM((1, LANES), F32)],
        compiler_params=_params("arbitrary"),
        name="moe_router",
    )(h1, norm_w.reshape(1, d), rw, rb, ltri)


def _row_copy(src, src_row, dst, dst_row, sem):
    return pltpu.make_async_copy(src.at[pl.ds(src_row, 1)], dst.at[pl.ds(dst_row, 1)], sem)


def _dispatch_body(pos_ref, tn_ref, init_ref, xs_ref, sem, *, rows_per_step):
    del init_ref
    base = pl.program_id(0) * rows_per_step

    def issue(r, carry):
        _row_copy(tn_ref, (base + r) // TOP_K, xs_ref, pos_ref[base + r], sem).start()
        return carry

    def drain(r, carry):
        _row_copy(tn_ref, 0, xs_ref, 0, sem).wait()
        return carry

    lax.fori_loop(0, rows_per_step, issue, 0)
    lax.fori_loop(0, rows_per_step, drain, 0)


def _dispatch(pos_flat, tn, n_rows, rows_per_step):
    t, d = tn.shape
    grid_spec = pltpu.PrefetchScalarGridSpec(
        num_scalar_prefetch=1, grid=(t * TOP_K // rows_per_step,),
        in_specs=[pl.BlockSpec(memory_space=pl.ANY), pl.BlockSpec(memory_space=pl.ANY)],
        out_specs=pl.BlockSpec(memory_space=pl.ANY),
        scratch_shapes=[pltpu.SemaphoreType.DMA])
    return pl.pallas_call(
        functools.partial(_dispatch_body, rows_per_step=rows_per_step),
        out_shape=jax.ShapeDtypeStruct((n_rows, d), F32),
        grid_spec=grid_spec,
        input_output_aliases={2: 0},
        compiler_params=_params("arbitrary"),
        name="moe_dispatch",
    )(pos_flat, tn, jnp.zeros((n_rows, d), F32))


def _swiglu(hg, hl):
    glu = jnp.minimum(hg, SWIGLU_LIMIT)
    lin = jnp.clip(hl, -SWIGLU_LIMIT, SWIGLU_LIMIT)
    return glu * _sigmoid(SWIGLU_ALPHA * glu) * (lin + 1.0)


def _expert_changed(te_ref, i):
    return (i == 0) | (te_ref[i] != te_ref[jnp.maximum(i - 1, 0)])


def _gemm1_body(te_ref, tv_ref, x_ref, wg_ref, wl_ref, bg_ref, bl_ref, h_ref, wgb_ref, wlb_ref):
    i = pl.program_id(1)

    @pl.when(_expert_changed(te_ref, i))
    def _():
        wgb_ref[...] = wg_ref[0].astype(BF16)
        wlb_ref[...] = wl_ref[0].astype(BF16)

    @pl.when(tv_ref[i] == 1)
    def _():
        x = x_ref[...].astype(BF16)
        hg = _dot(x, wgb_ref[...]) + bg_ref[0]
        hl = _dot(x, wlb_ref[...]) + bl_ref[0]
        h_ref[...] = _swiglu(hg, hl).astype(h_ref.dtype)

    @pl.when(tv_ref[i] == 0)
    def _():
        h_ref[...] = jnp.zeros(h_ref.shape, h_ref.dtype)


def _gemm2_body(te_ref, tv_ref, h_ref, w_ref, b_ref, y_ref, wb_ref):
    i = pl.program_id(1)

    @pl.when(_expert_changed(te_ref, i))
    def _():
        wb_ref[...] = w_ref[0].astype(BF16)

    @pl.when(tv_ref[i] == 1)
    def _():
        y_ref[...] = _dot(h_ref[...], wb_ref[...]) + b_ref[0]

    @pl.when(tv_ref[i] == 0)
    def _():
        y_ref[...] = jnp.zeros(y_ref.shape, y_ref.dtype)


def _expert_gemms(xs, tile_expert, tile_valid, w1, b1, w2, b2, *, tm, tf, tn):
    n_rows, d = xs.shape
    n_exp, _, f2 = w1.shape
    f = f2 // 2
    tf, tn = min(tf, f), min(tn, d)
    nf = f // tf
    n_tiles = n_rows // tm
    b1r = b1.reshape(n_exp, 1, f2)
    b2r = b2.reshape(n_exp, 1, d)
    grid1 = pltpu.PrefetchScalarGridSpec(
        num_scalar_prefetch=2, grid=(nf, n_tiles),
        in_specs=[pl.BlockSpec((tm, d), lambda j, i, te, tv: (i, 0)),
                  pl.BlockSpec((1, d, tf), lambda j, i, te, tv: (te[i], 0, j)),
                  pl.BlockSpec((1, d, tf), lambda j, i, te, tv: (te[i], 0, nf + j)),
                  pl.BlockSpec((1, 1, tf), lambda j, i, te, tv: (te[i], 0, j)),
                  pl.BlockSpec((1, 1, tf), lambda j, i, te, tv: (te[i], 0, nf + j))],
        out_specs=pl.BlockSpec((tm, tf), lambda j, i, te, tv: (i, j)),
        scratch_shapes=[pltpu.VMEM((d, tf), BF16), pltpu.VMEM((d, tf), BF16)])
    hidden = pl.pallas_call(
        _gemm1_body, out_shape=jax.ShapeDtypeStruct((n_rows, f), BF16), grid_spec=grid1,
        compiler_params=_params("arbitrary", "arbitrary"), name="moe_gemm1",
    )(tile_expert, tile_valid, xs, w1, w1, b1r, b1r)
    grid2 = pltpu.PrefetchScalarGridSpec(
        num_scalar_prefetch=2, grid=(d // tn, n_tiles),
        in_specs=[pl.BlockSpec((tm, f), lambda j, i, te, tv: (i, 0)),
                  pl.BlockSpec((1, f, tn), lambda j, i, te, tv: (te[i], 0, j)),
                  pl.BlockSpec((1, 1, tn), lambda j, i, te, tv: (te[i], 0, j))],
        out_specs=pl.BlockSpec((tm, tn), lambda j, i, te, tv: (i, j)),
        scratch_shapes=[pltpu.VMEM((f, tn), BF16)])
    return pl.pallas_call(
        _gemm2_body, out_shape=jax.ShapeDtypeStruct((n_rows, d), F32), grid_spec=grid2,
        compiler_params=_params("arbitrary", "arbitrary"), name="moe_gemm2",
    )(tile_expert, tile_valid, hidden, w2, b2r)


def _combine_body(pos_ref, y_ref, gates_ref, h_ref, nw_ref, o_ref, buf_ref, sem, *, tb, normalize):
    base = pl.program_id(0) * tb * TOP_K

    def issue(r, carry):
        _row_copy(y_ref, pos_ref[base + r], buf_ref.at[r % TOP_K], r // TOP_K, sem).start()
        return carry

    def drain(r, carry):
        _row_copy(y_ref, 0, buf_ref.at[0], 0, sem).wait()
        return carry

    lax.fori_loop(0, tb * TOP_K, issue, 0)
    lax.fori_loop(0, tb * TOP_K, drain, 0)
    gates = gates_ref[...]
    out = h_ref[...]
    for k in range(TOP_K):
        out = out + gates[:, k:k + 1] * buf_ref[k]
    if normalize:
        ms = jnp.mean(out * out, axis=-1, keepdims=True)
        out = out * lax.rsqrt(ms + EPS) * nw_ref[...]
    o_ref[...] = out


def _combine(pos_flat, y, gates, h1, final_w, tb, normalize):
    t, d = h1.shape
    grid_spec = pltpu.PrefetchScalarGridSpec(
        num_scalar_prefetch=1, grid=(t // tb,),
        in_specs=[pl.BlockSpec(memory_space=pl.ANY),
                  pl.BlockSpec((tb, LANES), lambda i, pos: (i, 0)),
                  pl.BlockSpec((tb, d), lambda i, pos: (i, 0)),
                  pl.BlockSpec((1, d), lambda i, pos: (0, 0))],
        out_specs=pl.BlockSpec((tb, d), lambda i, pos: (i, 0)),
        scratch_shapes=[pltpu.VMEM((TOP_K, tb, d), F32), pltpu.SemaphoreType.DMA])
    return pl.pallas_call(
        functools.partial(_combine_body, tb=tb, normalize=normalize),
        out_shape=jax.ShapeDtypeStruct((t, d), F32),
        grid_spec=grid_spec,
        compiler_params=_params("arbitrary"),
        name="moe_combine",
    )(pos_flat, y, gates, h1, final_w.reshape(1, d))


def _moe(h1, norm_w, router_w, router_b, w1, b1, w2, b2, final_w, normalize, *, tm_router, tm_expert,
         tf, tn, rows_per_step, tb):
    t, d = h1.shape
    tn_tokens, ids, gates, rank, cnt = _router(h1, norm_w, router_w, router_b, tm_router)
    counts = cnt[0, :N_EXPERTS].astype(I32)
    padded = (counts + tm_expert - 1) // tm_expert * tm_expert
    seg_end = jnp.cumsum(padded)
    seg_start = seg_end - padded
    pos = (seg_start[ids[:, :TOP_K]] + rank[:, :TOP_K]).reshape(-1)
    n_rows = t * TOP_K + N_EXPERTS * tm_expert
    n_tiles = n_rows // tm_expert
    tile_ids = jnp.arange(n_tiles, dtype=I32)
    tile_expert = jnp.minimum(jnp.searchsorted(seg_end // tm_expert, tile_ids, side="right"),
                              N_EXPERTS - 1).astype(I32)
    tile_valid = (tile_ids < seg_end[-1] // tm_expert).astype(I32)
    xs = _dispatch(pos, tn_tokens, n_rows, rows_per_step)
    y = _expert_gemms(xs, tile_expert, tile_valid, w1, b1, w2, b2, tm=tm_expert, tf=tf, tn=tn)
    return _combine(pos, y, gates, h1, final_w, tb, normalize)


def _layer(x2, batch, seq, w_in, conv_w, conv_b, dt_bias, a_log, d_skip, ssd_norm_w, w_ssd_branch, w_attn_branch,
           w_out, norm_mix_w, norm_ffn_w, router_w, router_b, w1, b1, w2, b2, final_w, normalize):
    t, d = x2.shape
    ssd_inner = dt_bias.shape[0] * SSD_HEAD_DIM
    conv_dim = conv_w.shape[1]
    attn_inner = w_attn_branch.shape[0]
    idx_q = IDX_HEADS * IDX_DIM
    tm = min(512, seq)

    sizes = (ssd_inner, conv_dim, dt_bias.shape[0], attn_inner, attn_inner, attn_inner,
             idx_q, IDX_DIM, IDX_HEADS, d, d)
    offs = [0]
    for s in sizes:
        offs.append(offs[-1] + s)
    col = lambda i: w_in[:, offs[i]:offs[i + 1]]
    w_z, w_xbc, w_dt, w_q, w_k, w_v, w_qi, w_ki, w_wi, w_gs, w_ga = (col(i) for i in range(len(sizes)))
    zeros_k = jnp.zeros_like(w_ki)
    w_ki2 = jnp.concatenate([w_ki, zeros_k, zeros_k, w_ki], axis=1)
    n_misc = w_dt.shape[1] + w_wi.shape[1]
    w_misc = jnp.pad(jnp.concatenate([w_dt, w_wi], axis=1), ((0, 0), (0, LANES - n_misc)))
    b16 = lambda w: w.astype(BF16)

    xn = _rmsnorm(x2, norm_mix_w, BF16, tm)
    mm = functools.partial(_matmul, tm=tm, tn=1024)
    z = mm(xn, b16(w_z), F32, "proj_z")
    xbc = mm(xn, b16(w_xbc), F32, "proj_xbc")
    gate_logits = mm(xn, b16(jnp.concatenate([w_gs, w_ga], axis=1)), F32, "proj_gates")
    misc = mm(xn, b16(w_misc), F32, "proj_misc")
    v = mm(xn, b16(w_v), BF16, "proj_v")
    attn_tabs = _rope_tables(seq, ATTN_HEAD_DIM // 4, ATTN_HEAD_DIM)
    idx_tabs = _rope_tables(seq, IDX_DIM // 4, IDX_DIM)
    qk = _rope_matmul(xn, b16(jnp.concatenate([w_q, w_k], axis=1)), attn_tabs, ATTN_HEAD_DIM // 4, seq,
                      "proj_qk", tm=tm, tn=1024)
    qi = _rope_matmul(xn, b16(w_qi), idx_tabs, IDX_DIM // 4, seq, "proj_qi", tm=tm, tn=512)
    ki = _rope_matmul(xn, b16(w_ki2), idx_tabs, IDX_DIM // 4, seq, "proj_ki", tm=tm, tn=256)

    y_ssd = _ssd(xbc, z, misc, conv_w, conv_b, dt_bias, a_log, d_skip, ssd_norm_w, batch, seq)
    mask = _indexer_mask(qi, ki, misc, batch, seq, w_lane=w_dt.shape[1])
    y_attn = _attention(qk, v, mask, batch, seq, tq=256, tk=256)

    nd = d // min(1024, d)
    g_spec = lambda off: pl.BlockSpec((tm, min(1024, d)), lambda i, j: (i, off * nd + j))
    same = pl.BlockSpec((tm, min(1024, d)), lambda i, j: (i, j))
    part = mm(y_ssd, b16(w_ssd_branch), F32, "merge_ssd", epilogue=_ep_gate, extras=(gate_logits,),
              extra_specs=(g_spec(0),))
    merged = mm(y_attn, b16(w_attn_branch), BF16, "merge_attn", epilogue=_ep_gate_add,
                extras=(gate_logits, part), extra_specs=(g_spec(1), same))
    h1 = mm(merged, b16(w_out), F32, "out_proj", epilogue=_ep_residual, extras=(x2,), extra_specs=(same,))

    return _moe(h1, norm_ffn_w, router_w, router_b, w1, b1, w2, b2, final_w, normalize,
                tm_router=min(256, t), tm_expert=256, tf=512, tn=512,
                rows_per_step=min(2048, t * TOP_K), tb=min(128, t))


def kernel(x, w_in, conv_w, conv_b, dt_bias, a_log, d_skip, ssd_norm_w, w_ssd_branch, w_attn_branch, w_out,
           norm_mix_w, norm_ffn_w, router_w, router_b, w_expert_in, b_expert_in, w_expert_out, b_expert_out,
           final_norm_w):
    batch, seq, d = x.shape
    depth = w_in.shape[0]
    per_layer = (w_in, conv_w, conv_b, dt_bias, a_log, d_skip, ssd_norm_w, w_ssd_branch, w_attn_branch, w_out,
                 norm_mix_w, norm_ffn_w, router_w, router_b, w_expert_in, b_expert_in, w_expert_out, b_expert_out)
    h = x.reshape(batch * seq, d)
    for layer in range(depth):
        h = _layer(h, batch, seq, *(p[layer] for p in per_layer), final_norm_w, layer == depth - 1)
    return h.reshape(batch, seq, d)
```

```python
import functools
import math

import jax
import jax.numpy as jnp
from jax import lax
from jax.experimental import pallas as pl
from jax.experimental.pallas import tpu as pltpu

EPS = 1e-5
SSD_HEADS = 32
SSD_HEAD_DIM = 64
SSD_GROUPS = 4
SSD_STATE = 128
SSD_CONV = 4
SSD_CHUNK = 128
ATTN_HEADS = 16
ATTN_HEAD_DIM = 128
ROPE_THETA = 500000.0
IDX_HEADS = 8
IDX_DIM = 64
IDX_TOPK = 256
Q_BLOCK = 128
N_EXPERTS = 32
TOP_K = 4
D_EXPERT = 2048
SWIGLU_ALPHA = 1.702
SWIGLU_LIMIT = 7.0

LANES = 128
SUBLANES = 8
VMEM_LIMIT_BYTES = 56 * 2**20
INT32_MIN = -2**31
MASKED_LOGIT = -1e30

F32 = jnp.float32
BF16 = jnp.bfloat16
I32 = jnp.int32


def _params(*sem):
    return pltpu.CompilerParams(dimension_semantics=sem, vmem_limit_bytes=VMEM_LIMIT_BYTES)


def _sigmoid(x):
    return 1.0 / (1.0 + jnp.exp(-x))


def _dot(a, b):
    return jnp.dot(a, b, preferred_element_type=F32)


def _dot_nt(a, b):
    return lax.dot_general(a, b, (((1,), (1,)), ((), ())), preferred_element_type=F32)


def _split3(x):
    hi = x.astype(BF16)
    r1 = x - hi.astype(F32)
    mid = r1.astype(BF16)
    lo = (r1 - mid.astype(F32)).astype(BF16)
    return hi, mid, lo


def _rmsnorm_body(x_ref, w_ref, o_ref):
    x = x_ref[...]
    ms = jnp.mean(x * x, axis=-1, keepdims=True)
    o_ref[...] = (x * lax.rsqrt(ms + EPS) * w_ref[...]).astype(o_ref.dtype)


def _rmsnorm(x, w, out_dtype, tm):
    t, d = x.shape
    return pl.pallas_call(
        _rmsnorm_body,
        out_shape=jax.ShapeDtypeStruct((t, d), out_dtype),
        grid=(t // tm,),
        in_specs=[pl.BlockSpec((tm, d), lambda i: (i, 0)), pl.BlockSpec((1, d), lambda i: (0, 0))],
        out_specs=pl.BlockSpec((tm, d), lambda i: (i, 0)),
        compiler_params=_params("parallel"),
        name="rmsnorm",
    )(x, w.reshape(1, d))


def _ep_identity(acc):
    return acc


def _ep_rope(acc, c, s1, s2, *, shift):
    outs = []
    for g in range(acc.shape[1] // LANES):
        o = acc[:, g * LANES:(g + 1) * LANES]
        outs.append(o * c + pltpu.roll(o, LANES - shift, 1) * s1 + pltpu.roll(o, shift, 1) * s2)
    return outs[0] if len(outs) == 1 else jnp.concatenate(outs, axis=1)


def _ep_gate(acc, g):
    return _sigmoid(g) * acc


def _ep_gate_add(acc, g, p):
    return p + _sigmoid(g) * acc


def _ep_residual(acc, x):
    return x + acc


def _mm_body(*refs, epilogue):
    a_ref, w_ref = refs[0], refs[1]
    o_ref = refs[-1]
    acc = _dot(a_ref[...], w_ref[...])
    o_ref[...] = epilogue(acc, *[r[...] for r in refs[2:-1]]).astype(o_ref.dtype)


def _matmul(a, w, out_dtype, name, *, tm, tn, epilogue=_ep_identity, extras=(), extra_specs=()):
    m, k = a.shape
    n = w.shape[1]
    tn = min(tn, n)
    return pl.pallas_call(
        functools.partial(_mm_body, epilogue=epilogue),
        out_shape=jax.ShapeDtypeStruct((m, n), out_dtype),
        grid=(m // tm, n // tn),
        in_specs=[pl.BlockSpec((tm, k), lambda i, j: (i, 0)),
                  pl.BlockSpec((k, tn), lambda i, j: (0, j)), *extra_specs],
        out_specs=pl.BlockSpec((tm, tn), lambda i, j: (i, j)),
        compiler_params=_params("parallel", "arbitrary"),
        name=name,
    )(a, w, *extras)


def _rope_tables(length, rot_dim, head_dim):
    half = rot_dim // 2
    inv = ROPE_THETA ** (-jnp.arange(0, rot_dim, 2, dtype=F32) / rot_dim)
    ang = jnp.arange(length, dtype=F32)[:, None] * inv[None, :]
    cos, sin = jnp.cos(ang), jnp.sin(ang)
    zeros = lambda n: jnp.zeros((length, n), F32)
    c = jnp.concatenate([cos, cos, jnp.ones((length, head_dim - rot_dim), F32)], axis=1)
    s1 = jnp.concatenate([-sin, zeros(head_dim - half)], axis=1)
    s2 = jnp.concatenate([zeros(half), sin, zeros(head_dim - rot_dim)], axis=1)
    reps = LANES // head_dim
    return tuple(jnp.tile(t, (1, reps)) for t in (c, s1, s2))


def _rope_matmul(a, w, tables, rot_dim, seq, name, *, tm, tn):
    nblk = seq // tm
    spec = pl.BlockSpec((tm, LANES), lambda i, j: (i % nblk, 0))
    return _matmul(a, w, BF16, name, tm=tm, tn=tn,
                   epilogue=functools.partial(_ep_rope, shift=rot_dim // 2),
                   extras=tables, extra_specs=(spec, spec, spec))


def _ssd_body(xbc_ref, z_ref, misc_ref, cw_ref, cb_ref, dtb_ref, alog_ref, dsk_ref, nw_ref,
              ltri_ref, exp_ref, o_ref, xext_ref, state_ref, *, inner, groups, heads):
    ch = SSD_CHUNK
    n = SSD_STATE
    gw = inner // groups
    c_idx = pl.program_id(1)
    tail = SSD_CONV - 1

    @pl.when(c_idx == 0)
    def _():
        xext_ref[0:SUBLANES, :] = jnp.zeros((SUBLANES, xext_ref.shape[1]), F32)
        state_ref[...] = jnp.zeros(state_ref.shape, F32)

    @pl.when(c_idx > 0)
    def _():
        xext_ref[0:SUBLANES, :] = xext_ref[ch:ch + SUBLANES, :]

    xext_ref[SUBLANES:SUBLANES + ch, :] = xbc_ref[...]

    conv = cb_ref[...]
    for j in range(SSD_CONV):
        conv = conv + xext_ref[pl.ds(SUBLANES - tail + j, ch), :] * cw_ref[j:j + 1, :]
    xbc = conv * _sigmoid(conv)
    xs = xbc[:, :inner]
    bm = xbc[:, inner:inner + groups * n]
    cm = xbc[:, inner + groups * n:]

    lane = lax.broadcasted_iota(I32, (1, LANES), 1)
    head_lane = lane < heads
    a = jnp.where(head_lane, -jnp.exp(alog_ref[...]), 0.0)
    dtr = misc_ref[...] + dtb_ref[...]
    dt = jnp.maximum(dtr, 0.0) + jnp.log1p(jnp.exp(-jnp.abs(dtr)))
    da = dt * a

    ltri = ltri_ref[...]
    cs = sum(_dot(ltri, p) for p in _split3(da))
    expand = exp_ref[...]
    dt_e = sum(_dot(p, expand) for p in _split3(dt))
    cs_e = sum(_dot(p, expand) for p in _split3(cs))
    cs_last = cs_e[ch - 1:ch, :]
    ecs = jnp.exp(cs_e)
    dte = jnp.exp(cs_last - cs_e)
    chunk_decay = jnp.exp(cs_last)

    xdt = xs * dt_e
    xdt_b = xdt.astype(BF16)
    xw_b = (xdt * dte).astype(BF16)
    cs_t = cs.T

    row = lax.broadcasted_iota(I32, (ch, ch), 0)
    col = lax.broadcasted_iota(I32, (ch, ch), 1)
    causal = row >= col
    first_half = lax.broadcasted_iota(I32, (ch, LANES), 1) < SSD_HEAD_DIM
    heads_per_group = heads // groups

    diag_cols, off_cols = [], []
    for g in range(groups):
        bg = bm[:, g * n:(g + 1) * n]
        cg_b = cm[:, g * n:(g + 1) * n].astype(BF16)
        cb = _dot_nt(cg_b, bg.astype(BF16))
        bg_t = bg.T.astype(BF16)
        for pr in range(heads_per_group // 2):
            h0 = g * heads_per_group + 2 * pr
            xp = xdt_b[:, h0 * SSD_HEAD_DIM:(h0 + 2) * SSD_HEAD_DIM]
            res = []
            for h in (h0, h0 + 1):
                seg = cs[:, h:h + 1] - cs_t[h:h + 1, :]
                decay = jnp.exp(jnp.where(causal, seg, -jnp.inf))
                res.append(_dot((cb * decay).astype(BF16), xp))
            diag_cols.append(jnp.where(first_half, res[0], res[1]))
        prev = state_ref[g]
        off_cols.append(_dot(cg_b, prev.astype(BF16)) * ecs[:, g * gw:(g + 1) * gw])
        states = _dot(bg_t, xw_b[:, g * gw:(g + 1) * gw])
        state_ref[g] = prev * chunk_decay[:, g * gw:(g + 1) * gw] + states

    y = jnp.concatenate(diag_cols, axis=1) + jnp.concatenate(off_cols, axis=1) + xs * dsk_ref[...]

    z = z_ref[...]
    gated = y * (z * _sigmoid(z))
    ms = jnp.mean(gated * gated, axis=-1, keepdims=True)
    o_ref[...] = (gated * lax.rsqrt(ms + EPS) * nw_ref[...]).astype(o_ref.dtype)


def _ssd(xbc, z, misc, conv_w, conv_b, dt_bias, a_log, d_skip, norm_w, batch, seq):
    t, conv_dim = xbc.shape
    inner = z.shape[1]
    heads = dt_bias.shape[0]
    groups = SSD_GROUPS
    ch = SSD_CHUNK
    nc = seq // ch
    pad = lambda v: jnp.pad(v.astype(F32), (0, LANES - heads)).reshape(1, LANES)
    ltri = jnp.tril(jnp.ones((ch, ch), F32)).astype(BF16)
    head_of_lane = jnp.arange(inner) // SSD_HEAD_DIM
    expand = (jnp.arange(LANES)[:, None] == head_of_lane[None, :]).astype(BF16)
    d_exp = jnp.repeat(d_skip.astype(F32), SSD_HEAD_DIM).reshape(1, inner)
    row = lambda b, c: (b * nc + c, 0)
    const = lambda b, c: (0, 0)
    return pl.pallas_call(
        functools.partial(_ssd_body, inner=inner, groups=groups, heads=heads),
        out_shape=jax.ShapeDtypeStruct((t, inner), BF16),
        grid=(batch, nc),
        in_specs=[pl.BlockSpec((ch, conv_dim), row), pl.BlockSpec((ch, inner), row),
                  pl.BlockSpec((ch, LANES), row),
                  pl.BlockSpec((SSD_CONV, conv_dim), const), pl.BlockSpec((1, conv_dim), const),
                  pl.BlockSpec((1, LANES), const), pl.BlockSpec((1, LANES), const),
                  pl.BlockSpec((1, inner), const), pl.BlockSpec((1, inner), const),
                  pl.BlockSpec((ch, ch), const), pl.BlockSpec((LANES, inner), const)],
        out_specs=pl.BlockSpec((ch, inner), row),
        scratch_shapes=[pltpu.VMEM((SUBLANES + ch, conv_dim), F32),
                        pltpu.VMEM((groups, SSD_STATE, inner // groups), F32)],
        compiler_params=_params("arbitrary", "arbitrary"),
        name="ssd_scan",
    )(xbc, z, misc, conv_w, conv_b.reshape(1, conv_dim), pad(dt_bias), pad(a_log), d_exp,
      norm_w.reshape(1, inner), ltri, expand)


def _indexer_body(qi_ref, ki_ref, misc_ref, u_ref, mask_ref, keys_ref, *, seq, kc, topk, w_lane, idx_scale):
    qb = pl.program_id(1)
    tq = Q_BLOCK
    q0 = qb * tq
    n_chunks = (q0 + tq + kc - 1) // kc
    qpos = q0 + lax.broadcasted_iota(I32, (tq, kc), 0)
    kofs = lax.broadcasted_iota(I32, (tq, kc), 1)
    w = misc_ref[...]
    qi = qi_ref[...]

    def score_chunk(c, carry):
        off = pl.multiple_of(c * kc, kc)
        k_lo = ki_ref[pl.ds(off, kc), 0:LANES]
        k_hi = ki_ref[pl.ds(off, kc), LANES:2 * LANES]
        s = jnp.zeros((tq, kc), F32)
        for j in range(IDX_HEADS // 2):
            qj = qi[:, j * LANES:(j + 1) * LANES]
            w0 = w[:, w_lane + 2 * j:w_lane + 2 * j + 1]
            w1 = w[:, w_lane + 2 * j + 1:w_lane + 2 * j + 2]
            s = s + w0 * jnp.maximum(_dot_nt(qj, k_lo), 0.0) + w1 * jnp.maximum(_dot_nt(qj, k_hi), 0.0)
        s = s * idx_scale
        bits = lax.bitcast_convert_type(s, I32)
        key = jnp.where(bits >= 0, bits, bits ^ jnp.int32(0x7FFFFFFF))
        key = jnp.where(off + kofs <= qpos, key, jnp.int32(INT32_MIN))
        keys_ref[:, pl.ds(off, kc)] = key
        return carry

    lax.fori_loop(0, n_chunks, score_chunk, 0)

    def count(pred):
        def body(c, acc):
            k = keys_ref[:, pl.ds(pl.multiple_of(c * kc, kc), kc)]
            m = jnp.where(pred(k), 1, 0).astype(I32)
            for j in range(kc // LANES):
                acc = acc + m[:, j * LANES:(j + 1) * LANES]
            return acc
        acc = lax.fori_loop(0, n_chunks, body, jnp.zeros((tq, LANES), I32))
        return jnp.sum(acc, axis=1, keepdims=True)

    def bit_cond(state):
        i, _, n_ge = state
        return (i < 32) & (jnp.max(n_ge) > topk)

    def bit_body(state):
        i, t_u, n_ge = state
        cand_u = t_u | jnp.left_shift(jnp.int32(1), 31 - i)
        cand_s = cand_u ^ jnp.int32(INT32_MIN)
        cnt = count(lambda k: k >= cand_s)
        keep = cnt >= topk
        return i + 1, jnp.where(keep, cand_u, t_u), jnp.where(keep, cnt, n_ge)

    n_admissible = q0 + lax.broadcasted_iota(I32, (tq, 1), 0) + 1
    _, t_u, _ = lax.while_loop(bit_cond, bit_body, (jnp.int32(0), jnp.zeros((tq, 1), I32), n_admissible))
    t_s = t_u ^ jnp.int32(INT32_MIN)
    n_gt = count(lambda k: k > t_s)
    n_tie = jnp.where(t_u == 0, 0, topk - n_gt).astype(F32)

    def mask_chunk(c, seen):
        off = pl.multiple_of(c * kc, kc)
        k = keys_ref[:, pl.ds(off, kc)]
        tie = k == t_s
        tie_f = jnp.where(tie, 1.0, 0.0)
        rank = seen + _dot(tie_f.astype(BF16), u_ref[...])
        sel = (k > t_s) | (tie & (rank <= n_tie))
        mask_ref[:, pl.ds(off, kc)] = jnp.where(sel, 1, 0).astype(jnp.int8)
        return seen + jnp.sum(tie_f, axis=1, keepdims=True)

    lax.fori_loop(0, n_chunks, mask_chunk, jnp.zeros((tq, 1), F32))

    def zero_chunk(c, carry):
        mask_ref[:, pl.ds(pl.multiple_of(c * kc, kc), kc)] = jnp.zeros((tq, kc), jnp.int8)
        return carry

    lax.fori_loop(n_chunks, seq // kc, zero_chunk, 0)


def _indexer_mask(qi, ki, misc, batch, seq, w_lane):
    t = qi.shape[0]
    tq = Q_BLOCK
    nq = seq // tq
    kc = min(512, seq)
    topk = min(IDX_TOPK, seq // 4)
    upper = jnp.triu(jnp.ones((kc, kc), F32)).astype(BF16)
    idx_scale = (IDX_DIM ** -0.5) * (IDX_HEADS ** -0.5)
    return pl.pallas_call(
        functools.partial(_indexer_body, seq=seq, kc=kc, topk=topk, w_lane=w_lane, idx_scale=idx_scale),
        out_shape=jax.ShapeDtypeStruct((t, seq), jnp.int8),
        grid=(batch, nq),
        in_specs=[pl.BlockSpec((tq, qi.shape[1]), lambda b, q: (b * nq + q, 0)),
                  pl.BlockSpec((seq, ki.shape[1]), lambda b, q: (b, 0)),
                  pl.BlockSpec((tq, LANES), lambda b, q: (b * nq + q, 0)),
                  pl.BlockSpec((kc, kc), lambda b, q: (0, 0))],
        out_specs=pl.BlockSpec((tq, seq), lambda b, q: (b * nq + q, 0)),
        scratch_shapes=[pltpu.VMEM((tq, seq), I32)],
        compiler_params=_params("parallel", "arbitrary"),
        name="indexer_topk_mask",
    )(qi, ki, misc, upper)


def _attn_body(q_ref, k_ref, v_ref, mask_ref, o_ref, acc_ref, m_ref, l_ref, *, tq, tk, heads, scale_log2e):
    qb = pl.program_id(1)
    kb = pl.program_id(2)
    last = ((qb + 1) * tq - 1) // tk
    hd = ATTN_HEAD_DIM

    @pl.when(kb == 0)
    def _():
        acc_ref[...] = jnp.zeros(acc_ref.shape, F32)
        m_ref[...] = jnp.full(m_ref.shape, MASKED_LOGIT, F32)
        l_ref[...] = jnp.zeros(l_ref.shape, F32)

    @pl.when(kb <= last)
    def _():
        bias = jnp.where(mask_ref[...].astype(I32) != 0, 0.0, MASKED_LOGIT)
        for h in range(heads):
            cols = slice(h * hd, (h + 1) * hd)
            s = _dot_nt(q_ref[:, cols], k_ref[:, cols]) * scale_log2e + bias
            m_prev = m_ref[h]
            m_new = jnp.maximum(m_prev, jnp.max(s, axis=1, keepdims=True))
            alpha = jnp.exp2(m_prev - m_new)
            p = jnp.exp2(s - m_new[:, 0:1])
            l_ref[h] = alpha * l_ref[h] + jnp.sum(p, axis=1, keepdims=True)
            acc_ref[:, cols] = acc_ref[:, cols] * alpha + _dot(p.astype(BF16), v_ref[:, cols])
            m_ref[h] = m_new

    @pl.when(kb == pl.num_programs(2) - 1)
    def _():
        for h in range(heads):
            cols = slice(h * hd, (h + 1) * hd)
            o_ref[:, cols] = (acc_ref[:, cols] / l_ref[h]).astype(o_ref.dtype)


def _attention(qk, v, mask, batch, seq, *, tq, tk):
    t, inner = v.shape
    heads = inner // ATTN_HEAD_DIM
    tq, tk = min(tq, seq), min(tk, seq)
    nq, nk = seq // tq, seq // tk
    last = lambda q: ((q + 1) * tq - 1) // tk
    return pl.pallas_call(
        functools.partial(_attn_body, tq=tq, tk=tk, heads=heads, scale_log2e=ATTN_HEAD_DIM ** -0.5 * math.log2(math.e)),
        out_shape=jax.ShapeDtypeStruct((t, inner), BF16),
        grid=(batch, nq, nk),
        in_specs=[pl.BlockSpec((tq, inner), lambda b, q, k: (b * nq + q, 0)),
                  pl.BlockSpec((tk, inner), lambda b, q, k: (b * nk + jnp.minimum(k, last(q)), 1)),
                  pl.BlockSpec((tk, inner), lambda b, q, k: (b * nk + jnp.minimum(k, last(q)), 0)),
                  pl.BlockSpec((tq, tk), lambda b, q, k: (b * nq + q, jnp.minimum(k, last(q))))],
        out_specs=pl.BlockSpec((tq, inner), lambda b, q, k: (b * nq + q, 0)),
        scratch_shapes=[pltpu.VMEM((tq, inner), F32),
                        pltpu.VMEM((heads, tq, LANES), F32),
                        pltpu.VMEM((heads, tq, LANES), F32)],
        compiler_params=_params("parallel", "parallel", "arbitrary"),
        name="masked_attention",
    )(qk, qk, v, mask)


def _pack_bf16_pairs(x):
    h = x.shape[1] // 2
    hi = lax.bitcast_convert_type(x[:, :h].astype(F32), jnp.uint32)
    lo = lax.bitcast_convert_type(x[:, h:].astype(F32), jnp.uint32)
    return lax.bitcast_convert_type(hi | (lo >> 16), I32)


def _unpack_bf16_pairs(p):
    u = lax.bitcast_convert_type(p, jnp.uint32)
    hi = lax.bitcast_convert_type(u & jnp.uint32(0xFFFF0000), F32).astype(BF16)
    lo = lax.bitcast_convert_type(u << 16, F32).astype(BF16)
    return hi, lo


def _router_body(h_ref, nw_ref, rw_ref, rb_ref, ltri_ref, tn_ref, ids_ref, gates_ref, rank_ref, cnt_ref,
                 carry_ref):
    @pl.when(pl.program_id(0) == 0)
    def _():
        carry_ref[...] = jnp.zeros(carry_ref.shape, F32)

    x = h_ref[...]
    ms = jnp.mean(x * x, axis=-1, keepdims=True)
    tn = (x * lax.rsqrt(ms + EPS) * nw_ref[...]).astype(BF16)
    tn_ref[...] = _pack_bf16_pairs(tn)
    tm = x.shape[0]
    lane = lax.broadcasted_iota(I32, (tm, LANES), 1)
    logits = _dot(tn, rw_ref[...]) + rb_ref[...]
    work = jnp.where(lane < N_EXPERTS, logits, -jnp.inf)
    vals, hits = [], []
    for _ in range(TOP_K):
        mx = jnp.max(work, axis=1, keepdims=True)
        idx = jnp.min(jnp.where(work == mx, lane, LANES), axis=1, keepdims=True)
        hit = lane == idx
        work = jnp.where(hit, -jnp.inf, work)
        vals.append(mx)
        hits.append(hit)
    es = [jnp.exp(v - vals[0]) for v in vals]
    tot = sum(es)
    onehot = sum(jnp.where(h, 1.0, 0.0) for h in hits)
    before = _dot(ltri_ref[...], onehot.astype(BF16)) + carry_ref[...]
    lane_f = lane.astype(F32)
    ids = jnp.zeros((tm, LANES), F32)
    gates = jnp.zeros((tm, LANES), F32)
    ranks = jnp.zeros((tm, LANES), F32)
    for k in range(TOP_K):
        slot = lane == k
        ids = jnp.where(slot, jnp.sum(jnp.where(hits[k], lane_f, 0.0), axis=1, keepdims=True), ids)
        gates = jnp.where(slot, es[k] / tot, gates)
        ranks = jnp.where(slot, jnp.sum(jnp.where(hits[k], before, 0.0), axis=1, keepdims=True), ranks)
    ids_ref[...] = ids.astype(I32)
    gates_ref[...] = gates
    rank_ref[...] = ranks.astype(I32)
    carry_ref[...] = carry_ref[...] + jnp.sum(onehot, axis=0, keepdims=True)
    cnt_ref[...] = carry_ref[...]


def _router(h1, norm_w, router_w, router_b, tm):
    t, d = h1.shape
    rw = jnp.pad(router_w, ((0, 0), (0, LANES - N_EXPERTS))).astype(BF16)
    rb = jnp.pad(router_b.astype(F32), (0, LANES - N_EXPERTS)).reshape(1, LANES)
    ltri = jnp.tril(jnp.ones((tm, tm), F32), -1).astype(BF16)
    row = lambda i: (i, 0)
    const = lambda i: (0, 0)
    return pl.pallas_call(
        _router_body,
        out_shape=(jax.ShapeDtypeStruct((t, d // 2), I32), jax.ShapeDtypeStruct((t, LANES), I32),
                   jax.ShapeDtypeStruct((t, LANES), F32), jax.ShapeDtypeStruct((t, LANES), I32),
                   jax.ShapeDtypeStruct((1, LANES), F32)),
        grid=(t // tm,),
        in_specs=[pl.BlockSpec((tm, d), row), pl.BlockSpec((1, d), const), pl.BlockSpec((d, LANES), const),
                  pl.BlockSpec((1, LANES), const), pl.BlockSpec((tm, tm), const)],
        out_specs=(pl.BlockSpec((tm, d // 2), row), pl.BlockSpec((tm, LANES), row), pl.BlockSpec((tm, LANES), row),
                   pl.BlockSpec((tm, LANES), row), pl.BlockSpec((1, LANES), const)),
        scratch_shapes=[pltpu.VMEM((1, LANES), F32)],
        compiler_params=_params("arbitrary"),
        name="moe_router",
    )(h1, norm_w.reshape(1, d), rw, rb, ltri)


def _row_copy(src, src_row, dst, dst_row, sem):
    return pltpu.make_async_copy(src.at[pl.ds(src_row, 1)], dst.at[pl.ds(dst_row, 1)], sem)


def _dispatch_body(pos_ref, tn_ref, init_ref, xs_ref, sem, *, tb):
    del init_ref
    base = pl.program_id(0) * tb * TOP_K

    def issue(r, carry):
        _row_copy(tn_ref, r // TOP_K, xs_ref, pos_ref[base + r], sem).start()
        return carry

    def drain(r, carry):
        _row_copy(tn_ref, 0, xs_ref, 0, sem).wait()
        return carry

    lax.fori_loop(0, tb * TOP_K, issue, 0, unroll=8)
    lax.fori_loop(0, tb * TOP_K, drain, 0, unroll=8)


def _dispatch(pos_flat, tn, n_rows, tb):
    t, d = tn.shape
    grid_spec = pltpu.PrefetchScalarGridSpec(
        num_scalar_prefetch=1, grid=(t // tb,),
        in_specs=[pl.BlockSpec((tb, d), lambda i, pos: (i, 0)), pl.BlockSpec(memory_space=pl.ANY)],
        out_specs=pl.BlockSpec(memory_space=pl.ANY),
        scratch_shapes=[pltpu.SemaphoreType.DMA])
    return pl.pallas_call(
        functools.partial(_dispatch_body, tb=tb),
        out_shape=jax.ShapeDtypeStruct((n_rows, d), tn.dtype),
        grid_spec=grid_spec,
        input_output_aliases={2: 0},
        compiler_params=_params("arbitrary"),
        name="moe_dispatch",
    )(pos_flat, tn, jnp.zeros((n_rows, d), tn.dtype))


def _swiglu(hg, hl):
    glu = jnp.minimum(hg, SWIGLU_LIMIT)
    lin = jnp.clip(hl, -SWIGLU_LIMIT, SWIGLU_LIMIT)
    return glu * _sigmoid(SWIGLU_ALPHA * glu) * (lin + 1.0)


def _expert_changed(te_ref, i):
    return (i == 0) | (te_ref[i] != te_ref[jnp.maximum(i - 1, 0)])


def _gemm1_body(te_ref, tv_ref, x_ref, wg_ref, wl_ref, bg_ref, bl_ref, h_ref, wgb_ref, wlb_ref):
    i = pl.program_id(1)

    @pl.when(_expert_changed(te_ref, i))
    def _():
        wgb_ref[...] = wg_ref[0].astype(BF16)
        wlb_ref[...] = wl_ref[0].astype(BF16)

    @pl.when(tv_ref[i] == 1)
    def _():
        x_hi, x_lo = _unpack_bf16_pairs(x_ref[...])
        half = x_hi.shape[1]
        hg = _dot(x_hi, wgb_ref[0:half, :]) + _dot(x_lo, wgb_ref[half:, :]) + bg_ref[0]
        hl = _dot(x_hi, wlb_ref[0:half, :]) + _dot(x_lo, wlb_ref[half:, :]) + bl_ref[0]
        h_ref[...] = _swiglu(hg, hl).astype(h_ref.dtype)

    @pl.when(tv_ref[i] == 0)
    def _():
        h_ref[...] = jnp.zeros(h_ref.shape, h_ref.dtype)


def _gemm2_body(te_ref, tv_ref, h_ref, w_ref, b_ref, y_ref, wb_ref):
    i = pl.program_id(1)

    @pl.when(_expert_changed(te_ref, i))
    def _():
        wb_ref[...] = w_ref[0].astype(BF16)

    @pl.when(tv_ref[i] == 1)
    def _():
        y_ref[...] = _dot(h_ref[...], wb_ref[...]) + b_ref[0]

    @pl.when(tv_ref[i] == 0)
    def _():
        y_ref[...] = jnp.zeros(y_ref.shape, y_ref.dtype)


def _expert_gemms(xs, tile_expert, tile_valid, w1, b1, w2, b2, *, tm, tf, tn):
    n_rows = xs.shape[0]
    n_exp, d, f2 = w1.shape
    f = f2 // 2
    tf, tn = min(tf, f), min(tn, d)
    nf = f // tf
    n_tiles = n_rows // tm
    b1r = b1.reshape(n_exp, 1, f2)
    b2r = b2.reshape(n_exp, 1, d)
    grid1 = pltpu.PrefetchScalarGridSpec(
        num_scalar_prefetch=2, grid=(nf, n_tiles),
        in_specs=[pl.BlockSpec((tm, d // 2), lambda j, i, te, tv: (i, 0)),
                  pl.BlockSpec((1, d, tf), lambda j, i, te, tv: (te[i], 0, j)),
                  pl.BlockSpec((1, d, tf), lambda j, i, te, tv: (te[i], 0, nf + j)),
                  pl.BlockSpec((1, 1, tf), lambda j, i, te, tv: (te[i], 0, j)),
                  pl.BlockSpec((1, 1, tf), lambda j, i, te, tv: (te[i], 0, nf + j))],
        out_specs=pl.BlockSpec((tm, tf), lambda j, i, te, tv: (i, j)),
        scratch_shapes=[pltpu.VMEM((d, tf), BF16), pltpu.VMEM((d, tf), BF16)])
    hidden = pl.pallas_call(
        _gemm1_body, out_shape=jax.ShapeDtypeStruct((n_rows, f), BF16), grid_spec=grid1,
        compiler_params=_params("arbitrary", "arbitrary"), name="moe_gemm1",
    )(tile_expert, tile_valid, xs, w1, w1, b1r, b1r)
    grid2 = pltpu.PrefetchScalarGridSpec(
        num_scalar_prefetch=2, grid=(d // tn, n_tiles),
        in_specs=[pl.BlockSpec((tm, f), lambda j, i, te, tv: (i, 0)),
                  pl.BlockSpec((1, f, tn), lambda j, i, te, tv: (te[i], 0, j)),
                  pl.BlockSpec((1, 1, tn), lambda j, i, te, tv: (te[i], 0, j))],
        out_specs=pl.BlockSpec((tm, tn), lambda j, i, te, tv: (i, j)),
        scratch_shapes=[pltpu.VMEM((f, tn), BF16)])
    return pl.pallas_call(
        _gemm2_body, out_shape=jax.ShapeDtypeStruct((n_rows, d), F32), grid_spec=grid2,
        compiler_params=_params("arbitrary", "arbitrary"), name="moe_gemm2",
    )(tile_expert, tile_valid, hidden, w2, b2r)


def _combine_body(pos_ref, y_ref, gates_ref, h_ref, nw_ref, o_ref, buf_ref, sem, *, tb, normalize):
    base = pl.program_id(0) * tb * TOP_K

    def issue(r, carry):
        _row_copy(y_ref, pos_ref[base + r], buf_ref.at[r % TOP_K], r // TOP_K, sem).start()
        return carry

    def drain(r, carry):
        _row_copy(y_ref, 0, buf_ref.at[0], 0, sem).wait()
        return carry

    lax.fori_loop(0, tb * TOP_K, issue, 0, unroll=8)
    lax.fori_loop(0, tb * TOP_K, drain, 0, unroll=8)
    gates = gates_ref[...]
    out = h_ref[...]
    for k in range(TOP_K):
        out = out + gates[:, k:k + 1] * buf_ref[k]
    if normalize:
        ms = jnp.mean(out * out, axis=-1, keepdims=True)
        out = out * lax.rsqrt(ms + EPS) * nw_ref[...]
    o_ref[...] = out


def _combine(pos_flat, y, gates, h1, final_w, tb, normalize):
    t, d = h1.shape
    grid_spec = pltpu.PrefetchScalarGridSpec(
        num_scalar_prefetch=1, grid=(t // tb,),
        in_specs=[pl.BlockSpec(memory_space=pl.ANY),
                  pl.BlockSpec((tb, LANES), lambda i, pos: (i, 0)),
                  pl.BlockSpec((tb, d), lambda i, pos: (i, 0)),
                  pl.BlockSpec((1, d), lambda i, pos: (0, 0))],
        out_specs=pl.BlockSpec((tb, d), lambda i, pos: (i, 0)),
        scratch_shapes=[pltpu.VMEM((TOP_K, tb, d), F32), pltpu.SemaphoreType.DMA])
    return pl.pallas_call(
        functools.partial(_combine_body, tb=tb, normalize=normalize),
        out_shape=jax.ShapeDtypeStruct((t, d), F32),
        grid_spec=grid_spec,
        compiler_params=_params("arbitrary"),
        name="moe_combine",
    )(pos_flat, y, gates, h1, final_w.reshape(1, d))


def _moe(h1, norm_w, router_w, router_b, w1, b1, w2, b2, final_w, normalize, *, tm_router, tm_expert,
         tf, tn, tb_dispatch, tb):
    t, d = h1.shape
    tn_tokens, ids, gates, rank, cnt = _router(h1, norm_w, router_w, router_b, tm_router)
    counts = cnt[0, :N_EXPERTS].astype(I32)
    padded = (counts + tm_expert - 1) // tm_expert * tm_expert
    seg_end = jnp.cumsum(padded)
    seg_start = seg_end - padded
    pos = (seg_start[ids[:, :TOP_K]] + rank[:, :TOP_K]).reshape(-1)
    n_rows = t * TOP_K + N_EXPERTS * tm_expert
    n_tiles = n_rows // tm_expert
    tile_ids = jnp.arange(n_tiles, dtype=I32)
    tiles_done = jnp.sum((tile_ids[:, None] >= (seg_end // tm_expert)[None, :]).astype(I32), axis=1)
    tile_expert = jnp.minimum(tiles_done, N_EXPERTS - 1)
    tile_valid = (tile_ids < seg_end[-1] // tm_expert).astype(I32)
    xs = _dispatch(pos, tn_tokens, n_rows, tb_dispatch)
    y = _expert_gemms(xs, tile_expert, tile_valid, w1, b1, w2, b2, tm=tm_expert, tf=tf, tn=tn)
    return _combine(pos, y, gates, h1, final_w, tb, normalize)


def _layer(x2, batch, seq, w_in, conv_w, conv_b, dt_bias, a_log, d_skip, ssd_norm_w, w_ssd_branch, w_attn_branch,
           w_out, norm_mix_w, norm_ffn_w, router_w, router_b, w1, b1, w2, b2, final_w, normalize):
    t, d = x2.shape
    ssd_inner = dt_bias.shape[0] * SSD_HEAD_DIM
    conv_dim = conv_w.shape[1]
    attn_inner = w_attn_branch.shape[0]
    idx_q = IDX_HEADS * IDX_DIM
    tm = min(512, seq)

    sizes = (ssd_inner, conv_dim, dt_bias.shape[0], attn_inner, attn_inner, attn_inner,
             idx_q, IDX_DIM, IDX_HEADS, d, d)
    offs = [0]
    for s in sizes:
        offs.append(offs[-1] + s)
    col = lambda i: w_in[:, offs[i]:offs[i + 1]]
    w_z, w_xbc, w_dt, w_q, w_k, w_v, w_qi, w_ki, w_wi, w_gs, w_ga = (col(i) for i in range(len(sizes)))
    zeros_k = jnp.zeros_like(w_ki)
    w_ki2 = jnp.concatenate([w_ki, zeros_k, zeros_k, w_ki], axis=1)
    n_misc = w_dt.shape[1] + w_wi.shape[1]
    w_misc = jnp.pad(jnp.concatenate([w_dt, w_wi], axis=1), ((0, 0), (0, LANES - n_misc)))
    b16 = lambda w: w.astype(BF16)

    xn = _rmsnorm(x2, norm_mix_w, BF16, tm)
    mm = functools.partial(_matmul, tm=tm, tn=1024)
    z = mm(xn, b16(w_z), F32, "proj_z")
    xbc = mm(xn, b16(w_xbc), F32, "proj_xbc")
    gate_logits = mm(xn, b16(jnp.concatenate([w_gs, w_ga], axis=1)), F32, "proj_gates")
    misc = mm(xn, b16(w_misc), F32, "proj_misc")
    v = mm(xn, b16(w_v), BF16, "proj_v")
    attn_tabs = _rope_tables(seq, ATTN_HEAD_DIM // 4, ATTN_HEAD_DIM)
    idx_tabs = _rope_tables(seq, IDX_DIM // 4, IDX_DIM)
    qk = _rope_matmul(xn, b16(jnp.concatenate([w_q, w_k], axis=1)), attn_tabs, ATTN_HEAD_DIM // 4, seq,
                      "proj_qk", tm=tm, tn=1024)
    qi = _rope_matmul(xn, b16(w_qi), idx_tabs, IDX_DIM // 4, seq, "proj_qi", tm=tm, tn=512)
    ki = _rope_matmul(xn, b16(w_ki2), idx_tabs, IDX_DIM // 4, seq, "proj_ki", tm=tm, tn=256)

    y_ssd = _ssd(xbc, z, misc, conv_w, conv_b, dt_bias, a_log, d_skip, ssd_norm_w, batch, seq)
    mask = _indexer_mask(qi, ki, misc, batch, seq, w_lane=w_dt.shape[1])
    y_attn = _attention(qk, v, mask, batch, seq, tq=256, tk=1024)

    nd = d // min(1024, d)
    g_spec = lambda off: pl.BlockSpec((tm, min(1024, d)), lambda i, j: (i, off * nd + j))
    same = pl.BlockSpec((tm, min(1024, d)), lambda i, j: (i, j))
    part = mm(y_ssd, b16(w_ssd_branch), F32, "merge_ssd", epilogue=_ep_gate, extras=(gate_logits,),
              extra_specs=(g_spec(0),))
    merged = mm(y_attn, b16(w_attn_branch), BF16, "merge_attn", epilogue=_ep_gate_add,
                extras=(gate_logits, part), extra_specs=(g_spec(1), same))
    h1 = mm(merged, b16(w_out), F32, "out_proj", epilogue=_ep_residual, extras=(x2,), extra_specs=(same,))

    return _moe(h1, norm_ffn_w, router_w, router_b, w1, b1, w2, b2, final_w, normalize,
                tm_router=min(256, t), tm_expert=256, tf=1024, tn=2048,
                tb_dispatch=min(512, t), tb=min(128, t))


def kernel(x, w_in, conv_w, conv_b, dt_bias, a_log, d_skip, ssd_norm_w, w_ssd_branch, w_attn_branch, w_out,
           norm_mix_w, norm_ffn_w, router_w, router_b, w_expert_in, b_expert_in, w_expert_out, b_expert_out,
           final_norm_w):
    batch, seq, d = x.shape
    depth = w_in.shape[0]
    per_layer = (w_in, conv_w, conv_b, dt_bias, a_log, d_skip, ssd_norm_w, w_ssd_branch, w_attn_branch, w_out,
                 norm_mix_w, norm_ffn_w, router_w, router_b, w_expert_in, b_expert_in, w_expert_out, b_expert_out)
    h = x.reshape(batch * seq, d)
    for layer in range(depth):
        h = _layer(h, batch, seq, *(p[layer] for p in per_layer), final_norm_w, layer == depth - 1)
    return h.reshape(batch, seq, d)
```

```python
import functools
import math

import jax
import jax.numpy as jnp
from jax import lax
from jax.experimental import pallas as pl
from jax.experimental.pallas import tpu as pltpu

EPS = 1e-5
SSD_HEADS = 32
SSD_HEAD_DIM = 64
SSD_GROUPS = 4
SSD_STATE = 128
SSD_CONV = 4
SSD_CHUNK = 128
ATTN_HEADS = 16
ATTN_HEAD_DIM = 128
ROPE_THETA = 500000.0
IDX_HEADS = 8
IDX_DIM = 64
IDX_TOPK = 256
Q_BLOCK = 128
N_EXPERTS = 32
TOP_K = 4
D_EXPERT = 2048
SWIGLU_ALPHA = 1.702
SWIGLU_LIMIT = 7.0

LANES = 128
SUBLANES = 8
VMEM_LIMIT_BYTES = 56 * 2**20
INT32_MIN = -2**31
MASKED_LOGIT = -1e30

F32 = jnp.float32
BF16 = jnp.bfloat16
I32 = jnp.int32


def _params(*sem):
    return pltpu.CompilerParams(dimension_semantics=sem, vmem_limit_bytes=VMEM_LIMIT_BYTES)


def _sigmoid(x):
    return 1.0 / (1.0 + jnp.exp(-x))


def _dot(a, b):
    return jnp.dot(a, b, preferred_element_type=F32)


def _dot_nt(a, b):
    return lax.dot_general(a, b, (((1,), (1,)), ((), ())), preferred_element_type=F32)


def _split3(x):
    hi = x.astype(BF16)
    r1 = x - hi.astype(F32)
    mid = r1.astype(BF16)
    lo = (r1 - mid.astype(F32)).astype(BF16)
    return hi, mid, lo


def _rmsnorm_body(x_ref, w_ref, o_ref):
    x = x_ref[...]
    ms = jnp.mean(x * x, axis=-1, keepdims=True)
    o_ref[...] = (x * lax.rsqrt(ms + EPS) * w_ref[...]).astype(o_ref.dtype)


def _rmsnorm(x, w, out_dtype, tm):
    t, d = x.shape
    return pl.pallas_call(
        _rmsnorm_body,
        out_shape=jax.ShapeDtypeStruct((t, d), out_dtype),
        grid=(t // tm,),
        in_specs=[pl.BlockSpec((tm, d), lambda i: (i, 0)), pl.BlockSpec((1, d), lambda i: (0, 0))],
        out_specs=pl.BlockSpec((tm, d), lambda i: (i, 0)),
        compiler_params=_params("parallel"),
        name="rmsnorm",
    )(x, w.reshape(1, d))


def _ep_identity(acc):
    return acc


def _ep_rope(acc, c, s1, s2, *, shift):
    outs = []
    for g in range(acc.shape[1] // LANES):
        o = acc[:, g * LANES:(g + 1) * LANES]
        outs.append(o * c + pltpu.roll(o, LANES - shift, 1) * s1 + pltpu.roll(o, shift, 1) * s2)
    return outs[0] if len(outs) == 1 else jnp.concatenate(outs, axis=1)


def _ep_gate(acc, g):
    return _sigmoid(g) * acc


def _ep_gate_add(acc, g, p):
    return p + _sigmoid(g) * acc


def _ep_residual(acc, x):
    return x + acc


def _mm_body(*refs, epilogue):
    a_ref, w_ref = refs[0], refs[1]
    o_ref = refs[-1]
    acc = _dot(a_ref[...], w_ref[...])
    o_ref[...] = epilogue(acc, *[r[...] for r in refs[2:-1]]).astype(o_ref.dtype)


def _matmul(a, w, out_dtype, name, *, tm, tn, epilogue=_ep_identity, extras=(), extra_specs=()):
    m, k = a.shape
    n = w.shape[1]
    tn = min(tn, n)
    return pl.pallas_call(
        functools.partial(_mm_body, epilogue=epilogue),
        out_shape=jax.ShapeDtypeStruct((m, n), out_dtype),
        grid=(m // tm, n // tn),
        in_specs=[pl.BlockSpec((tm, k), lambda i, j: (i, 0)),
                  pl.BlockSpec((k, tn), lambda i, j: (0, j)), *extra_specs],
        out_specs=pl.BlockSpec((tm, tn), lambda i, j: (i, j)),
        compiler_params=_params("parallel", "arbitrary"),
        name=name,
    )(a, w, *extras)


def _rope_tables(length, rot_dim, head_dim):
    half = rot_dim // 2
    inv = ROPE_THETA ** (-jnp.arange(0, rot_dim, 2, dtype=F32) / rot_dim)
    ang = jnp.arange(length, dtype=F32)[:, None] * inv[None, :]
    cos, sin = jnp.cos(ang), jnp.sin(ang)
    zeros = lambda n: jnp.zeros((length, n), F32)
    c = jnp.concatenate([cos, cos, jnp.ones((length, head_dim - rot_dim), F32)], axis=1)
    s1 = jnp.concatenate([-sin, zeros(head_dim - half)], axis=1)
    s2 = jnp.concatenate([zeros(half), sin, zeros(head_dim - rot_dim)], axis=1)
    reps = LANES // head_dim
    return tuple(jnp.tile(t, (1, reps)) for t in (c, s1, s2))


def _rope_matmul(a, w, tables, rot_dim, seq, name, *, tm, tn):
    nblk = seq // tm
    spec = pl.BlockSpec((tm, LANES), lambda i, j: (i % nblk, 0))
    return _matmul(a, w, BF16, name, tm=tm, tn=tn,
                   epilogue=functools.partial(_ep_rope, shift=rot_dim // 2),
                   extras=tables, extra_specs=(spec, spec, spec))


def _ssd_body(xbc_ref, z_ref, misc_ref, cw_ref, cb_ref, dtb_ref, alog_ref, dsk_ref, nw_ref,
              ltri_ref, exp_ref, o_ref, xext_ref, state_ref, *, inner, groups, heads):
    ch = SSD_CHUNK
    n = SSD_STATE
    gw = inner // groups
    c_idx = pl.program_id(1)
    tail = SSD_CONV - 1

    @pl.when(c_idx == 0)
    def _():
        xext_ref[0:SUBLANES, :] = jnp.zeros((SUBLANES, xext_ref.shape[1]), F32)
        state_ref[...] = jnp.zeros(state_ref.shape, F32)

    @pl.when(c_idx > 0)
    def _():
        xext_ref[0:SUBLANES, :] = xext_ref[ch:ch + SUBLANES, :]

    xext_ref[SUBLANES:SUBLANES + ch, :] = xbc_ref[...]

    conv = cb_ref[...]
    for j in range(SSD_CONV):
        conv = conv + xext_ref[pl.ds(SUBLANES - tail + j, ch), :] * cw_ref[j:j + 1, :]
    xbc = conv * _sigmoid(conv)
    xs = xbc[:, :inner]
    bm = xbc[:, inner:inner + groups * n]
    cm = xbc[:, inner + groups * n:]

    lane = lax.broadcasted_iota(I32, (1, LANES), 1)
    head_lane = lane < heads
    a = jnp.where(head_lane, -jnp.exp(alog_ref[...]), 0.0)
    dtr = misc_ref[...] + dtb_ref[...]
    dt = jnp.maximum(dtr, 0.0) + jnp.log1p(jnp.exp(-jnp.abs(dtr)))
    da = dt * a

    ltri = ltri_ref[...]
    cs = sum(_dot(ltri, p) for p in _split3(da))
    expand = exp_ref[...]
    dt_e = sum(_dot(p, expand) for p in _split3(dt))
    cs_e = sum(_dot(p, expand) for p in _split3(cs))
    cs_last = cs_e[ch - 1:ch, :]
    ecs = jnp.exp(cs_e)
    dte = jnp.exp(cs_last - cs_e)
    chunk_decay = jnp.exp(cs_last)

    xdt = xs * dt_e
    xdt_b = xdt.astype(BF16)
    xw_b = (xdt * dte).astype(BF16)
    cs_t = cs.T

    row = lax.broadcasted_iota(I32, (ch, ch), 0)
    col = lax.broadcasted_iota(I32, (ch, ch), 1)
    causal = row >= col
    first_half = lax.broadcasted_iota(I32, (ch, LANES), 1) < SSD_HEAD_DIM
    heads_per_group = heads // groups

    diag_cols, off_cols = [], []
    for g in range(groups):
        bg = bm[:, g * n:(g + 1) * n]
        cg_b = cm[:, g * n:(g + 1) * n].astype(BF16)
        cb = _dot_nt(cg_b, bg.astype(BF16))
        bg_t = bg.T.astype(BF16)
        for pr in range(heads_per_group // 2):
            h0 = g * heads_per_group + 2 * pr
            xp = xdt_b[:, h0 * SSD_HEAD_DIM:(h0 + 2) * SSD_HEAD_DIM]
            res = []
            for h in (h0, h0 + 1):
                seg = cs[:, h:h + 1] - cs_t[h:h + 1, :]
                decay = jnp.exp(jnp.where(causal, seg, -jnp.inf))
                res.append(_dot((cb * decay).astype(BF16), xp))
            diag_cols.append(jnp.where(first_half, res[0], res[1]))
        prev = state_ref[g]
        off_cols.append(_dot(cg_b, prev.astype(BF16)) * ecs[:, g * gw:(g + 1) * gw])
        states = _dot(bg_t, xw_b[:, g * gw:(g + 1) * gw])
        state_ref[g] = prev * chunk_decay[:, g * gw:(g + 1) * gw] + states

    y = jnp.concatenate(diag_cols, axis=1) + jnp.concatenate(off_cols, axis=1) + xs * dsk_ref[...]

    z = z_ref[...]
    gated = y * (z * _sigmoid(z))
    ms = jnp.mean(gated * gated, axis=-1, keepdims=True)
    o_ref[...] = (gated * lax.rsqrt(ms + EPS) * nw_ref[...]).astype(o_ref.dtype)


def _ssd(xbc, z, misc, conv_w, conv_b, dt_bias, a_log, d_skip, norm_w, batch, seq):
    t, conv_dim = xbc.shape
    inner = z.shape[1]
    heads = dt_bias.shape[0]
    groups = SSD_GROUPS
    ch = SSD_CHUNK
    nc = seq // ch
    pad = lambda v: jnp.pad(v.astype(F32), (0, LANES - heads)).reshape(1, LANES)
    ltri = jnp.tril(jnp.ones((ch, ch), F32)).astype(BF16)
    head_of_lane = jnp.arange(inner) // SSD_HEAD_DIM
    expand = (jnp.arange(LANES)[:, None] == head_of_lane[None, :]).astype(BF16)
    d_exp = jnp.repeat(d_skip.astype(F32), SSD_HEAD_DIM).reshape(1, inner)
    row = lambda b, c: (b * nc + c, 0)
    const = lambda b, c: (0, 0)
    return pl.pallas_call(
        functools.partial(_ssd_body, inner=inner, groups=groups, heads=heads),
        out_shape=jax.ShapeDtypeStruct((t, inner), BF16),
        grid=(batch, nc),
        in_specs=[pl.BlockSpec((ch, conv_dim), row), pl.BlockSpec((ch, inner), row),
                  pl.BlockSpec((ch, LANES), row),
                  pl.BlockSpec((SSD_CONV, conv_dim), const), pl.BlockSpec((1, conv_dim), const),
                  pl.BlockSpec((1, LANES), const), pl.BlockSpec((1, LANES), const),
                  pl.BlockSpec((1, inner), const), pl.BlockSpec((1, inner), const),
                  pl.BlockSpec((ch, ch), const), pl.BlockSpec((LANES, inner), const)],
        out_specs=pl.BlockSpec((ch, inner), row),
        scratch_shapes=[pltpu.VMEM((SUBLANES + ch, conv_dim), F32),
                        pltpu.VMEM((groups, SSD_STATE, inner // groups), F32)],
        compiler_params=_params("arbitrary", "arbitrary"),
        name="ssd_scan",
    )(xbc, z, misc, conv_w, conv_b.reshape(1, conv_dim), pad(dt_bias), pad(a_log), d_exp,
      norm_w.reshape(1, inner), ltri, expand)


def _indexer_body(qi_ref, ki_ref, misc_ref, u_ref, ones_ref, mask_ref, keys_ref, hi_ref, lo_ref, *,
                  seq, kc, rows, topk, w_lane, idx_scale):
    qb = pl.program_id(1)
    q0 = qb * rows
    n_chunks = (q0 + rows + kc - 1) // kc
    n_unmasked = (q0 + 1) // kc
    n_lane_chunks = kc // LANES
    lane_chunk = lambda a, j: a[:, j * LANES:(j + 1) * LANES]
    int16_min = jnp.int16(-2**15)
    w = misc_ref[...] * idx_scale
    qi = qi_ref[...]
    ones = ones_ref[...]

    def score_chunk(c, carry, *, masked):
        off = pl.multiple_of(c * kc, kc)
        k_lo = ki_ref[pl.ds(off, kc), 0:LANES]
        k_hi = ki_ref[pl.ds(off, kc), LANES:2 * LANES]
        s = jnp.zeros((rows, kc), F32)
        for j in range(IDX_HEADS // 2):
            qj = lane_chunk(qi, j)
            w0 = w[:, w_lane + 2 * j:w_lane + 2 * j + 1]
            w1 = w[:, w_lane + 2 * j + 1:w_lane + 2 * j + 2]
            s = s + w0 * jnp.maximum(_dot_nt(qj, k_lo), 0.0) + w1 * jnp.maximum(_dot_nt(qj, k_hi), 0.0)
        bits = lax.bitcast_convert_type(s, I32)
        key = jnp.where(bits >= 0, bits, bits ^ jnp.int32(0x7FFFFFFF))
        if masked:
            qpos = q0 + lax.broadcasted_iota(I32, (rows, kc), 0)
            kpos = off + lax.broadcasted_iota(I32, (rows, kc), 1)
            key = jnp.where(kpos <= qpos, key, jnp.int32(INT32_MIN))
        keys_ref[:, pl.ds(off, kc)] = key
        hi_ref[:, pl.ds(off, kc)] = (key >> 16).astype(jnp.int16)
        lo_ref[:, pl.ds(off, kc)] = ((key & 0xFFFF) - 2**15).astype(jnp.int16)
        return carry

    lax.fori_loop(0, n_unmasked, functools.partial(score_chunk, masked=False), 0)
    lax.fori_loop(n_unmasked, n_chunks, functools.partial(score_chunk, masked=True), 0)

    def count16(pred):
        def body(c, acc):
            h = hi_ref[:, pl.ds(pl.multiple_of(c * kc, kc), kc)]
            for j in range(n_lane_chunks):
                acc = acc + jnp.where(pred(lane_chunk(h, j)), jnp.int16(1), jnp.int16(0))
            return acc
        acc = lax.fori_loop(0, n_chunks, body, jnp.zeros((rows, LANES), jnp.int16))
        return _dot(acc.astype(I32).astype(F32).astype(BF16), ones)

    def to_i16(u):
        return (u - 2**15).astype(jnp.int16)

    def search16(need):
        def body(i, t):
            cand = t | jnp.left_shift(jnp.int32(1), 15 - i)
            cand16 = to_i16(cand)
            cnt = count16(lambda h: h >= cand16)
            return jnp.where(cnt >= need, cand, t)
        return lax.fori_loop(0, 16, body, jnp.zeros((rows, LANES), I32))

    t_hi = search16(jnp.float32(topk))
    t_hi16 = to_i16(t_hi)
    n_gt_hi = count16(lambda h: h > t_hi16)

    def keep_band(c, carry):
        sl = pl.ds(pl.multiple_of(c * kc, kc), kc)
        h = hi_ref[:, sl]
        lo = lo_ref[:, sl]
        hi_ref[:, sl] = jnp.concatenate(
            [jnp.where(lane_chunk(h, j) == t_hi16, lane_chunk(lo, j), int16_min) for j in range(n_lane_chunks)],
            axis=1)
        return carry

    lax.fori_loop(0, n_chunks, keep_band, 0)
    t_lo = search16(topk - n_gt_hi)
    t_lo16 = to_i16(t_lo)
    n_gt = n_gt_hi + count16(lambda h: h > t_lo16)
    n_ge = n_gt_hi + count16(lambda h: h >= t_lo16)
    t_s = (t_hi - 2**15) * 2**16 + t_lo
    select_all = t_s == jnp.int32(INT32_MIN)
    n_tie = jnp.where(select_all, 0.0, topk - n_gt)
    excess_ties = jnp.max(jnp.where(select_all, 0.0, n_ge - topk)) > 0.0

    @pl.when(jnp.logical_not(excess_ties))
    def _():
        def mask_chunk(c, carry):
            sl = pl.ds(pl.multiple_of(c * kc, kc), kc)
            k = keys_ref[:, sl]
            sel = [(lane_chunk(k, j) > t_s) | ((lane_chunk(k, j) == t_s) & jnp.logical_not(select_all))
                   for j in range(n_lane_chunks)]
            mask_ref[:, sl] = jnp.where(jnp.concatenate(sel, axis=1), 1, 0).astype(jnp.int8)
            return carry
        lax.fori_loop(0, n_chunks, mask_chunk, 0)

    @pl.when(excess_ties)
    def _():
        t_col = t_s[:, 0:1]
        n_tie_col = n_tie[:, 0:1]

        def mask_chunk(c, seen):
            sl = pl.ds(pl.multiple_of(c * kc, kc), kc)
            k = keys_ref[:, sl]
            tie = k == t_col
            tie_f = jnp.where(tie, 1.0, 0.0)
            rank = seen + _dot(tie_f.astype(BF16), u_ref[...])
            sel = (k > t_col) | (tie & (rank <= n_tie_col))
            mask_ref[:, sl] = jnp.where(sel, 1, 0).astype(jnp.int8)
            return seen + jnp.sum(tie_f, axis=1, keepdims=True)
        lax.fori_loop(0, n_chunks, mask_chunk, jnp.zeros((rows, 1), F32))

    def zero_chunk(c, carry):
        mask_ref[:, pl.ds(pl.multiple_of(c * kc, kc), kc)] = jnp.zeros((rows, kc), jnp.int8)
        return carry

    lax.fori_loop(n_chunks, seq // kc, zero_chunk, 0)


def _indexer_mask(qi, ki, misc, batch, seq, w_lane):
    t = qi.shape[0]
    rows = min(2 * Q_BLOCK, seq)
    nq = seq // rows
    kc = min(512, seq)
    topk = min(IDX_TOPK, seq // 4)
    assert seq // LANES <= 256, "per-lane counts must stay exact in bf16"
    upper = jnp.triu(jnp.ones((kc, kc), F32)).astype(BF16)
    ones = jnp.ones((LANES, LANES), BF16)
    idx_scale = (IDX_DIM ** -0.5) * (IDX_HEADS ** -0.5)
    return pl.pallas_call(
        functools.partial(_indexer_body, seq=seq, kc=kc, rows=rows, topk=topk, w_lane=w_lane,
                          idx_scale=idx_scale),
        out_shape=jax.ShapeDtypeStruct((t, seq), jnp.int8),
        grid=(batch, nq),
        in_specs=[pl.BlockSpec((rows, qi.shape[1]), lambda b, q: (b * nq + q, 0)),
                  pl.BlockSpec((seq, ki.shape[1]), lambda b, q: (b, 0)),
                  pl.BlockSpec((rows, LANES), lambda b, q: (b * nq + q, 0)),
                  pl.BlockSpec((kc, kc), lambda b, q: (0, 0)),
                  pl.BlockSpec((LANES, LANES), lambda b, q: (0, 0))],
        out_specs=pl.BlockSpec((rows, seq), lambda b, q: (b * nq + q, 0)),
        scratch_shapes=[pltpu.VMEM((rows, seq), I32), pltpu.VMEM((rows, seq), jnp.int16),
                        pltpu.VMEM((rows, seq), jnp.int16)],
        compiler_params=_params("parallel", "arbitrary"),
        name="indexer_topk_mask",
    )(qi, ki, misc, upper, ones)


def _attn_body(q_ref, k_ref, v_ref, mask_ref, o_ref, acc_ref, m_ref, l_ref, *, tq, tk, heads, group, scale_log2e):
    qb = pl.program_id(1)
    kb = pl.program_id(2)
    last = ((qb + 1) * tq - 1) // tk
    hd = ATTN_HEAD_DIM

    @pl.when(kb == 0)
    def _():
        acc_ref[...] = jnp.zeros(acc_ref.shape, F32)
        m_ref[...] = jnp.full(m_ref.shape, MASKED_LOGIT, F32)
        l_ref[...] = jnp.zeros(l_ref.shape, F32)

    @pl.when(kb <= last)
    def _():
        bias = jnp.where(mask_ref[...].astype(I32) != 0, 0.0, MASKED_LOGIT)
        for h0 in range(0, heads, group):
            hs = range(h0, min(h0 + group, heads))
            col = lambda h: slice(h * hd, (h + 1) * hd)
            s = [_dot_nt(q_ref[:, col(h)], k_ref[:, col(h)]) * scale_log2e + bias for h in hs]
            m_new = [jnp.maximum(m_ref[h], jnp.max(sh, axis=1, keepdims=True)) for h, sh in zip(hs, s)]
            p = [jnp.exp2(sh - mh[:, 0:1]) for sh, mh in zip(s, m_new)]
            for h, ph, mh in zip(hs, p, m_new):
                alpha = jnp.exp2(m_ref[h] - mh)
                l_ref[h] = alpha * l_ref[h] + jnp.sum(ph, axis=1, keepdims=True)
                acc_ref[:, col(h)] = acc_ref[:, col(h)] * alpha + _dot(ph.astype(BF16), v_ref[:, col(h)])
                m_ref[h] = mh

    @pl.when(kb == pl.num_programs(2) - 1)
    def _():
        for h in range(heads):
            cols = slice(h * hd, (h + 1) * hd)
            o_ref[:, cols] = (acc_ref[:, cols] / l_ref[h]).astype(o_ref.dtype)


def _attention(qk, v, mask, batch, seq, *, tq, tk, group=4):
    t, inner = v.shape
    heads = inner // ATTN_HEAD_DIM
    tq, tk = min(tq, seq), min(tk, seq)
    nq, nk = seq // tq, seq // tk
    last = lambda q: ((q + 1) * tq - 1) // tk
    return pl.pallas_call(
        functools.partial(_attn_body, tq=tq, tk=tk, heads=heads, group=group,
                          scale_log2e=ATTN_HEAD_DIM ** -0.5 * math.log2(math.e)),
        out_shape=jax.ShapeDtypeStruct((t, inner), BF16),
        grid=(batch, nq, nk),
        in_specs=[pl.BlockSpec((tq, inner), lambda b, q, k: (b * nq + q, 0)),
                  pl.BlockSpec((tk, inner), lambda b, q, k: (b * nk + jnp.minimum(k, last(q)), 1)),
                  pl.BlockSpec((tk, inner), lambda b, q, k: (b * nk + jnp.minimum(k, last(q)), 0)),
                  pl.BlockSpec((tq, tk), lambda b, q, k: (b * nq + q, jnp.minimum(k, last(q))))],
        out_specs=pl.BlockSpec((tq, inner), lambda b, q, k: (b * nq + q, 0)),
        scratch_shapes=[pltpu.VMEM((tq, inner), F32),
                        pltpu.VMEM((heads, tq, LANES), F32),
                        pltpu.VMEM((heads, tq, LANES), F32)],
        compiler_params=_params("parallel", "parallel", "arbitrary"),
        name="masked_attention",
    )(qk, qk, v, mask)


def _pack_bf16_pairs(x):
    h = x.shape[1] // 2
    hi = lax.bitcast_convert_type(x[:, :h].astype(F32), jnp.uint32)
    lo = lax.bitcast_convert_type(x[:, h:].astype(F32), jnp.uint32)
    return lax.bitcast_convert_type(hi | (lo >> 16), I32)


def _unpack_bf16_pairs(p):
    u = lax.bitcast_convert_type(p, jnp.uint32)
    hi = lax.bitcast_convert_type(u & jnp.uint32(0xFFFF0000), F32).astype(BF16)
    lo = lax.bitcast_convert_type(u << 16, F32).astype(BF16)
    return hi, lo


def _router_body(h_ref, nw_ref, rw_ref, rb_ref, ltri_ref, tn_ref, ids_ref, gates_ref, rank_ref, cnt_ref,
                 carry_ref):
    @pl.when(pl.program_id(0) == 0)
    def _():
        carry_ref[...] = jnp.zeros(carry_ref.shape, F32)

    x = h_ref[...]
    ms = jnp.mean(x * x, axis=-1, keepdims=True)
    tn = (x * lax.rsqrt(ms + EPS) * nw_ref[...]).astype(BF16)
    tn_ref[...] = _pack_bf16_pairs(tn)
    tm = x.shape[0]
    lane = lax.broadcasted_iota(I32, (tm, LANES), 1)
    logits = _dot(tn, rw_ref[...]) + rb_ref[...]
    work = jnp.where(lane < N_EXPERTS, logits, -jnp.inf)
    vals, hits = [], []
    for _ in range(TOP_K):
        mx = jnp.max(work, axis=1, keepdims=True)
        idx = jnp.min(jnp.where(work == mx, lane, LANES), axis=1, keepdims=True)
        hit = lane == idx
        work = jnp.where(hit, -jnp.inf, work)
        vals.append(mx)
        hits.append(hit)
    es = [jnp.exp(v - vals[0]) for v in vals]
    tot = sum(es)
    onehot = sum(jnp.where(h, 1.0, 0.0) for h in hits)
    before = _dot(ltri_ref[...], onehot.astype(BF16)) + carry_ref[...]
    lane_f = lane.astype(F32)
    ids = jnp.zeros((tm, LANES), F32)
    gates = jnp.zeros((tm, LANES), F32)
    ranks = jnp.zeros((tm, LANES), F32)
    for k in range(TOP_K):
        slot = lane == k
        ids = jnp.where(slot, jnp.sum(jnp.where(hits[k], lane_f, 0.0), axis=1, keepdims=True), ids)
        gates = jnp.where(slot, es[k] / tot, gates)
        ranks = jnp.where(slot, jnp.sum(jnp.where(hits[k], before, 0.0), axis=1, keepdims=True), ranks)
    ids_ref[...] = ids.astype(I32)
    gates_ref[...] = gates
    rank_ref[...] = ranks.astype(I32)
    carry_ref[...] = carry_ref[...] + jnp.sum(onehot, axis=0, keepdims=True)
    cnt_ref[...] = carry_ref[...]


def _router(h1, norm_w, router_w, router_b, tm):
    t, d = h1.shape
    rw = jnp.pad(router_w, ((0, 0), (0, LANES - N_EXPERTS))).astype(BF16)
    rb = jnp.pad(router_b.astype(F32), (0, LANES - N_EXPERTS)).reshape(1, LANES)
    ltri = jnp.tril(jnp.ones((tm, tm), F32), -1).astype(BF16)
    row = lambda i: (i, 0)
    const = lambda i: (0, 0)
    return pl.pallas_call(
        _router_body,
        out_shape=(jax.ShapeDtypeStruct((t, d // 2), I32), jax.ShapeDtypeStruct((t, LANES), I32),
                   jax.ShapeDtypeStruct((t, LANES), F32), jax.ShapeDtypeStruct((t, LANES), I32),
                   jax.ShapeDtypeStruct((1, LANES), F32)),
        grid=(t // tm,),
        in_specs=[pl.BlockSpec((tm, d), row), pl.BlockSpec((1, d), const), pl.BlockSpec((d, LANES), const),
                  pl.BlockSpec((1, LANES), const), pl.BlockSpec((tm, tm), const)],
        out_specs=(pl.BlockSpec((tm, d // 2), row), pl.BlockSpec((tm, LANES), row), pl.BlockSpec((tm, LANES), row),
                   pl.BlockSpec((tm, LANES), row), pl.BlockSpec((1, LANES), const)),
        scratch_shapes=[pltpu.VMEM((1, LANES), F32)],
        compiler_params=_params("arbitrary"),
        name="moe_router",
    )(h1, norm_w.reshape(1, d), rw, rb, ltri)


def _row_copy(src, src_row, dst, dst_row, sem):
    return pltpu.make_async_copy(src.at[pl.ds(src_row, 1)], dst.at[pl.ds(dst_row, 1)], sem)


def _dispatch_body(pos_ref, tn_ref, init_ref, xs_ref, sem, *, tb):
    del init_ref
    base = pl.program_id(0) * tb * TOP_K

    def issue(tok, carry):
        for k in range(TOP_K):
            _row_copy(tn_ref, tok, xs_ref, pos_ref[base + tok * TOP_K + k], sem).start()
        return carry

    def drain(tok, carry):
        for _ in range(TOP_K):
            _row_copy(tn_ref, 0, xs_ref, 0, sem).wait()
        return carry

    lax.fori_loop(0, tb, issue, 0, unroll=4)
    lax.fori_loop(0, tb, drain, 0, unroll=4)


def _dispatch(pos_flat, tn, n_rows, tb):
    t, d = tn.shape
    grid_spec = pltpu.PrefetchScalarGridSpec(
        num_scalar_prefetch=1, grid=(t // tb,),
        in_specs=[pl.BlockSpec((tb, d), lambda i, pos: (i, 0)), pl.BlockSpec(memory_space=pl.ANY)],
        out_specs=pl.BlockSpec(memory_space=pl.ANY),
        scratch_shapes=[pltpu.SemaphoreType.DMA])
    return pl.pallas_call(
        functools.partial(_dispatch_body, tb=tb),
        out_shape=jax.ShapeDtypeStruct((n_rows, d), tn.dtype),
        grid_spec=grid_spec,
        input_output_aliases={2: 0},
        compiler_params=_params("arbitrary"),
        name="moe_dispatch",
    )(pos_flat, tn, jnp.zeros((n_rows, d), tn.dtype))


def _swiglu(hg, hl):
    glu = jnp.minimum(hg, SWIGLU_LIMIT)
    lin = jnp.clip(hl, -SWIGLU_LIMIT, SWIGLU_LIMIT)
    return glu * _sigmoid(SWIGLU_ALPHA * glu) * (lin + 1.0)


def _expert_changed(te_ref, i):
    return (i == 0) | (te_ref[i] != te_ref[jnp.maximum(i - 1, 0)])


def _gemm1_body(te_ref, tv_ref, x_ref, wg_ref, wl_ref, bg_ref, bl_ref, h_ref, wgb_ref, wlb_ref):
    i = pl.program_id(1)

    @pl.when(_expert_changed(te_ref, i))
    def _():
        wgb_ref[...] = wg_ref[0].astype(BF16)
        wlb_ref[...] = wl_ref[0].astype(BF16)

    @pl.when(tv_ref[i] == 1)
    def _():
        x_hi, x_lo = _unpack_bf16_pairs(x_ref[...])
        half = x_hi.shape[1]
        hg = _dot(x_hi, wgb_ref[0:half, :]) + _dot(x_lo, wgb_ref[half:, :]) + bg_ref[0]
        hl = _dot(x_hi, wlb_ref[0:half, :]) + _dot(x_lo, wlb_ref[half:, :]) + bl_ref[0]
        h_ref[...] = _swiglu(hg, hl).astype(h_ref.dtype)

    @pl.when(tv_ref[i] == 0)
    def _():
        h_ref[...] = jnp.zeros(h_ref.shape, h_ref.dtype)


def _gemm2_body(te_ref, tv_ref, h_ref, w_ref, b_ref, y_ref, wb_ref):
    i = pl.program_id(1)

    @pl.when(_expert_changed(te_ref, i))
    def _():
        wb_ref[...] = w_ref[0].astype(BF16)

    @pl.when(tv_ref[i] == 1)
    def _():
        y_ref[...] = _dot(h_ref[...], wb_ref[...]) + b_ref[0]

    @pl.when(tv_ref[i] == 0)
    def _():
        y_ref[...] = jnp.zeros(y_ref.shape, y_ref.dtype)


def _expert_gemms(xs, tile_expert, tile_valid, w1, b1, w2, b2, *, tm, tf, tn):
    n_rows = xs.shape[0]
    n_exp, d, f2 = w1.shape
    f = f2 // 2
    tf, tn = min(tf, f), min(tn, d)
    nf = f // tf
    n_tiles = n_rows // tm
    b1r = b1.reshape(n_exp, 1, f2)
    b2r = b2.reshape(n_exp, 1, d)
    grid1 = pltpu.PrefetchScalarGridSpec(
        num_scalar_prefetch=2, grid=(nf, n_tiles),
        in_specs=[pl.BlockSpec((tm, d // 2), lambda j, i, te, tv: (i, 0)),
                  pl.BlockSpec((1, d, tf), lambda j, i, te, tv: (te[i], 0, j)),
                  pl.BlockSpec((1, d, tf), lambda j, i, te, tv: (te[i], 0, nf + j)),
                  pl.BlockSpec((1, 1, tf), lambda j, i, te, tv: (te[i], 0, j)),
                  pl.BlockSpec((1, 1, tf), lambda j, i, te, tv: (te[i], 0, nf + j))],
        out_specs=pl.BlockSpec((tm, tf), lambda j, i, te, tv: (i, j)),
        scratch_shapes=[pltpu.VMEM((d, tf), BF16), pltpu.VMEM((d, tf), BF16)])
    hidden = pl.pallas_call(
        _gemm1_body, out_shape=jax.ShapeDtypeStruct((n_rows, f), BF16), grid_spec=grid1,
        compiler_params=_params("arbitrary", "arbitrary"), name="moe_gemm1",
    )(tile_expert, tile_valid, xs, w1, w1, b1r, b1r)
    grid2 = pltpu.PrefetchScalarGridSpec(
        num_scalar_prefetch=2, grid=(d // tn, n_tiles),
        in_specs=[pl.BlockSpec((tm, f), lambda j, i, te, tv: (i, 0)),
                  pl.BlockSpec((1, f, tn), lambda j, i, te, tv: (te[i], 0, j)),
                  pl.BlockSpec((1, 1, tn), lambda j, i, te, tv: (te[i], 0, j))],
        out_specs=pl.BlockSpec((tm, tn), lambda j, i, te, tv: (i, j)),
        scratch_shapes=[pltpu.VMEM((f, tn), BF16)])
    return pl.pallas_call(
        _gemm2_body, out_shape=jax.ShapeDtypeStruct((n_rows, d), F32), grid_spec=grid2,
        compiler_params=_params("arbitrary", "arbitrary"), name="moe_gemm2",
    )(tile_expert, tile_valid, hidden, w2, b2r)


def _combine_body(pos_ref, y_ref, gates_ref, h_ref, nw_ref, o_ref, buf_ref, sem, *, tb, normalize):
    base = pl.program_id(0) * tb * TOP_K

    def issue(tok, carry):
        for k in range(TOP_K):
            _row_copy(y_ref, pos_ref[base + tok * TOP_K + k], buf_ref.at[k], tok, sem).start()
        return carry

    def drain(tok, carry):
        for _ in range(TOP_K):
            _row_copy(y_ref, 0, buf_ref.at[0], 0, sem).wait()
        return carry

    lax.fori_loop(0, tb, issue, 0, unroll=4)
    lax.fori_loop(0, tb, drain, 0, unroll=4)
    gates = gates_ref[...]
    out = h_ref[...]
    for k in range(TOP_K):
        out = out + gates[:, k:k + 1] * buf_ref[k]
    if normalize:
        ms = jnp.mean(out * out, axis=-1, keepdims=True)
        out = out * lax.rsqrt(ms + EPS) * nw_ref[...]
    o_ref[...] = out


def _combine(pos_flat, y, gates, h1, final_w, tb, normalize):
    t, d = h1.shape
    grid_spec = pltpu.PrefetchScalarGridSpec(
        num_scalar_prefetch=1, grid=(t // tb,),
        in_specs=[pl.BlockSpec(memory_space=pl.ANY),
                  pl.BlockSpec((tb, LANES), lambda i, pos: (i, 0)),
                  pl.BlockSpec((tb, d), lambda i, pos: (i, 0)),
                  pl.BlockSpec((1, d), lambda i, pos: (0, 0))],
        out_specs=pl.BlockSpec((tb, d), lambda i, pos: (i, 0)),
        scratch_shapes=[pltpu.VMEM((TOP_K, tb, d), F32), pltpu.SemaphoreType.DMA])
    return pl.pallas_call(
        functools.partial(_combine_body, tb=tb, normalize=normalize),
        out_shape=jax.ShapeDtypeStruct((t, d), F32),
        grid_spec=grid_spec,
        compiler_params=_params("arbitrary"),
        name="moe_combine",
    )(pos_flat, y, gates, h1, final_w.reshape(1, d))


def _moe(h1, norm_w, router_w, router_b, w1, b1, w2, b2, final_w, normalize, *, tm_router, tm_expert,
         tf, tn, tb_dispatch, tb):
    t, d = h1.shape
    tn_tokens, ids, gates, rank, cnt = _router(h1, norm_w, router_w, router_b, tm_router)
    counts = cnt[0, :N_EXPERTS].astype(I32)
    padded = (counts + tm_expert - 1) // tm_expert * tm_expert
    seg_end = jnp.cumsum(padded)
    seg_start = seg_end - padded
    pos = (seg_start[ids[:, :TOP_K]] + rank[:, :TOP_K]).reshape(-1)
    n_rows = t * TOP_K + N_EXPERTS * tm_expert
    n_tiles = n_rows // tm_expert
    tile_ids = jnp.arange(n_tiles, dtype=I32)
    tiles_done = jnp.sum((tile_ids[:, None] >= (seg_end // tm_expert)[None, :]).astype(I32), axis=1)
    tile_expert = jnp.minimum(tiles_done, N_EXPERTS - 1)
    tile_valid = (tile_ids < seg_end[-1] // tm_expert).astype(I32)
    xs = _dispatch(pos, tn_tokens, n_rows, tb_dispatch)
    y = _expert_gemms(xs, tile_expert, tile_valid, w1, b1, w2, b2, tm=tm_expert, tf=tf, tn=tn)
    return _combine(pos, y, gates, h1, final_w, tb, normalize)


def _layer(x2, batch, seq, w_in, conv_w, conv_b, dt_bias, a_log, d_skip, ssd_norm_w, w_ssd_branch, w_attn_branch,
           w_out, norm_mix_w, norm_ffn_w, router_w, router_b, w1, b1, w2, b2, final_w, normalize):
    t, d = x2.shape
    ssd_inner = dt_bias.shape[0] * SSD_HEAD_DIM
    conv_dim = conv_w.shape[1]
    attn_inner = w_attn_branch.shape[0]
    idx_q = IDX_HEADS * IDX_DIM
    tm = min(512, seq)

    sizes = (ssd_inner, conv_dim, dt_bias.shape[0], attn_inner, attn_inner, attn_inner,
             idx_q, IDX_DIM, IDX_HEADS, d, d)
    offs = [0]
    for s in sizes:
        offs.append(offs[-1] + s)
    col = lambda i: w_in[:, offs[i]:offs[i + 1]]
    w_z, w_xbc, w_dt, w_q, w_k, w_v, w_qi, w_ki, w_wi, w_gs, w_ga = (col(i) for i in range(len(sizes)))
    zeros_k = jnp.zeros_like(w_ki)
    w_ki2 = jnp.concatenate([w_ki, zeros_k, zeros_k, w_ki], axis=1)
    n_misc = w_dt.shape[1] + w_wi.shape[1]
    w_misc = jnp.pad(jnp.concatenate([w_dt, w_wi], axis=1), ((0, 0), (0, LANES - n_misc)))
    b16 = lambda w: w.astype(BF16)

    xn = _rmsnorm(x2, norm_mix_w, BF16, tm)
    mm = functools.partial(_matmul, tm=tm, tn=1024)
    z = mm(xn, b16(w_z), F32, "proj_z")
    xbc = mm(xn, b16(w_xbc), F32, "proj_xbc")
    gate_logits = mm(xn, b16(jnp.concatenate([w_gs, w_ga], axis=1)), F32, "proj_gates")
    misc = mm(xn, b16(w_misc), F32, "proj_misc")
    v = mm(xn, b16(w_v), BF16, "proj_v")
    attn_tabs = _rope_tables(seq, ATTN_HEAD_DIM // 4, ATTN_HEAD_DIM)
    idx_tabs = _rope_tables(seq, IDX_DIM // 4, IDX_DIM)
    qk = _rope_matmul(xn, b16(jnp.concatenate([w_q, w_k], axis=1)), attn_tabs, ATTN_HEAD_DIM // 4, seq,
                      "proj_qk", tm=tm, tn=1024)
    qi = _rope_matmul(xn, b16(w_qi), idx_tabs, IDX_DIM // 4, seq, "proj_qi", tm=tm, tn=512)
    ki = _rope_matmul(xn, b16(w_ki2), idx_tabs, IDX_DIM // 4, seq, "proj_ki", tm=tm, tn=256)

    y_ssd = _ssd(xbc, z, misc, conv_w, conv_b, dt_bias, a_log, d_skip, ssd_norm_w, batch, seq)
    mask = _indexer_mask(qi, ki, misc, batch, seq, w_lane=w_dt.shape[1])
    y_attn = _attention(qk, v, mask, batch, seq, tq=256, tk=1024)

    nd = d // min(1024, d)
    g_spec = lambda off: pl.BlockSpec((tm, min(1024, d)), lambda i, j: (i, off * nd + j))
    same = pl.BlockSpec((tm, min(1024, d)), lambda i, j: (i, j))
    part = mm(y_ssd, b16(w_ssd_branch), F32, "merge_ssd", epilogue=_ep_gate, extras=(gate_logits,),
              extra_specs=(g_spec(0),))
    merged = mm(y_attn, b16(w_attn_branch), BF16, "merge_attn", epilogue=_ep_gate_add,
                extras=(gate_logits, part), extra_specs=(g_spec(1), same))
    h1 = mm(merged, b16(w_out), F32, "out_proj", epilogue=_ep_residual, extras=(x2,), extra_specs=(same,))

    return _moe(h1, norm_ffn_w, router_w, router_b, w1, b1, w2, b2, final_w, normalize,
                tm_router=min(256, t), tm_expert=256, tf=1024, tn=2048,
                tb_dispatch=min(512, t), tb=min(128, t))


def kernel(x, w_in, conv_w, conv_b, dt_bias, a_log, d_skip, ssd_norm_w, w_ssd_branch, w_attn_branch, w_out,
           norm_mix_w, norm_ffn_w, router_w, router_b, w_expert_in, b_expert_in, w_expert_out, b_expert_out,
           final_norm_w):
    batch, seq, d = x.shape
    depth = w_in.shape[0]
    per_layer = (w_in, conv_w, conv_b, dt_bias, a_log, d_skip, ssd_norm_w, w_ssd_branch, w_attn_branch, w_out,
                 norm_mix_w, norm_ffn_w, router_w, router_b, w_expert_in, b_expert_in, w_expert_out, b_expert_out)
    h = x.reshape(batch * seq, d)
    for layer in range(depth):
        h = _layer(h, batch, seq, *(p[layer] for p in per_layer), final_norm_w, layer == depth - 1)
    return h.reshape(batch, seq, d)
```

```python
import functools
import math

import jax
import jax.numpy as jnp
from jax import lax
from jax.experimental import pallas as pl
from jax.experimental.pallas import tpu as pltpu

EPS = 1e-5
SSD_HEADS = 32
SSD_HEAD_DIM = 64
SSD_GROUPS = 4
SSD_STATE = 128
SSD_CONV = 4
SSD_CHUNK = 128
ATTN_HEADS = 16
ATTN_HEAD_DIM = 128
ROPE_THETA = 500000.0
IDX_HEADS = 8
IDX_DIM = 64
IDX_TOPK = 256
Q_BLOCK = 128
N_EXPERTS = 32
TOP_K = 4
D_EXPERT = 2048
SWIGLU_ALPHA = 1.702
SWIGLU_LIMIT = 7.0

LANES = 128
SUBLANES = 8
VMEM_LIMIT_BYTES = 56 * 2**20
INT32_MIN = -2**31
MASKED_LOGIT = -1e30

F32 = jnp.float32
BF16 = jnp.bfloat16
I32 = jnp.int32


def _params(*sem):
    return pltpu.CompilerParams(dimension_semantics=sem, vmem_limit_bytes=VMEM_LIMIT_BYTES)


def _sigmoid(x):
    return 1.0 / (1.0 + jnp.exp(-x))


def _dot(a, b):
    return jnp.dot(a, b, preferred_element_type=F32)


def _dot_nt(a, b):
    return lax.dot_general(a, b, (((1,), (1,)), ((), ())), preferred_element_type=F32)


def _split3(x):
    hi = x.astype(BF16)
    r1 = x - hi.astype(F32)
    mid = r1.astype(BF16)
    lo = (r1 - mid.astype(F32)).astype(BF16)
    return hi, mid, lo


def _rmsnorm_body(x_ref, w_ref, o_ref):
    x = x_ref[...]
    ms = jnp.mean(x * x, axis=-1, keepdims=True)
    o_ref[...] = (x * lax.rsqrt(ms + EPS) * w_ref[...]).astype(o_ref.dtype)


def _rmsnorm(x, w, out_dtype, tm):
    t, d = x.shape
    return pl.pallas_call(
        _rmsnorm_body,
        out_shape=jax.ShapeDtypeStruct((t, d), out_dtype),
        grid=(t // tm,),
        in_specs=[pl.BlockSpec((tm, d), lambda i: (i, 0)), pl.BlockSpec((1, d), lambda i: (0, 0))],
        out_specs=pl.BlockSpec((tm, d), lambda i: (i, 0)),
        compiler_params=_params("parallel"),
        name="rmsnorm",
    )(x, w.reshape(1, d))


def _ep_identity(acc):
    return acc


def _ep_rope(acc, c, s1, s2, *, shift):
    outs = []
    for g in range(acc.shape[1] // LANES):
        o = acc[:, g * LANES:(g + 1) * LANES]
        outs.append(o * c + pltpu.roll(o, LANES - shift, 1) * s1 + pltpu.roll(o, shift, 1) * s2)
    return outs[0] if len(outs) == 1 else jnp.concatenate(outs, axis=1)


def _ep_gate(acc, g):
    return _sigmoid(g) * acc


def _ep_gate_add(acc, g, p):
    return p + _sigmoid(g) * acc


def _ep_residual(acc, x):
    return x + acc


def _mm_body(*refs, epilogue):
    a_ref, w_ref = refs[0], refs[1]
    o_ref = refs[-1]
    acc = _dot(a_ref[...], w_ref[...])
    o_ref[...] = epilogue(acc, *[r[...] for r in refs[2:-1]]).astype(o_ref.dtype)


MAX_COL_TILE = 2048


def _col_tile(n):
    return max(t for t in range(LANES, min(n, MAX_COL_TILE) + 1, LANES) if n % t == 0)


def _matmul(a, w, out_dtype, name, *, tm, epilogue=_ep_identity, extras=(), extra_specs=()):
    m, k = a.shape
    n = w.shape[1]
    tn = _col_tile(n)
    return pl.pallas_call(
        functools.partial(_mm_body, epilogue=epilogue),
        out_shape=jax.ShapeDtypeStruct((m, n), out_dtype),
        grid=(n // tn, m // tm),
        in_specs=[pl.BlockSpec((tm, k), lambda j, i: (i, 0)),
                  pl.BlockSpec((k, tn), lambda j, i: (0, j)), *[f(tm, tn) for f in extra_specs]],
        out_specs=pl.BlockSpec((tm, tn), lambda j, i: (i, j)),
        compiler_params=_params("parallel", "parallel"),
        name=name,
    )(a, w, *extras)


def _rope_tables(length, rot_dim, head_dim):
    half = rot_dim // 2
    inv = ROPE_THETA ** (-jnp.arange(0, rot_dim, 2, dtype=F32) / rot_dim)
    ang = jnp.arange(length, dtype=F32)[:, None] * inv[None, :]
    cos, sin = jnp.cos(ang), jnp.sin(ang)
    zeros = lambda n: jnp.zeros((length, n), F32)
    c = jnp.concatenate([cos, cos, jnp.ones((length, head_dim - rot_dim), F32)], axis=1)
    s1 = jnp.concatenate([-sin, zeros(head_dim - half)], axis=1)
    s2 = jnp.concatenate([zeros(half), sin, zeros(head_dim - rot_dim)], axis=1)
    reps = LANES // head_dim
    return tuple(jnp.tile(t, (1, reps)) for t in (c, s1, s2))


def _rope_matmul(a, w, tables, rot_dim, seq, name, *, tm):
    nblk = seq // tm
    spec = lambda tm_, tn_: pl.BlockSpec((tm_, LANES), lambda j, i: (i % nblk, 0))
    return _matmul(a, w, BF16, name, tm=tm,
                   epilogue=functools.partial(_ep_rope, shift=rot_dim // 2),
                   extras=tables, extra_specs=(spec, spec, spec))


def _ssd_body(xbc_ref, z_ref, misc_ref, cw_ref, cb_ref, dtb_ref, alog_ref, dsk_ref, nw_ref,
              ltri_ref, exp_ref, o_ref, xext_ref, state_ref, *, inner, groups, heads):
    ch = SSD_CHUNK
    n = SSD_STATE
    gw = inner // groups
    c_idx = pl.program_id(1)
    tail = SSD_CONV - 1

    @pl.when(c_idx == 0)
    def _():
        xext_ref[0:SUBLANES, :] = jnp.zeros((SUBLANES, xext_ref.shape[1]), F32)
        state_ref[...] = jnp.zeros(state_ref.shape, F32)

    @pl.when(c_idx > 0)
    def _():
        xext_ref[0:SUBLANES, :] = xext_ref[ch:ch + SUBLANES, :]

    xext_ref[SUBLANES:SUBLANES + ch, :] = xbc_ref[...]

    conv = cb_ref[...]
    for j in range(SSD_CONV):
        conv = conv + xext_ref[pl.ds(SUBLANES - tail + j, ch), :] * cw_ref[j:j + 1, :]
    xbc = conv * _sigmoid(conv)
    xs = xbc[:, :inner]
    bm = xbc[:, inner:inner + groups * n]
    cm = xbc[:, inner + groups * n:]

    lane = lax.broadcasted_iota(I32, (1, LANES), 1)
    head_lane = lane < heads
    a = jnp.where(head_lane, -jnp.exp(alog_ref[...]), 0.0)
    dtr = misc_ref[...] + dtb_ref[...]
    dt = jnp.maximum(dtr, 0.0) + jnp.log1p(jnp.exp(-jnp.abs(dtr)))
    da = dt * a

    ltri = ltri_ref[...]
    cs = sum(_dot(ltri, p) for p in _split3(da))
    expand = exp_ref[...]
    dt_e = sum(_dot(p, expand) for p in _split3(dt))
    cs_e = sum(_dot(p, expand) for p in _split3(cs))
    cs_last = cs_e[ch - 1:ch, :]
    ecs = jnp.exp(cs_e)
    dte = jnp.exp(cs_last - cs_e)
    chunk_decay = jnp.exp(cs_last)

    xdt = xs * dt_e
    xdt_b = xdt.astype(BF16)
    xw_b = (xdt * dte).astype(BF16)
    cs_t = cs.T

    row = lax.broadcasted_iota(I32, (ch, ch), 0)
    col = lax.broadcasted_iota(I32, (ch, ch), 1)
    causal = row >= col
    first_half = lax.broadcasted_iota(I32, (ch, LANES), 1) < SSD_HEAD_DIM
    heads_per_group = heads // groups

    diag_cols, off_cols = [], []
    for g in range(groups):
        bg = bm[:, g * n:(g + 1) * n]
        cg_b = cm[:, g * n:(g + 1) * n].astype(BF16)
        cb = _dot_nt(cg_b, bg.astype(BF16))
        bg_t = bg.T.astype(BF16)
        for pr in range(heads_per_group // 2):
            h0 = g * heads_per_group + 2 * pr
            xp = xdt_b[:, h0 * SSD_HEAD_DIM:(h0 + 2) * SSD_HEAD_DIM]
            res = []
            for h in (h0, h0 + 1):
                seg = cs[:, h:h + 1] - cs_t[h:h + 1, :]
                decay = jnp.exp(jnp.where(causal, seg, -jnp.inf))
                res.append(_dot((cb * decay).astype(BF16), xp))
            diag_cols.append(jnp.where(first_half, res[0], res[1]))
        prev = state_ref[g]
        off_cols.append(_dot(cg_b, prev.astype(BF16)) * ecs[:, g * gw:(g + 1) * gw])
        states = _dot(bg_t, xw_b[:, g * gw:(g + 1) * gw])
        state_ref[g] = prev * chunk_decay[:, g * gw:(g + 1) * gw] + states

    y = jnp.concatenate(diag_cols, axis=1) + jnp.concatenate(off_cols, axis=1) + xs * dsk_ref[...]

    z = z_ref[...]
    gated = y * (z * _sigmoid(z))
    ms = jnp.mean(gated * gated, axis=-1, keepdims=True)
    o_ref[...] = (gated * lax.rsqrt(ms + EPS) * nw_ref[...]).astype(o_ref.dtype)


def _ssd(xbc, z, misc, conv_w, conv_b, dt_bias, a_log, d_skip, norm_w, batch, seq):
    t, conv_dim = xbc.shape
    inner = z.shape[1]
    heads = dt_bias.shape[0]
    groups = SSD_GROUPS
    ch = SSD_CHUNK
    nc = seq // ch
    pad = lambda v: jnp.pad(v.astype(F32), (0, LANES - heads)).reshape(1, LANES)
    ltri = jnp.tril(jnp.ones((ch, ch), F32)).astype(BF16)
    head_of_lane = jnp.arange(inner) // SSD_HEAD_DIM
    expand = (jnp.arange(LANES)[:, None] == head_of_lane[None, :]).astype(BF16)
    d_exp = jnp.repeat(d_skip.astype(F32), SSD_HEAD_DIM).reshape(1, inner)
    row = lambda b, c: (b * nc + c, 0)
    const = lambda b, c: (0, 0)
    return pl.pallas_call(
        functools.partial(_ssd_body, inner=inner, groups=groups, heads=heads),
        out_shape=jax.ShapeDtypeStruct((t, inner), BF16),
        grid=(batch, nc),
        in_specs=[pl.BlockSpec((ch, conv_dim), row), pl.BlockSpec((ch, inner), row),
                  pl.BlockSpec((ch, LANES), row),
                  pl.BlockSpec((SSD_CONV, conv_dim), const), pl.BlockSpec((1, conv_dim), const),
                  pl.BlockSpec((1, LANES), const), pl.BlockSpec((1, LANES), const),
                  pl.BlockSpec((1, inner), const), pl.BlockSpec((1, inner), const),
                  pl.BlockSpec((ch, ch), const), pl.BlockSpec((LANES, inner), const)],
        out_specs=pl.BlockSpec((ch, inner), row),
        scratch_shapes=[pltpu.VMEM((SUBLANES + ch, conv_dim), F32),
                        pltpu.VMEM((groups, SSD_STATE, inner // groups), F32)],
        compiler_params=_params("arbitrary", "arbitrary"),
        name="ssd_scan",
    )(xbc, z, misc, conv_w, conv_b.reshape(1, conv_dim), pad(dt_bias), pad(a_log), d_exp,
      norm_w.reshape(1, inner), ltri, expand)


def _indexer_body(qi_ref, ki_ref, misc_ref, u_ref, ones_ref, mask_ref, keys_ref, *,
                  seq, kc, rows, topk, w_lane, idx_scale):
    qb = pl.program_id(1)
    q0 = qb * rows
    n_chunks = (q0 + rows + kc - 1) // kc
    n_unmasked = (q0 + 1) // kc
    n_lane_chunks = kc // LANES
    lane_chunk = lambda a, j: a[:, j * LANES:(j + 1) * LANES]
    w = misc_ref[...] * idx_scale
    qi = qi_ref[...]
    ones = ones_ref[...]

    def score_chunk(c, carry, *, masked):
        off = pl.multiple_of(c * kc, kc)
        k_lo = ki_ref[pl.ds(off, kc), 0:LANES]
        k_hi = ki_ref[pl.ds(off, kc), LANES:2 * LANES]
        s = jnp.zeros((rows, kc), F32)
        for j in range(IDX_HEADS // 2):
            qj = lane_chunk(qi, j)
            w0 = w[:, w_lane + 2 * j:w_lane + 2 * j + 1]
            w1 = w[:, w_lane + 2 * j + 1:w_lane + 2 * j + 2]
            s = s + w0 * jnp.maximum(_dot_nt(qj, k_lo), 0.0) + w1 * jnp.maximum(_dot_nt(qj, k_hi), 0.0)
        bits = lax.bitcast_convert_type(s, I32)
        key = jnp.where(bits >= 0, bits, bits ^ jnp.int32(0x7FFFFFFF))
        if masked:
            qpos = q0 + lax.broadcasted_iota(I32, (rows, kc), 0)
            kpos = off + lax.broadcasted_iota(I32, (rows, kc), 1)
            key = jnp.where(kpos <= qpos, key, jnp.int32(INT32_MIN))
        keys_ref[:, pl.ds(off, kc)] = key
        return carry

    lax.fori_loop(0, n_unmasked, functools.partial(score_chunk, masked=False), 0)
    lax.fori_loop(n_unmasked, n_chunks, functools.partial(score_chunk, masked=True), 0)

    def count(pred):
        def body(c, acc):
            k = keys_ref[:, pl.ds(pl.multiple_of(c * kc, kc), kc)]
            for j in range(n_lane_chunks):
                acc = acc + jnp.where(pred(lane_chunk(k, j)), 1, 0)
            return acc
        acc = lax.fori_loop(0, n_chunks, body, jnp.zeros((rows, LANES), I32))
        return _dot(acc.astype(F32).astype(BF16), ones)

    def bit_body(i, state):
        t_u, n_ge = state
        cand_u = t_u | jnp.left_shift(jnp.int32(1), 31 - i)
        cand_s = cand_u ^ jnp.int32(INT32_MIN)
        cnt = count(lambda k: k >= cand_s)
        keep = cnt >= topk
        return jnp.where(keep, cand_u, t_u), jnp.where(keep, cnt, n_ge)

    n_admissible = (q0 + lax.broadcasted_iota(I32, (rows, LANES), 0) + 1).astype(F32)
    t_u, n_ge = lax.fori_loop(0, 32, bit_body, (jnp.zeros((rows, LANES), I32), n_admissible))
    t_s = t_u ^ jnp.int32(INT32_MIN)
    n_gt = count(lambda k: k > t_s)
    select_all = t_u == 0
    n_tie = jnp.where(select_all, 0.0, topk - n_gt)
    excess_ties = jnp.max(jnp.where(select_all, 0.0, n_ge - topk)) > 0.0

    @pl.when(jnp.logical_not(excess_ties))
    def _():
        def mask_chunk(c, carry):
            sl = pl.ds(pl.multiple_of(c * kc, kc), kc)
            k = keys_ref[:, sl]
            sel = [(lane_chunk(k, j) > t_s) | ((lane_chunk(k, j) == t_s) & jnp.logical_not(select_all))
                   for j in range(n_lane_chunks)]
            mask_ref[:, sl] = jnp.where(jnp.concatenate(sel, axis=1), 1, 0).astype(jnp.int8)
            return carry
        lax.fori_loop(0, n_chunks, mask_chunk, 0)

    @pl.when(excess_ties)
    def _():
        t_col = t_s[:, 0:1]
        n_tie_col = n_tie[:, 0:1]

        def mask_chunk(c, seen):
            sl = pl.ds(pl.multiple_of(c * kc, kc), kc)
            k = keys_ref[:, sl]
            tie = k == t_col
            tie_f = jnp.where(tie, 1.0, 0.0)
            rank = seen + _dot(tie_f.astype(BF16), u_ref[...])
            sel = (k > t_col) | (tie & (rank <= n_tie_col))
            mask_ref[:, sl] = jnp.where(sel, 1, 0).astype(jnp.int8)
            return seen + jnp.sum(tie_f, axis=1, keepdims=True)
        lax.fori_loop(0, n_chunks, mask_chunk, jnp.zeros((rows, 1), F32))

    def zero_chunk(c, carry):
        mask_ref[:, pl.ds(pl.multiple_of(c * kc, kc), kc)] = jnp.zeros((rows, kc), jnp.int8)
        return carry

    lax.fori_loop(n_chunks, seq // kc, zero_chunk, 0)


def _indexer_mask(qi, ki, misc, batch, seq, w_lane):
    t = qi.shape[0]
    rows = min(Q_BLOCK, seq)
    nq = seq // rows
    kc = min(512, seq)
    topk = min(IDX_TOPK, seq // 4)
    assert seq // LANES <= 256, "per-lane counts must stay exact in bf16"
    upper = jnp.triu(jnp.ones((kc, kc), F32)).astype(BF16)
    ones = jnp.ones((LANES, LANES), BF16)
    idx_scale = (IDX_DIM ** -0.5) * (IDX_HEADS ** -0.5)
    return pl.pallas_call(
        functools.partial(_indexer_body, seq=seq, kc=kc, rows=rows, topk=topk, w_lane=w_lane,
                          idx_scale=idx_scale),
        out_shape=jax.ShapeDtypeStruct((t, seq), jnp.int8),
        grid=(batch, nq),
        in_specs=[pl.BlockSpec((rows, qi.shape[1]), lambda b, q: (b * nq + q, 0)),
                  pl.BlockSpec((seq, ki.shape[1]), lambda b, q: (b, 0)),
                  pl.BlockSpec((rows, LANES), lambda b, q: (b * nq + q, 0)),
                  pl.BlockSpec((kc, kc), lambda b, q: (0, 0)),
                  pl.BlockSpec((LANES, LANES), lambda b, q: (0, 0))],
        out_specs=pl.BlockSpec((rows, seq), lambda b, q: (b * nq + q, 0)),
        scratch_shapes=[pltpu.VMEM((rows, seq), I32)],
        compiler_params=_params("parallel", "arbitrary"),
        name="indexer_topk_mask",
    )(qi, ki, misc, upper, ones)


def _attn_body(q_ref, k_ref, v_ref, mask_ref, o_ref, acc_ref, m_ref, l_ref, *, tq, tk, heads, group, scale_log2e):
    qb = pl.program_id(1)
    kb = pl.program_id(2)
    last = ((qb + 1) * tq - 1) // tk
    hd = ATTN_HEAD_DIM

    @pl.when(kb == 0)
    def _():
        acc_ref[...] = jnp.zeros(acc_ref.shape, F32)
        m_ref[...] = jnp.full(m_ref.shape, MASKED_LOGIT, F32)
        l_ref[...] = jnp.zeros(l_ref.shape, F32)

    @pl.when(kb <= last)
    def _():
        bias = jnp.where(mask_ref[...].astype(I32) != 0, 0.0, MASKED_LOGIT)
        for h0 in range(0, heads, group):
            hs = range(h0, min(h0 + group, heads))
            col = lambda h: slice(h * hd, (h + 1) * hd)
            s = [_dot_nt(q_ref[:, col(h)], k_ref[:, col(h)]) * scale_log2e + bias for h in hs]
            m_new = [jnp.maximum(m_ref[h], jnp.max(sh, axis=1, keepdims=True)) for h, sh in zip(hs, s)]
            p = [jnp.exp2(sh - mh[:, 0:1]) for sh, mh in zip(s, m_new)]
            for h, ph, mh in zip(hs, p, m_new):
                alpha = jnp.exp2(m_ref[h] - mh)
                l_ref[h] = alpha * l_ref[h] + jnp.sum(ph, axis=1, keepdims=True)
                acc_ref[:, col(h)] = acc_ref[:, col(h)] * alpha + _dot(ph.astype(BF16), v_ref[:, col(h)])
                m_ref[h] = mh

    @pl.when(kb == pl.num_programs(2) - 1)
    def _():
        for h in range(heads):
            cols = slice(h * hd, (h + 1) * hd)
            o_ref[:, cols] = (acc_ref[:, cols] / l_ref[h]).astype(o_ref.dtype)


def _attention(qk, v, mask, batch, seq, *, tq, tk, group=4):
    t, inner = v.shape
    heads = inner // ATTN_HEAD_DIM
    tq, tk = min(tq, seq), min(tk, seq)
    nq, nk = seq // tq, seq // tk
    last = lambda q: ((q + 1) * tq - 1) // tk
    return pl.pallas_call(
        functools.partial(_attn_body, tq=tq, tk=tk, heads=heads, group=group,
                          scale_log2e=ATTN_HEAD_DIM ** -0.5 * math.log2(math.e)),
        out_shape=jax.ShapeDtypeStruct((t, inner), BF16),
        grid=(batch, nq, nk),
        in_specs=[pl.BlockSpec((tq, inner), lambda b, q, k: (b * nq + q, 0)),
                  pl.BlockSpec((tk, inner), lambda b, q, k: (b * nk + jnp.minimum(k, last(q)), 1)),
                  pl.BlockSpec((tk, inner), lambda b, q, k: (b * nk + jnp.minimum(k, last(q)), 0)),
                  pl.BlockSpec((tq, tk), lambda b, q, k: (b * nq + q, jnp.minimum(k, last(q))))],
        out_specs=pl.BlockSpec((tq, inner), lambda b, q, k: (b * nq + q, 0)),
        scratch_shapes=[pltpu.VMEM((tq, inner), F32),
                        pltpu.VMEM((heads, tq, LANES), F32),
                        pltpu.VMEM((heads, tq, LANES), F32)],
        compiler_params=_params("parallel", "parallel", "arbitrary"),
        name="masked_attention",
    )(qk, qk, v, mask)


def _pack_bf16_pairs(x):
    h = x.shape[1] // 2
    hi = lax.bitcast_convert_type(x[:, :h].astype(F32), jnp.uint32)
    lo = lax.bitcast_convert_type(x[:, h:].astype(F32), jnp.uint32)
    return lax.bitcast_convert_type(hi | (lo >> 16), I32)


def _unpack_bf16_pairs(p):
    u = lax.bitcast_convert_type(p, jnp.uint32)
    hi = lax.bitcast_convert_type(u & jnp.uint32(0xFFFF0000), F32).astype(BF16)
    lo = lax.bitcast_convert_type(u << 16, F32).astype(BF16)
    return hi, lo


def _router_body(h_ref, nw_ref, rw_ref, rb_ref, ltri_ref, tn_ref, ids_ref, gates_ref, rank_ref, cnt_ref,
                 carry_ref):
    @pl.when(pl.program_id(0) == 0)
    def _():
        carry_ref[...] = jnp.zeros(carry_ref.shape, F32)

    x = h_ref[...]
    ms = jnp.mean(x * x, axis=-1, keepdims=True)
    tn = (x * lax.rsqrt(ms + EPS) * nw_ref[...]).astype(BF16)
    tn_ref[...] = _pack_bf16_pairs(tn)
    tm = x.shape[0]
    lane = lax.broadcasted_iota(I32, (tm, LANES), 1)
    logits = _dot(tn, rw_ref[...]) + rb_ref[...]
    work = jnp.where(lane < N_EXPERTS, logits, -jnp.inf)
    vals, hits = [], []
    for _ in range(TOP_K):
        mx = jnp.max(work, axis=1, keepdims=True)
        idx = jnp.min(jnp.where(work == mx, lane, LANES), axis=1, keepdims=True)
        hit = lane == idx
        work = jnp.where(hit, -jnp.inf, work)
        vals.append(mx)
        hits.append(hit)
    es = [jnp.exp(v - vals[0]) for v in vals]
    tot = sum(es)
    onehot = sum(jnp.where(h, 1.0, 0.0) for h in hits)
    before = _dot(ltri_ref[...], onehot.astype(BF16)) + carry_ref[...]
    lane_f = lane.astype(F32)
    ids = jnp.zeros((tm, LANES), F32)
    gates = jnp.zeros((tm, LANES), F32)
    ranks = jnp.zeros((tm, LANES), F32)
    for k in range(TOP_K):
        slot = lane == k
        ids = jnp.where(slot, jnp.sum(jnp.where(hits[k], lane_f, 0.0), axis=1, keepdims=True), ids)
        gates = jnp.where(slot, es[k] / tot, gates)
        ranks = jnp.where(slot, jnp.sum(jnp.where(hits[k], before, 0.0), axis=1, keepdims=True), ranks)
    ids_ref[...] = ids.astype(I32)
    gates_ref[...] = gates
    rank_ref[...] = ranks.astype(I32)
    carry_ref[...] = carry_ref[...] + jnp.sum(onehot, axis=0, keepdims=True)
    cnt_ref[...] = carry_ref[...]


def _router(h1, norm_w, router_w, router_b, tm):
    t, d = h1.shape
    rw = jnp.pad(router_w, ((0, 0), (0, LANES - N_EXPERTS))).astype(BF16)
    rb = jnp.pad(router_b.astype(F32), (0, LANES - N_EXPERTS)).reshape(1, LANES)
    ltri = jnp.tril(jnp.ones((tm, tm), F32), -1).astype(BF16)
    row = lambda i: (i, 0)
    const = lambda i: (0, 0)
    return pl.pallas_call(
        _router_body,
        out_shape=(jax.ShapeDtypeStruct((t, d // 2), I32), jax.ShapeDtypeStruct((t, LANES), I32),
                   jax.ShapeDtypeStruct((t, LANES), F32), jax.ShapeDtypeStruct((t, LANES), I32),
                   jax.ShapeDtypeStruct((1, LANES), F32)),
        grid=(t // tm,),
        in_specs=[pl.BlockSpec((tm, d), row), pl.BlockSpec((1, d), const), pl.BlockSpec((d, LANES), const),
                  pl.BlockSpec((1, LANES), const), pl.BlockSpec((tm, tm), const)],
        out_specs=(pl.BlockSpec((tm, d // 2), row), pl.BlockSpec((tm, LANES), row), pl.BlockSpec((tm, LANES), row),
                   pl.BlockSpec((tm, LANES), row), pl.BlockSpec((1, LANES), const)),
        scratch_shapes=[pltpu.VMEM((1, LANES), F32)],
        compiler_params=_params("arbitrary"),
        name="moe_router",
    )(h1, norm_w.reshape(1, d), rw, rb, ltri)


def _row_copy(src, src_row, dst, dst_row, sem):
    return pltpu.make_async_copy(src.at[pl.ds(src_row, 1)], dst.at[pl.ds(dst_row, 1)], sem)


def _dispatch_body(pos_ref, tn_ref, init_ref, xs_ref, sem, *, tb):
    del init_ref
    base = pl.program_id(0) * tb * TOP_K

    def issue(tok, carry):
        for k in range(TOP_K):
            _row_copy(tn_ref, tok, xs_ref, pos_ref[base + tok * TOP_K + k], sem).start()
        return carry

    def drain(tok, carry):
        for _ in range(TOP_K):
            _row_copy(tn_ref, 0, xs_ref, 0, sem).wait()
        return carry

    lax.fori_loop(0, tb, issue, 0, unroll=4)
    lax.fori_loop(0, tb, drain, 0, unroll=4)


def _dispatch(pos_flat, tn, n_rows, tb):
    t, d = tn.shape
    grid_spec = pltpu.PrefetchScalarGridSpec(
        num_scalar_prefetch=1, grid=(t // tb,),
        in_specs=[pl.BlockSpec((tb, d), lambda i, pos: (i, 0)), pl.BlockSpec(memory_space=pl.ANY)],
        out_specs=pl.BlockSpec(memory_space=pl.ANY),
        scratch_shapes=[pltpu.SemaphoreType.DMA])
    return pl.pallas_call(
        functools.partial(_dispatch_body, tb=tb),
        out_shape=jax.ShapeDtypeStruct((n_rows, d), tn.dtype),
        grid_spec=grid_spec,
        input_output_aliases={2: 0},
        compiler_params=_params("arbitrary"),
        name="moe_dispatch",
    )(pos_flat, tn, jnp.zeros((n_rows, d), tn.dtype))


def _swiglu(hg, hl):
    glu = jnp.minimum(hg, SWIGLU_LIMIT)
    lin = jnp.clip(hl, -SWIGLU_LIMIT, SWIGLU_LIMIT)
    return glu * _sigmoid(SWIGLU_ALPHA * glu) * (lin + 1.0)


def _expert_changed(te_ref, i):
    return (i == 0) | (te_ref[i] != te_ref[jnp.maximum(i - 1, 0)])


def _gemm1_body(te_ref, tv_ref, x_ref, wg_ref, wl_ref, bg_ref, bl_ref, h_ref, wgb_ref, wlb_ref):
    i = pl.program_id(1)

    @pl.when(_expert_changed(te_ref, i))
    def _():
        wgb_ref[...] = wg_ref[0].astype(BF16)
        wlb_ref[...] = wl_ref[0].astype(BF16)

    @pl.when(tv_ref[i] == 1)
    def _():
        x_hi, x_lo = _unpack_bf16_pairs(x_ref[...])
        half = x_hi.shape[1]
        hg = _dot(x_hi, wgb_ref[0:half, :]) + _dot(x_lo, wgb_ref[half:, :]) + bg_ref[0]
        hl = _dot(x_hi, wlb_ref[0:half, :]) + _dot(x_lo, wlb_ref[half:, :]) + bl_ref[0]
        h_ref[...] = _swiglu(hg, hl).astype(h_ref.dtype)

    @pl.when(tv_ref[i] == 0)
    def _():
        h_ref[...] = jnp.zeros(h_ref.shape, h_ref.dtype)


def _gemm2_body(te_ref, tv_ref, h_ref, w_ref, b_ref, y_ref, wb_ref):
    i = pl.program_id(1)

    @pl.when(_expert_changed(te_ref, i))
    def _():
        wb_ref[...] = w_ref[0].astype(BF16)

    @pl.when(tv_ref[i] == 1)
    def _():
        y_ref[...] = _dot(h_ref[...], wb_ref[...]) + b_ref[0]

    @pl.when(tv_ref[i] == 0)
    def _():
        y_ref[...] = jnp.zeros(y_ref.shape, y_ref.dtype)


def _expert_gemms(xs, tile_expert, tile_valid, w1, b1, w2, b2, *, tm, tf, tn):
    n_rows = xs.shape[0]
    n_exp, d, f2 = w1.shape
    f = f2 // 2
    tf, tn = min(tf, f), min(tn, d)
    nf = f // tf
    n_tiles = n_rows // tm
    b1r = b1.reshape(n_exp, 1, f2)
    b2r = b2.reshape(n_exp, 1, d)
    grid1 = pltpu.PrefetchScalarGridSpec(
        num_scalar_prefetch=2, grid=(nf, n_tiles),
        in_specs=[pl.BlockSpec((tm, d // 2), lambda j, i, te, tv: (i, 0)),
                  pl.BlockSpec((1, d, tf), lambda j, i, te, tv: (te[i], 0, j)),
                  pl.BlockSpec((1, d, tf), lambda j, i, te, tv: (te[i], 0, nf + j)),
                  pl.BlockSpec((1, 1, tf), lambda j, i, te, tv: (te[i], 0, j)),
                  pl.BlockSpec((1, 1, tf), lambda j, i, te, tv: (te[i], 0, nf + j))],
        out_specs=pl.BlockSpec((tm, tf), lambda j, i, te, tv: (i, j)),
        scratch_shapes=[pltpu.VMEM((d, tf), BF16), pltpu.VMEM((d, tf), BF16)])
    hidden = pl.pallas_call(
        _gemm1_body, out_shape=jax.ShapeDtypeStruct((n_rows, f), BF16), grid_spec=grid1,
        compiler_params=_params("arbitrary", "arbitrary"), name="moe_gemm1",
    )(tile_expert, tile_valid, xs, w1, w1, b1r, b1r)
    grid2 = pltpu.PrefetchScalarGridSpec(
        num_scalar_prefetch=2, grid=(d // tn, n_tiles),
        in_specs=[pl.BlockSpec((tm, f), lambda j, i, te, tv: (i, 0)),
                  pl.BlockSpec((1, f, tn), lambda j, i, te, tv: (te[i], 0, j)),
                  pl.BlockSpec((1, 1, tn), lambda j, i, te, tv: (te[i], 0, j))],
        out_specs=pl.BlockSpec((tm, tn), lambda j, i, te, tv: (i, j)),
        scratch_shapes=[pltpu.VMEM((f, tn), BF16)])
    return pl.pallas_call(
        _gemm2_body, out_shape=jax.ShapeDtypeStruct((n_rows, d), F32), grid_spec=grid2,
        compiler_params=_params("arbitrary", "arbitrary"), name="moe_gemm2",
    )(tile_expert, tile_valid, hidden, w2, b2r)


def _combine_body(pos_ref, y_ref, gates_ref, h_ref, nw_ref, o_ref, buf_ref, sem, *, tb, normalize):
    base = pl.program_id(0) * tb * TOP_K

    def issue(tok, carry):
        for k in range(TOP_K):
            _row_copy(y_ref, pos_ref[base + tok * TOP_K + k], buf_ref.at[k], tok, sem).start()
        return carry

    def drain(tok, carry):
        for _ in range(TOP_K):
            _row_copy(y_ref, 0, buf_ref.at[0], 0, sem).wait()
        return carry

    lax.fori_loop(0, tb, issue, 0, unroll=4)
    lax.fori_loop(0, tb, drain, 0, unroll=4)
    gates = gates_ref[...]
    out = h_ref[...]
    for k in range(TOP_K):
        out = out + gates[:, k:k + 1] * buf_ref[k]
    if normalize:
        ms = jnp.mean(out * out, axis=-1, keepdims=True)
        out = out * lax.rsqrt(ms + EPS) * nw_ref[...]
    o_ref[...] = out


def _combine(pos_flat, y, gates, h1, final_w, tb, normalize):
    t, d = h1.shape
    grid_spec = pltpu.PrefetchScalarGridSpec(
        num_scalar_prefetch=1, grid=(t // tb,),
        in_specs=[pl.BlockSpec(memory_space=pl.ANY),
                  pl.BlockSpec((tb, LANES), lambda i, pos: (i, 0)),
                  pl.BlockSpec((tb, d), lambda i, pos: (i, 0)),
                  pl.BlockSpec((1, d), lambda i, pos: (0, 0))],
        out_specs=pl.BlockSpec((tb, d), lambda i, pos: (i, 0)),
        scratch_shapes=[pltpu.VMEM((TOP_K, tb, d), F32), pltpu.SemaphoreType.DMA])
    return pl.pallas_call(
        functools.partial(_combine_body, tb=tb, normalize=normalize),
        out_shape=jax.ShapeDtypeStruct((t, d), F32),
        grid_spec=grid_spec,
        compiler_params=_params("arbitrary"),
        name="moe_combine",
    )(pos_flat, y, gates, h1, final_w.reshape(1, d))


def _moe(h1, norm_w, router_w, router_b, w1, b1, w2, b2, final_w, normalize, *, tm_router, tm_expert,
         tf, tn, tb_dispatch, tb):
    t, d = h1.shape
    tn_tokens, ids, gates, rank, cnt = _router(h1, norm_w, router_w, router_b, tm_router)
    counts = cnt[0, :N_EXPERTS].astype(I32)
    padded = (counts + tm_expert - 1) // tm_expert * tm_expert
    seg_end = jnp.cumsum(padded)
    seg_start = seg_end - padded
    pos = (seg_start[ids[:, :TOP_K]] + rank[:, :TOP_K]).reshape(-1)
    n_rows = t * TOP_K + N_EXPERTS * tm_expert
    n_tiles = n_rows // tm_expert
    tile_ids = jnp.arange(n_tiles, dtype=I32)
    tiles_done = jnp.sum((tile_ids[:, None] >= (seg_end // tm_expert)[None, :]).astype(I32), axis=1)
    tile_expert = jnp.minimum(tiles_done, N_EXPERTS - 1)
    tile_valid = (tile_ids < seg_end[-1] // tm_expert).astype(I32)
    xs = _dispatch(pos, tn_tokens, n_rows, tb_dispatch)
    y = _expert_gemms(xs, tile_expert, tile_valid, w1, b1, w2, b2, tm=tm_expert, tf=tf, tn=tn)
    return _combine(pos, y, gates, h1, final_w, tb, normalize)


def _layer(x2, batch, seq, w_in, conv_w, conv_b, dt_bias, a_log, d_skip, ssd_norm_w, w_ssd_branch, w_attn_branch,
           w_out, norm_mix_w, norm_ffn_w, router_w, router_b, w1, b1, w2, b2, final_w, normalize):
    t, d = x2.shape
    ssd_inner = dt_bias.shape[0] * SSD_HEAD_DIM
    conv_dim = conv_w.shape[1]
    attn_inner = w_attn_branch.shape[0]
    idx_q = IDX_HEADS * IDX_DIM
    tm = min(512, seq)

    sizes = (ssd_inner, conv_dim, dt_bias.shape[0], attn_inner, attn_inner, attn_inner,
             idx_q, IDX_DIM, IDX_HEADS, d, d)
    offs = [0]
    for s in sizes:
        offs.append(offs[-1] + s)
    col = lambda i: w_in[:, offs[i]:offs[i + 1]]
    w_z, w_xbc, w_dt, w_q, w_k, w_v, w_qi, w_ki, w_wi, w_gs, w_ga = (col(i) for i in range(len(sizes)))
    zeros_k = jnp.zeros_like(w_ki)
    w_ki2 = jnp.concatenate([w_ki, zeros_k, zeros_k, w_ki], axis=1)
    n_misc = w_dt.shape[1] + w_wi.shape[1]
    w_misc = jnp.pad(jnp.concatenate([w_dt, w_wi], axis=1), ((0, 0), (0, LANES - n_misc)))
    b16 = lambda w: w.astype(BF16)

    xn = _rmsnorm(x2, norm_mix_w, BF16, tm)
    mm = functools.partial(_matmul, tm=tm)
    z = mm(xn, b16(w_z), F32, "proj_z")
    xbc = mm(xn, b16(w_xbc), F32, "proj_xbc")
    gate_logits = mm(xn, b16(jnp.concatenate([w_gs, w_ga], axis=1)), F32, "proj_gates")
    misc = mm(xn, b16(w_misc), F32, "proj_misc")
    v = mm(xn, b16(w_v), BF16, "proj_v")
    attn_tabs = _rope_tables(seq, ATTN_HEAD_DIM // 4, ATTN_HEAD_DIM)
    idx_tabs = _rope_tables(seq, IDX_DIM // 4, IDX_DIM)
    qk = _rope_matmul(xn, b16(jnp.concatenate([w_q, w_k], axis=1)), attn_tabs, ATTN_HEAD_DIM // 4, seq,
                      "proj_qk", tm=tm)
    qi = _rope_matmul(xn, b16(w_qi), idx_tabs, IDX_DIM // 4, seq, "proj_qi", tm=tm)
    ki = _rope_matmul(xn, b16(w_ki2), idx_tabs, IDX_DIM // 4, seq, "proj_ki", tm=tm)

    y_ssd = _ssd(xbc, z, misc, conv_w, conv_b, dt_bias, a_log, d_skip, ssd_norm_w, batch, seq)
    mask = _indexer_mask(qi, ki, misc, batch, seq, w_lane=w_dt.shape[1])
    y_attn = _attention(qk, v, mask, batch, seq, tq=256, tk=1024)

    g_spec = lambda off: lambda tm_, tn_: pl.BlockSpec((tm_, tn_), lambda j, i: (i, off * (d // tn_) + j))
    same = lambda tm_, tn_: pl.BlockSpec((tm_, tn_), lambda j, i: (i, j))
    part = mm(y_ssd, b16(w_ssd_branch), F32, "merge_ssd", epilogue=_ep_gate, extras=(gate_logits,),
              extra_specs=(g_spec(0),))
    merged = mm(y_attn, b16(w_attn_branch), BF16, "merge_attn", epilogue=_ep_gate_add,
                extras=(gate_logits, part), extra_specs=(g_spec(1), same))
    h1 = mm(merged, b16(w_out), F32, "out_proj", epilogue=_ep_residual, extras=(x2,), extra_specs=(same,))

    return _moe(h1, norm_ffn_w, router_w, router_b, w1, b1, w2, b2, final_w, normalize,
                tm_router=min(256, t), tm_expert=256, tf=1024, tn=2048,
                tb_dispatch=min(512, t), tb=min(128, t))


def kernel(x, w_in, conv_w, conv_b, dt_bias, a_log, d_skip, ssd_norm_w, w_ssd_branch, w_attn_branch, w_out,
           norm_mix_w, norm_ffn_w, router_w, router_b, w_expert_in, b_expert_in, w_expert_out, b_expert_out,
           final_norm_w):
    batch, seq, d = x.shape
    depth = w_in.shape[0]
    per_layer = (w_in, conv_w, conv_b, dt_bias, a_log, d_skip, ssd_norm_w, w_ssd_branch, w_attn_branch, w_out,
                 norm_mix_w, norm_ffn_w, router_w, router_b, w_expert_in, b_expert_in, w_expert_out, b_expert_out)
    h = x.reshape(batch * seq, d)
    for layer in range(depth):
        h = _layer(h, batch, seq, *(p[layer] for p in per_layer), final_norm_w, layer == depth - 1)
    return h.reshape(batch, seq, d)
```

```python
import functools
import math

import jax
import jax.numpy as jnp
from jax import lax
from jax.experimental import pallas as pl
from jax.experimental.pallas import tpu as pltpu

EPS = 1e-5
SSD_HEADS = 32
SSD_HEAD_DIM = 64
SSD_GROUPS = 4
SSD_STATE = 128
SSD_CONV = 4
SSD_CHUNK = 128
ATTN_HEADS = 16
ATTN_HEAD_DIM = 128
ROPE_THETA = 500000.0
IDX_HEADS = 8
IDX_DIM = 64
IDX_TOPK = 256
Q_BLOCK = 128
N_EXPERTS = 32
TOP_K = 4
D_EXPERT = 2048
SWIGLU_ALPHA = 1.702
SWIGLU_LIMIT = 7.0

LANES = 128
SUBLANES = 8
VMEM_LIMIT_BYTES = 56 * 2**20
INT32_MIN = -2**31
MASKED_LOGIT = -1e30

ROW_TILE = 512
ATTN_Q_TILE = 256
ATTN_K_TILE = 1024
ATTN_HEAD_GROUP = 4
INDEXER_KEY_CHUNK = 512
ROUTER_ROW_TILE = 256
EXPERT_ROW_TILE = 256
EXPERT_HIDDEN_TILE = 1024
EXPERT_OUT_TILE = 2048
DISPATCH_TOKENS = 512
COMBINE_TOKENS = 128

F32 = jnp.float32
BF16 = jnp.bfloat16
I32 = jnp.int32


def _params(*sem):
    return pltpu.CompilerParams(dimension_semantics=sem, vmem_limit_bytes=VMEM_LIMIT_BYTES)


def _sigmoid(x):
    return 1.0 / (1.0 + jnp.exp(-x))


def _dot(a, b):
    return jnp.dot(a, b, preferred_element_type=F32)


def _dot_nt(a, b):
    return lax.dot_general(a, b, (((1,), (1,)), ((), ())), preferred_element_type=F32)


def _split3(x):
    hi = x.astype(BF16)
    r1 = x - hi.astype(F32)
    mid = r1.astype(BF16)
    lo = (r1 - mid.astype(F32)).astype(BF16)
    return hi, mid, lo


def _rmsnorm_body(x_ref, w_ref, o_ref):
    x = x_ref[...]
    ms = jnp.mean(x * x, axis=-1, keepdims=True)
    o_ref[...] = (x * lax.rsqrt(ms + EPS) * w_ref[...]).astype(o_ref.dtype)


def _rmsnorm(x, w, out_dtype, tm):
    t, d = x.shape
    return pl.pallas_call(
        _rmsnorm_body,
        out_shape=jax.ShapeDtypeStruct((t, d), out_dtype),
        grid=(t // tm,),
        in_specs=[pl.BlockSpec((tm, d), lambda i: (i, 0)), pl.BlockSpec((1, d), lambda i: (0, 0))],
        out_specs=pl.BlockSpec((tm, d), lambda i: (i, 0)),
        compiler_params=_params("parallel"),
        name="rmsnorm",
    )(x, w.reshape(1, d))


def _ep_identity(acc):
    return acc


def _ep_rope(acc, c, s1, s2, *, shift):
    outs = []
    for g in range(acc.shape[1] // LANES):
        o = acc[:, g * LANES:(g + 1) * LANES]
        outs.append(o * c + pltpu.roll(o, LANES - shift, 1) * s1 + pltpu.roll(o, shift, 1) * s2)
    return outs[0] if len(outs) == 1 else jnp.concatenate(outs, axis=1)


def _ep_gate(acc, g):
    return _sigmoid(g) * acc


def _ep_gate_add(acc, g, p):
    return p + _sigmoid(g) * acc


def _ep_residual(acc, x):
    return x + acc


def _mm_body(*refs, epilogue):
    a_ref, w_ref = refs[0], refs[1]
    o_ref = refs[-1]
    acc = _dot(a_ref[...], w_ref[...])
    o_ref[...] = epilogue(acc, *[r[...] for r in refs[2:-1]]).astype(o_ref.dtype)


MAX_COL_TILE = 2048


def _col_tile(n):
    return max(t for t in range(LANES, min(n, MAX_COL_TILE) + 1, LANES) if n % t == 0)


def _matmul(a, w, out_dtype, name, *, tm, epilogue=_ep_identity, extras=(), extra_specs=()):
    m, k = a.shape
    n = w.shape[1]
    tn = _col_tile(n)
    return pl.pallas_call(
        functools.partial(_mm_body, epilogue=epilogue),
        out_shape=jax.ShapeDtypeStruct((m, n), out_dtype),
        grid=(n // tn, m // tm),
        in_specs=[pl.BlockSpec((tm, k), lambda j, i: (i, 0)),
                  pl.BlockSpec((k, tn), lambda j, i: (0, j)), *[f(tm, tn) for f in extra_specs]],
        out_specs=pl.BlockSpec((tm, tn), lambda j, i: (i, j)),
        compiler_params=_params("parallel", "parallel"),
        name=name,
    )(a, w, *extras)


def _rope_tables(length, rot_dim, head_dim):
    half = rot_dim // 2
    inv = ROPE_THETA ** (-jnp.arange(0, rot_dim, 2, dtype=F32) / rot_dim)
    ang = jnp.arange(length, dtype=F32)[:, None] * inv[None, :]
    cos, sin = jnp.cos(ang), jnp.sin(ang)
    zeros = lambda n: jnp.zeros((length, n), F32)
    c = jnp.concatenate([cos, cos, jnp.ones((length, head_dim - rot_dim), F32)], axis=1)
    s1 = jnp.concatenate([-sin, zeros(head_dim - half)], axis=1)
    s2 = jnp.concatenate([zeros(half), sin, zeros(head_dim - rot_dim)], axis=1)
    reps = LANES // head_dim
    return tuple(jnp.tile(t, (1, reps)) for t in (c, s1, s2))


def _rope_matmul(a, w, tables, rot_dim, seq, name, *, tm):
    nblk = seq // tm
    spec = lambda tm_, tn_: pl.BlockSpec((tm_, LANES), lambda j, i: (i % nblk, 0))
    return _matmul(a, w, BF16, name, tm=tm,
                   epilogue=functools.partial(_ep_rope, shift=rot_dim // 2),
                   extras=tables, extra_specs=(spec, spec, spec))


def _ssd_body(xbc_ref, z_ref, misc_ref, cw_ref, cb_ref, dtb_ref, alog_ref, dsk_ref, nw_ref,
              ltri_ref, exp_ref, o_ref, tail_ref, state_ref, *, inner, groups, heads):
    ch = SSD_CHUNK
    n = SSD_STATE
    gw = inner // groups
    c_idx = pl.program_id(1)
    tail = SSD_CONV - 1

    @pl.when(c_idx == 0)
    def _():
        tail_ref[...] = jnp.zeros(tail_ref.shape, F32)
        state_ref[...] = jnp.zeros(state_ref.shape, F32)

    cur = xbc_ref[...]
    prev_rows = tail_ref[...]
    row8 = lax.broadcasted_iota(I32, (SUBLANES, cur.shape[1]), 0)
    conv = cb_ref[...] + cur * cw_ref[tail:tail + 1, :]
    for j in range(tail):
        shift = tail - j
        rolled = pltpu.roll(cur, shift, 0)
        head = jnp.where(row8 < shift, pltpu.roll(prev_rows, shift, 0), rolled[0:SUBLANES, :])
        conv = conv + jnp.concatenate([head, rolled[SUBLANES:, :]], axis=0) * cw_ref[j:j + 1, :]
    tail_ref[...] = cur[ch - SUBLANES:, :]
    xbc = conv * _sigmoid(conv)
    xs = xbc[:, :inner]
    bm = xbc[:, inner:inner + groups * n]
    cm = xbc[:, inner + groups * n:]

    lane = lax.broadcasted_iota(I32, (1, LANES), 1)
    head_lane = lane < heads
    a = jnp.where(head_lane, -jnp.exp(alog_ref[...]), 0.0)
    dtr = misc_ref[...] + dtb_ref[...]
    dt = jnp.maximum(dtr, 0.0) + jnp.log1p(jnp.exp(-jnp.abs(dtr)))
    da = dt * a

    ltri = ltri_ref[...]
    cs = sum(_dot(ltri, p) for p in _split3(da))
    expand = exp_ref[...]
    dt_e = sum(_dot(p, expand) for p in _split3(dt))
    cs_e = sum(_dot(p, expand) for p in _split3(cs))
    cs_last = cs_e[ch - 1:ch, :]
    ecs = jnp.exp(cs_e)
    dte = jnp.exp(cs_last - cs_e)
    chunk_decay = jnp.exp(cs_last)

    xdt = xs * dt_e
    xdt_b = xdt.astype(BF16)
    xw_b = (xdt * dte).astype(BF16)
    cs_t = cs.T

    row = lax.broadcasted_iota(I32, (ch, ch), 0)
    col = lax.broadcasted_iota(I32, (ch, ch), 1)
    causal = row >= col
    first_half = lax.broadcasted_iota(I32, (ch, LANES), 1) < SSD_HEAD_DIM
    heads_per_group = heads // groups

    diag_cols, off_cols = [], []
    for g in range(groups):
        bg = bm[:, g * n:(g + 1) * n]
        cg_b = cm[:, g * n:(g + 1) * n].astype(BF16)
        cb = _dot_nt(cg_b, bg.astype(BF16))
        bg_t = bg.T.astype(BF16)
        for pr in range(heads_per_group // 2):
            h0 = g * heads_per_group + 2 * pr
            xp = xdt_b[:, h0 * SSD_HEAD_DIM:(h0 + 2) * SSD_HEAD_DIM]
            res = []
            for h in (h0, h0 + 1):
                seg = cs[:, h:h + 1] - cs_t[h:h + 1, :]
                decay = jnp.exp(jnp.where(causal, seg, -jnp.inf))
                res.append(_dot((cb * decay).astype(BF16), xp))
            diag_cols.append(jnp.where(first_half, res[0], res[1]))
        prev = state_ref[g]
        off_cols.append(_dot(cg_b, prev.astype(BF16)) * ecs[:, g * gw:(g + 1) * gw])
        states = _dot(bg_t, xw_b[:, g * gw:(g + 1) * gw])
        state_ref[g] = prev * chunk_decay[:, g * gw:(g + 1) * gw] + states

    y = jnp.concatenate(diag_cols, axis=1) + jnp.concatenate(off_cols, axis=1) + xs * dsk_ref[...]

    z = z_ref[...]
    gated = y * (z * _sigmoid(z))
    ms = jnp.mean(gated * gated, axis=-1, keepdims=True)
    o_ref[...] = (gated * lax.rsqrt(ms + EPS) * nw_ref[...]).astype(o_ref.dtype)


def _ssd(xbc, z, misc, conv_w, conv_b, dt_bias, a_log, d_skip, norm_w, batch, seq):
    t, conv_dim = xbc.shape
    inner = z.shape[1]
    heads = dt_bias.shape[0]
    groups = SSD_GROUPS
    ch = SSD_CHUNK
    nc = seq // ch
    pad = lambda v: jnp.pad(v.astype(F32), (0, LANES - heads)).reshape(1, LANES)
    ltri = jnp.tril(jnp.ones((ch, ch), F32)).astype(BF16)
    head_of_lane = jnp.arange(inner) // SSD_HEAD_DIM
    expand = (jnp.arange(LANES)[:, None] == head_of_lane[None, :]).astype(BF16)
    d_exp = jnp.repeat(d_skip.astype(F32), SSD_HEAD_DIM).reshape(1, inner)
    row = lambda b, c: (b * nc + c, 0)
    const = lambda b, c: (0, 0)
    return pl.pallas_call(
        functools.partial(_ssd_body, inner=inner, groups=groups, heads=heads),
        out_shape=jax.ShapeDtypeStruct((t, inner), BF16),
        grid=(batch, nc),
        in_specs=[pl.BlockSpec((ch, conv_dim), row), pl.BlockSpec((ch, inner), row),
                  pl.BlockSpec((ch, LANES), row),
                  pl.BlockSpec((SSD_CONV, conv_dim), const), pl.BlockSpec((1, conv_dim), const),
                  pl.BlockSpec((1, LANES), const), pl.BlockSpec((1, LANES), const),
                  pl.BlockSpec((1, inner), const), pl.BlockSpec((1, inner), const),
                  pl.BlockSpec((ch, ch), const), pl.BlockSpec((LANES, inner), const)],
        out_specs=pl.BlockSpec((ch, inner), row),
        scratch_shapes=[pltpu.VMEM((SUBLANES, conv_dim), F32),
                        pltpu.VMEM((groups, SSD_STATE, inner // groups), F32)],
        compiler_params=_params("arbitrary", "arbitrary"),
        name="ssd_scan",
    )(xbc, z, misc, conv_w, conv_b.reshape(1, conv_dim), pad(dt_bias), pad(a_log), d_exp,
      norm_w.reshape(1, inner), ltri, expand)


def _indexer_body(qi_ref, ki_ref, misc_ref, u_ref, ones_ref, mask_ref, keys_ref, *,
                  seq, kc, rows, topk, w_lane, idx_scale):
    qb = pl.program_id(1)
    q0 = qb * rows
    n_chunks = (q0 + rows + kc - 1) // kc
    n_unmasked = (q0 + 1) // kc
    n_lane_chunks = kc // LANES
    lane_chunk = lambda a, j: a[:, j * LANES:(j + 1) * LANES]
    w = misc_ref[...] * idx_scale
    qi = qi_ref[...]
    ones = ones_ref[...]

    def score_chunk(c, carry, *, masked):
        off = pl.multiple_of(c * kc, kc)
        k_lo = ki_ref[pl.ds(off, kc), 0:LANES]
        k_hi = ki_ref[pl.ds(off, kc), LANES:2 * LANES]
        s = jnp.zeros((rows, kc), F32)
        for j in range(IDX_HEADS // 2):
            qj = lane_chunk(qi, j)
            w0 = w[:, w_lane + 2 * j:w_lane + 2 * j + 1]
            w1 = w[:, w_lane + 2 * j + 1:w_lane + 2 * j + 2]
            s = s + w0 * jnp.maximum(_dot_nt(qj, k_lo), 0.0) + w1 * jnp.maximum(_dot_nt(qj, k_hi), 0.0)
        bits = lax.bitcast_convert_type(s, I32)
        key = jnp.where(bits >= 0, bits, bits ^ jnp.int32(0x7FFFFFFF))
        if masked:
            qpos = q0 + lax.broadcasted_iota(I32, (rows, kc), 0)
            kpos = off + lax.broadcasted_iota(I32, (rows, kc), 1)
            key = jnp.where(kpos <= qpos, key, jnp.int32(INT32_MIN))
        keys_ref[:, pl.ds(off, kc)] = key
        return carry

    lax.fori_loop(0, n_unmasked, functools.partial(score_chunk, masked=False), 0)
    lax.fori_loop(n_unmasked, n_chunks, functools.partial(score_chunk, masked=True), 0)

    def count(pred):
        def body(c, acc):
            k = keys_ref[:, pl.ds(pl.multiple_of(c * kc, kc), kc)]
            for j in range(n_lane_chunks):
                acc = acc + jnp.where(pred(lane_chunk(k, j)), 1, 0)
            return acc
        acc = lax.fori_loop(0, n_chunks, body, jnp.zeros((rows, LANES), I32))
        return _dot(acc.astype(F32).astype(BF16), ones)

    def bit_body(i, state):
        t_u, n_ge = state
        cand_u = t_u | jnp.left_shift(jnp.int32(1), 31 - i)
        cand_s = cand_u ^ jnp.int32(INT32_MIN)
        cnt = count(lambda k: k >= cand_s)
        keep = cnt >= topk
        return jnp.where(keep, cand_u, t_u), jnp.where(keep, cnt, n_ge)

    n_admissible = (q0 + lax.broadcasted_iota(I32, (rows, LANES), 0) + 1).astype(F32)
    t_u, n_ge = lax.fori_loop(0, 32, bit_body, (jnp.zeros((rows, LANES), I32), n_admissible))
    t_s = t_u ^ jnp.int32(INT32_MIN)
    n_gt = count(lambda k: k > t_s)
    select_all = t_u == 0
    n_tie = jnp.where(select_all, 0.0, topk - n_gt)
    excess_ties = jnp.max(jnp.where(select_all, 0.0, n_ge - topk)) > 0.0

    @pl.when(jnp.logical_not(excess_ties))
    def _():
        def mask_chunk(c, carry):
            sl = pl.ds(pl.multiple_of(c * kc, kc), kc)
            k = keys_ref[:, sl]
            sel = [(lane_chunk(k, j) > t_s) | ((lane_chunk(k, j) == t_s) & jnp.logical_not(select_all))
                   for j in range(n_lane_chunks)]
            mask_ref[:, sl] = jnp.where(jnp.concatenate(sel, axis=1), 1, 0).astype(jnp.int8)
            return carry
        lax.fori_loop(0, n_chunks, mask_chunk, 0)

    @pl.when(excess_ties)
    def _():
        t_col = t_s[:, 0:1]
        n_tie_col = n_tie[:, 0:1]

        def mask_chunk(c, seen):
            sl = pl.ds(pl.multiple_of(c * kc, kc), kc)
            k = keys_ref[:, sl]
            tie = k == t_col
            tie_f = jnp.where(tie, 1.0, 0.0)
            rank = seen + _dot(tie_f.astype(BF16), u_ref[...])
            sel = (k > t_col) | (tie & (rank <= n_tie_col))
            mask_ref[:, sl] = jnp.where(sel, 1, 0).astype(jnp.int8)
            return seen + jnp.sum(tie_f, axis=1, keepdims=True)
        lax.fori_loop(0, n_chunks, mask_chunk, jnp.zeros((rows, 1), F32))

    def zero_chunk(c, carry):
        mask_ref[:, pl.ds(pl.multiple_of(c * kc, kc), kc)] = jnp.zeros((rows, kc), jnp.int8)
        return carry

    lax.fori_loop(n_chunks, seq // kc, zero_chunk, 0)


def _indexer_mask(qi, ki, misc, batch, seq, w_lane):
    t = qi.shape[0]
    rows = min(Q_BLOCK, seq)
    nq = seq // rows
    kc = min(INDEXER_KEY_CHUNK, seq)
    topk = min(IDX_TOPK, seq // 4)
    assert seq // LANES <= 256, "per-lane counts must stay exact in bf16"
    upper = jnp.triu(jnp.ones((kc, kc), F32)).astype(BF16)
    ones = jnp.ones((LANES, LANES), BF16)
    idx_scale = (IDX_DIM ** -0.5) * (IDX_HEADS ** -0.5)
    return pl.pallas_call(
        functools.partial(_indexer_body, seq=seq, kc=kc, rows=rows, topk=topk, w_lane=w_lane,
                          idx_scale=idx_scale),
        out_shape=jax.ShapeDtypeStruct((t, seq), jnp.int8),
        grid=(batch, nq),
        in_specs=[pl.BlockSpec((rows, qi.shape[1]), lambda b, q: (b * nq + q, 0)),
                  pl.BlockSpec((seq, ki.shape[1]), lambda b, q: (b, 0)),
                  pl.BlockSpec((rows, LANES), lambda b, q: (b * nq + q, 0)),
                  pl.BlockSpec((kc, kc), lambda b, q: (0, 0)),
                  pl.BlockSpec((LANES, LANES), lambda b, q: (0, 0))],
        out_specs=pl.BlockSpec((rows, seq), lambda b, q: (b * nq + q, 0)),
        scratch_shapes=[pltpu.VMEM((rows, seq), I32)],
        compiler_params=_params("parallel", "arbitrary"),
        name="indexer_topk_mask",
    )(qi, ki, misc, upper, ones)


def _attn_body(q_ref, k_ref, v_ref, mask_ref, o_ref, acc_ref, m_ref, l_ref, *, tq, tk, heads, group, scale_log2e):
    qb = pl.program_id(1)
    kb = pl.program_id(2)
    last = ((qb + 1) * tq - 1) // tk
    hd = ATTN_HEAD_DIM

    @pl.when(kb == 0)
    def _():
        acc_ref[...] = jnp.zeros(acc_ref.shape, F32)
        m_ref[...] = jnp.full(m_ref.shape, MASKED_LOGIT, F32)
        l_ref[...] = jnp.zeros(l_ref.shape, F32)

    @pl.when(kb <= last)
    def _():
        bias = jnp.where(mask_ref[...].astype(I32) != 0, 0.0, MASKED_LOGIT)
        for h0 in range(0, heads, group):
            hs = range(h0, min(h0 + group, heads))
            col = lambda h: slice(h * hd, (h + 1) * hd)
            s = [_dot_nt(q_ref[:, col(h)], k_ref[:, col(h)]) * scale_log2e + bias for h in hs]
            m_new = [jnp.maximum(m_ref[h], jnp.max(sh, axis=1, keepdims=True)) for h, sh in zip(hs, s)]
            p = [jnp.exp2(sh - mh[:, 0:1]) for sh, mh in zip(s, m_new)]
            for h, ph, mh in zip(hs, p, m_new):
                alpha = jnp.exp2(m_ref[h] - mh)
                l_ref[h] = alpha * l_ref[h] + jnp.sum(ph, axis=1, keepdims=True)
                acc_ref[:, col(h)] = acc_ref[:, col(h)] * alpha + _dot(ph.astype(BF16), v_ref[:, col(h)])
                m_ref[h] = mh

    @pl.when(kb == pl.num_programs(2) - 1)
    def _():
        for h in range(heads):
            cols = slice(h * hd, (h + 1) * hd)
            o_ref[:, cols] = (acc_ref[:, cols] / l_ref[h]).astype(o_ref.dtype)


def _attention(qk, v, mask, batch, seq, *, tq, tk, group=ATTN_HEAD_GROUP):
    t, inner = v.shape
    heads = inner // ATTN_HEAD_DIM
    tq, tk = min(tq, seq), min(tk, seq)
    nq, nk = seq // tq, seq // tk
    last = lambda q: ((q + 1) * tq - 1) // tk
    return pl.pallas_call(
        functools.partial(_attn_body, tq=tq, tk=tk, heads=heads, group=group,
                          scale_log2e=ATTN_HEAD_DIM ** -0.5 * math.log2(math.e)),
        out_shape=jax.ShapeDtypeStruct((t, inner), BF16),
        grid=(batch, nq, nk),
        in_specs=[pl.BlockSpec((tq, inner), lambda b, q, k: (b * nq + q, 0)),
                  pl.BlockSpec((tk, inner), lambda b, q, k: (b * nk + jnp.minimum(k, last(q)), 1)),
                  pl.BlockSpec((tk, inner), lambda b, q, k: (b * nk + jnp.minimum(k, last(q)), 0)),
                  pl.BlockSpec((tq, tk), lambda b, q, k: (b * nq + q, jnp.minimum(k, last(q))))],
        out_specs=pl.BlockSpec((tq, inner), lambda b, q, k: (b * nq + q, 0)),
        scratch_shapes=[pltpu.VMEM((tq, inner), F32),
                        pltpu.VMEM((heads, tq, LANES), F32),
                        pltpu.VMEM((heads, tq, LANES), F32)],
        compiler_params=_params("parallel", "parallel", "arbitrary"),
        name="masked_attention",
    )(qk, qk, v, mask)


def _pack_bf16_pairs(x):
    h = x.shape[1] // 2
    hi = lax.bitcast_convert_type(x[:, :h].astype(F32), jnp.uint32)
    lo = lax.bitcast_convert_type(x[:, h:].astype(F32), jnp.uint32)
    return lax.bitcast_convert_type(hi | (lo >> 16), I32)


def _unpack_bf16_pairs(p):
    u = lax.bitcast_convert_type(p, jnp.uint32)
    hi = lax.bitcast_convert_type(u & jnp.uint32(0xFFFF0000), F32).astype(BF16)
    lo = lax.bitcast_convert_type(u << 16, F32).astype(BF16)
    return hi, lo


def _router_body(h_ref, nw_ref, rw_ref, rb_ref, ltri_ref, tn_ref, ids_ref, gates_ref, rank_ref, cnt_ref,
                 carry_ref):
    @pl.when(pl.program_id(0) == 0)
    def _():
        carry_ref[...] = jnp.zeros(carry_ref.shape, F32)

    x = h_ref[...]
    ms = jnp.mean(x * x, axis=-1, keepdims=True)
    tn = (x * lax.rsqrt(ms + EPS) * nw_ref[...]).astype(BF16)
    tn_ref[...] = _pack_bf16_pairs(tn)
    tm = x.shape[0]
    lane = lax.broadcasted_iota(I32, (tm, LANES), 1)
    logits = _dot(tn, rw_ref[...]) + rb_ref[...]
    work = jnp.where(lane < N_EXPERTS, logits, -jnp.inf)
    vals, hits = [], []
    for _ in range(TOP_K):
        mx = jnp.max(work, axis=1, keepdims=True)
        idx = jnp.min(jnp.where(work == mx, lane, LANES), axis=1, keepdims=True)
        hit = lane == idx
        work = jnp.where(hit, -jnp.inf, work)
        vals.append(mx)
        hits.append(hit)
    es = [jnp.exp(v - vals[0]) for v in vals]
    tot = sum(es)
    onehot = sum(jnp.where(h, 1.0, 0.0) for h in hits)
    before = _dot(ltri_ref[...], onehot.astype(BF16)) + carry_ref[...]
    lane_f = lane.astype(F32)
    ids = jnp.zeros((tm, LANES), F32)
    gates = jnp.zeros((tm, LANES), F32)
    ranks = jnp.zeros((tm, LANES), F32)
    for k in range(TOP_K):
        slot = lane == k
        ids = jnp.where(slot, jnp.sum(jnp.where(hits[k], lane_f, 0.0), axis=1, keepdims=True), ids)
        gates = jnp.where(slot, es[k] / tot, gates)
        ranks = jnp.where(slot, jnp.sum(jnp.where(hits[k], before, 0.0), axis=1, keepdims=True), ranks)
    ids_ref[...] = ids.astype(I32)
    gates_ref[...] = gates
    rank_ref[...] = ranks.astype(I32)
    carry_ref[...] = carry_ref[...] + jnp.sum(onehot, axis=0, keepdims=True)
    cnt_ref[...] = carry_ref[...]


def _router(h1, norm_w, router_w, router_b, tm):
    t, d = h1.shape
    rw = jnp.pad(router_w, ((0, 0), (0, LANES - N_EXPERTS))).astype(BF16)
    rb = jnp.pad(router_b.astype(F32), (0, LANES - N_EXPERTS)).reshape(1, LANES)
    ltri = jnp.tril(jnp.ones((tm, tm), F32), -1).astype(BF16)
    row = lambda i: (i, 0)
    const = lambda i: (0, 0)
    return pl.pallas_call(
        _router_body,
        out_shape=(jax.ShapeDtypeStruct((t, d // 2), I32), jax.ShapeDtypeStruct((t, LANES), I32),
                   jax.ShapeDtypeStruct((t, LANES), F32), jax.ShapeDtypeStruct((t, LANES), I32),
                   jax.ShapeDtypeStruct((1, LANES), F32)),
        grid=(t // tm,),
        in_specs=[pl.BlockSpec((tm, d), row), pl.BlockSpec((1, d), const), pl.BlockSpec((d, LANES), const),
                  pl.BlockSpec((1, LANES), const), pl.BlockSpec((tm, tm), const)],
        out_specs=(pl.BlockSpec((tm, d // 2), row), pl.BlockSpec((tm, LANES), row), pl.BlockSpec((tm, LANES), row),
                   pl.BlockSpec((tm, LANES), row), pl.BlockSpec((1, LANES), const)),
        scratch_shapes=[pltpu.VMEM((1, LANES), F32)],
        compiler_params=_params("arbitrary"),
        name="moe_router",
    )(h1, norm_w.reshape(1, d), rw, rb, ltri)


def _row_copy(src, src_row, dst, dst_row, sem):
    return pltpu.make_async_copy(src.at[pl.ds(src_row, 1)], dst.at[pl.ds(dst_row, 1)], sem)


def _dispatch_body(pos_ref, pad_ref, tn_ref, xs_ref, zeros_ref, sem, zero_sem, *, tb, pad_rows, n_fills):
    base = pl.program_id(0) * tb * TOP_K

    @pl.when(pl.program_id(0) == 0)
    def _():
        zeros_ref[...] = jnp.zeros(zeros_ref.shape, zeros_ref.dtype)
        fill = lambda e: pltpu.make_async_copy(
            zeros_ref, xs_ref.at[pl.ds(pl.multiple_of(pad_ref[e], pad_rows), pad_rows)], zero_sem)
        for e in range(n_fills):
            pl.when(pad_ref[e] >= 0)(lambda e=e: fill(e).start())
        for e in range(n_fills):
            pl.when(pad_ref[e] >= 0)(lambda e=e: fill(e).wait())

    def issue(tok, carry):
        for k in range(TOP_K):
            _row_copy(tn_ref, tok, xs_ref, pos_ref[base + tok * TOP_K + k], sem).start()
        return carry

    def drain(tok, carry):
        for _ in range(TOP_K):
            _row_copy(tn_ref, 0, xs_ref, 0, sem).wait()
        return carry

    lax.fori_loop(0, tb, issue, 0, unroll=4)
    lax.fori_loop(0, tb, drain, 0, unroll=4)


def _dispatch(pos_flat, pad_start, tn, n_rows, tb, pad_rows):
    t, d = tn.shape
    grid_spec = pltpu.PrefetchScalarGridSpec(
        num_scalar_prefetch=2, grid=(t // tb,),
        in_specs=[pl.BlockSpec((tb, d), lambda i, pos, pad: (i, 0))],
        out_specs=pl.BlockSpec(memory_space=pl.ANY),
        scratch_shapes=[pltpu.VMEM((pad_rows, d), tn.dtype), pltpu.SemaphoreType.DMA, pltpu.SemaphoreType.DMA])
    return pl.pallas_call(
        functools.partial(_dispatch_body, tb=tb, pad_rows=pad_rows, n_fills=pad_start.shape[0]),
        out_shape=jax.ShapeDtypeStruct((n_rows, d), tn.dtype),
        grid_spec=grid_spec,
        compiler_params=_params("arbitrary"),
        name="moe_dispatch",
    )(pos_flat, pad_start, tn)


def _swiglu(hg, hl):
    glu = jnp.minimum(hg, SWIGLU_LIMIT)
    lin = jnp.clip(hl, -SWIGLU_LIMIT, SWIGLU_LIMIT)
    return glu * _sigmoid(SWIGLU_ALPHA * glu) * (lin + 1.0)


def _expert_changed(te_ref, i):
    return (i == 0) | (te_ref[i] != te_ref[jnp.maximum(i - 1, 0)])


def _gemm1_body(te_ref, tv_ref, x_ref, wg_ref, wl_ref, bg_ref, bl_ref, h_ref, wgb_ref, wlb_ref):
    i = pl.program_id(1)

    @pl.when(_expert_changed(te_ref, i))
    def _():
        wgb_ref[...] = wg_ref[0].astype(BF16)
        wlb_ref[...] = wl_ref[0].astype(BF16)

    @pl.when(tv_ref[i] == 1)
    def _():
        x_hi, x_lo = _unpack_bf16_pairs(x_ref[...])
        half = x_hi.shape[1]
        hg = _dot(x_hi, wgb_ref[0:half, :]) + _dot(x_lo, wgb_ref[half:, :]) + bg_ref[0]
        hl = _dot(x_hi, wlb_ref[0:half, :]) + _dot(x_lo, wlb_ref[half:, :]) + bl_ref[0]
        h_ref[...] = _swiglu(hg, hl).astype(h_ref.dtype)

    @pl.when(tv_ref[i] == 0)
    def _():
        h_ref[...] = jnp.zeros(h_ref.shape, h_ref.dtype)


def _gemm2_body(te_ref, tv_ref, h_ref, w_ref, b_ref, y_ref, wb_ref):
    i = pl.program_id(1)

    @pl.when(_expert_changed(te_ref, i))
    def _():
        wb_ref[...] = w_ref[0].astype(BF16)

    @pl.when(tv_ref[i] == 1)
    def _():
        y_ref[...] = _dot(h_ref[...], wb_ref[...]) + b_ref[0]

    @pl.when(tv_ref[i] == 0)
    def _():
        y_ref[...] = jnp.zeros(y_ref.shape, y_ref.dtype)


def _expert_gemms(xs, tile_expert, tile_valid, w1, b1, w2, b2, *, tm, tf, tn):
    n_rows = xs.shape[0]
    n_exp, d, f2 = w1.shape
    f = f2 // 2
    tf, tn = min(tf, f), min(tn, d)
    nf = f // tf
    n_tiles = n_rows // tm
    b1r = b1.reshape(n_exp, 1, f2)
    b2r = b2.reshape(n_exp, 1, d)
    grid1 = pltpu.PrefetchScalarGridSpec(
        num_scalar_prefetch=2, grid=(nf, n_tiles),
        in_specs=[pl.BlockSpec((tm, d // 2), lambda j, i, te, tv: (i, 0)),
                  pl.BlockSpec((1, d, tf), lambda j, i, te, tv: (te[i], 0, j)),
                  pl.BlockSpec((1, d, tf), lambda j, i, te, tv: (te[i], 0, nf + j)),
                  pl.BlockSpec((1, 1, tf), lambda j, i, te, tv: (te[i], 0, j)),
                  pl.BlockSpec((1, 1, tf), lambda j, i, te, tv: (te[i], 0, nf + j))],
        out_specs=pl.BlockSpec((tm, tf), lambda j, i, te, tv: (i, j)),
        scratch_shapes=[pltpu.VMEM((d, tf), BF16), pltpu.VMEM((d, tf), BF16)])
    hidden = pl.pallas_call(
        _gemm1_body, out_shape=jax.ShapeDtypeStruct((n_rows, f), BF16), grid_spec=grid1,
        compiler_params=_params("arbitrary", "arbitrary"), name="moe_gemm1",
    )(tile_expert, tile_valid, xs, w1, w1, b1r, b1r)
    grid2 = pltpu.PrefetchScalarGridSpec(
        num_scalar_prefetch=2, grid=(d // tn, n_tiles),
        in_specs=[pl.BlockSpec((tm, f), lambda j, i, te, tv: (i, 0)),
                  pl.BlockSpec((1, f, tn), lambda j, i, te, tv: (te[i], 0, j)),
                  pl.BlockSpec((1, 1, tn), lambda j, i, te, tv: (te[i], 0, j))],
        out_specs=pl.BlockSpec((tm, tn), lambda j, i, te, tv: (i, j)),
        scratch_shapes=[pltpu.VMEM((f, tn), BF16)])
    return pl.pallas_call(
        _gemm2_body, out_shape=jax.ShapeDtypeStruct((n_rows, d), F32), grid_spec=grid2,
        compiler_params=_params("arbitrary", "arbitrary"), name="moe_gemm2",
    )(tile_expert, tile_valid, hidden, w2, b2r)


def _combine_body(pos_ref, y_ref, gates_ref, h_ref, nw_ref, o_ref, buf_ref, sem, *, tb, normalize):
    base = pl.program_id(0) * tb * TOP_K

    def issue(tok, carry):
        for k in range(TOP_K):
            _row_copy(y_ref, pos_ref[base + tok * TOP_K + k], buf_ref.at[k], tok, sem).start()
        return carry

    def drain(tok, carry):
        for _ in range(TOP_K):
            _row_copy(y_ref, 0, buf_ref.at[0], 0, sem).wait()
        return carry

    lax.fori_loop(0, tb, issue, 0, unroll=4)
    lax.fori_loop(0, tb, drain, 0, unroll=4)
    gates = gates_ref[...]
    out = h_ref[...]
    for k in range(TOP_K):
        out = out + gates[:, k:k + 1] * buf_ref[k]
    if normalize:
        ms = jnp.mean(out * out, axis=-1, keepdims=True)
        out = out * lax.rsqrt(ms + EPS) * nw_ref[...]
    o_ref[...] = out


def _combine(pos_flat, y, gates, h1, final_w, tb, normalize):
    t, d = h1.shape
    grid_spec = pltpu.PrefetchScalarGridSpec(
        num_scalar_prefetch=1, grid=(t // tb,),
        in_specs=[pl.BlockSpec(memory_space=pl.ANY),
                  pl.BlockSpec((tb, LANES), lambda i, pos: (i, 0)),
                  pl.BlockSpec((tb, d), lambda i, pos: (i, 0)),
                  pl.BlockSpec((1, d), lambda i, pos: (0, 0))],
        out_specs=pl.BlockSpec((tb, d), lambda i, pos: (i, 0)),
        scratch_shapes=[pltpu.VMEM((TOP_K, tb, d), F32), pltpu.SemaphoreType.DMA])
    return pl.pallas_call(
        functools.partial(_combine_body, tb=tb, normalize=normalize),
        out_shape=jax.ShapeDtypeStruct((t, d), F32),
        grid_spec=grid_spec,
        compiler_params=_params("arbitrary"),
        name="moe_combine",
    )(pos_flat, y, gates, h1, final_w.reshape(1, d))


def _moe(h1, norm_w, router_w, router_b, w1, b1, w2, b2, final_w, normalize, *, tm_router, tm_expert,
         tf, tn, tb_dispatch, tb):
    t, d = h1.shape
    tn_tokens, ids, gates, rank, cnt = _router(h1, norm_w, router_w, router_b, tm_router)
    counts = cnt[0, :N_EXPERTS].astype(I32)
    padded = (counts + tm_expert - 1) // tm_expert * tm_expert
    seg_end = jnp.cumsum(padded)
    seg_start = seg_end - padded
    expert_of = ids[:, :TOP_K, None] == jnp.arange(N_EXPERTS, dtype=I32)
    pos = (jnp.sum(jnp.where(expert_of, seg_start, 0), axis=-1) + rank[:, :TOP_K]).reshape(-1)
    n_rows = t * TOP_K + N_EXPERTS * tm_expert
    n_tiles = n_rows // tm_expert
    tile_ids = jnp.arange(n_tiles, dtype=I32)
    tiles_done = jnp.sum((tile_ids[:, None] >= (seg_end // tm_expert)[None, :]).astype(I32), axis=1)
    tile_expert = jnp.minimum(tiles_done, N_EXPERTS - 1)
    tile_valid = (tile_ids < seg_end[-1] // tm_expert).astype(I32)
    expert_pad = jnp.where(padded > 0, seg_end - tm_expert, -1)
    tail_pad = seg_end[-1] + jnp.arange(N_EXPERTS, dtype=I32) * tm_expert
    tail_pad = jnp.where(tail_pad < n_rows, tail_pad, -1)
    xs = _dispatch(pos, jnp.concatenate([expert_pad, tail_pad]), tn_tokens, n_rows, tb_dispatch, tm_expert)
    y = _expert_gemms(xs, tile_expert, tile_valid, w1, b1, w2, b2, tm=tm_expert, tf=tf, tn=tn)
    return _combine(pos, y, gates, h1, final_w, tb, normalize)


def _layer(x2, batch, seq, w_in, conv_w, conv_b, dt_bias, a_log, d_skip, ssd_norm_w, w_ssd_branch, w_attn_branch,
           w_out, norm_mix_w, norm_ffn_w, router_w, router_b, w1, b1, w2, b2, final_w, normalize):
    t, d = x2.shape
    ssd_inner = dt_bias.shape[0] * SSD_HEAD_DIM
    conv_dim = conv_w.shape[1]
    attn_inner = w_attn_branch.shape[0]
    idx_q = IDX_HEADS * IDX_DIM
    tm = min(ROW_TILE, seq)

    sizes = (ssd_inner, conv_dim, dt_bias.shape[0], attn_inner, attn_inner, attn_inner,
             idx_q, IDX_DIM, IDX_HEADS, d, d)
    offs = [0]
    for s in sizes:
        offs.append(offs[-1] + s)
    col = lambda i: w_in[:, offs[i]:offs[i + 1]]
    w_z, w_xbc, w_dt, w_q, w_k, w_v, w_qi, w_ki, w_wi, w_gs, w_ga = (col(i) for i in range(len(sizes)))
    zeros_k = jnp.zeros_like(w_ki)
    w_ki2 = jnp.concatenate([w_ki, zeros_k, zeros_k, w_ki], axis=1)
    n_misc = w_dt.shape[1] + w_wi.shape[1]
    w_misc = jnp.pad(jnp.concatenate([w_dt, w_wi], axis=1), ((0, 0), (0, LANES - n_misc)))
    b16 = lambda w: w.astype(BF16)

    xn = _rmsnorm(x2, norm_mix_w, BF16, tm)
    mm = functools.partial(_matmul, tm=tm)
    z = mm(xn, b16(w_z), F32, "proj_z")
    xbc = mm(xn, b16(w_xbc), F32, "proj_xbc")
    gate_logits = mm(xn, b16(jnp.concatenate([w_gs, w_ga], axis=1)), F32, "proj_gates")
    misc = mm(xn, b16(w_misc), F32, "proj_misc")
    v = mm(xn, b16(w_v), BF16, "proj_v")
    attn_tabs = _rope_tables(seq, ATTN_HEAD_DIM // 4, ATTN_HEAD_DIM)
    idx_tabs = _rope_tables(seq, IDX_DIM // 4, IDX_DIM)
    qk = _rope_matmul(xn, b16(jnp.concatenate([w_q, w_k], axis=1)), attn_tabs, ATTN_HEAD_DIM // 4, seq,
                      "proj_qk", tm=tm)
    qi = _rope_matmul(xn, b16(w_qi), idx_tabs, IDX_DIM // 4, seq, "proj_qi", tm=tm)
    ki = _rope_matmul(xn, b16(w_ki2), idx_tabs, IDX_DIM // 4, seq, "proj_ki", tm=tm)

    y_ssd = _ssd(xbc, z, misc, conv_w, conv_b, dt_bias, a_log, d_skip, ssd_norm_w, batch, seq)
    mask = _indexer_mask(qi, ki, misc, batch, seq, w_lane=w_dt.shape[1])
    y_attn = _attention(qk, v, mask, batch, seq, tq=ATTN_Q_TILE, tk=ATTN_K_TILE)

    g_spec = lambda off: lambda tm_, tn_: pl.BlockSpec((tm_, tn_), lambda j, i: (i, off * (d // tn_) + j))
    same = lambda tm_, tn_: pl.BlockSpec((tm_, tn_), lambda j, i: (i, j))
    part = mm(y_ssd, b16(w_ssd_branch), F32, "merge_ssd", epilogue=_ep_gate, extras=(gate_logits,),
              extra_specs=(g_spec(0),))
    merged = mm(y_attn, b16(w_attn_branch), BF16, "merge_attn", epilogue=_ep_gate_add,
                extras=(gate_logits, part), extra_specs=(g_spec(1), same))
    h1 = mm(merged, b16(w_out), F32, "out_proj", epilogue=_ep_residual, extras=(x2,), extra_specs=(same,))

    return _moe(h1, norm_ffn_w, router_w, router_b, w1, b1, w2, b2, final_w, normalize,
                tm_router=min(ROUTER_ROW_TILE, t), tm_expert=EXPERT_ROW_TILE, tf=EXPERT_HIDDEN_TILE,
                tn=EXPERT_OUT_TILE, tb_dispatch=min(DISPATCH_TOKENS, t), tb=min(COMBINE_TOKENS, t))


def kernel(x, w_in, conv_w, conv_b, dt_bias, a_log, d_skip, ssd_norm_w, w_ssd_branch, w_attn_branch, w_out,
           norm_mix_w, norm_ffn_w, router_w, router_b, w_expert_in, b_expert_in, w_expert_out, b_expert_out,
           final_norm_w):
    batch, seq, d = x.shape
    depth = w_in.shape[0]
    per_layer = (w_in, conv_w, conv_b, dt_bias, a_log, d_skip, ssd_norm_w, w_ssd_branch, w_attn_branch, w_out,
                 norm_mix_w, norm_ffn_w, router_w, router_b, w_expert_in, b_expert_in, w_expert_out, b_expert_out)
    h = x.reshape(batch * seq, d)
    for layer in range(depth):
        h = _layer(h, batch, seq, *(p[layer] for p in per_layer), final_norm_w, layer == depth - 1)
    return h.reshape(batch, seq, d)
```

```python
import functools
import math

import jax
import jax.numpy as jnp
from jax import lax
from jax.experimental import pallas as pl
from jax.experimental.pallas import tpu as pltpu

EPS = 1e-5
SSD_HEADS = 32
SSD_HEAD_DIM = 64
SSD_GROUPS = 4
SSD_STATE = 128
SSD_CONV = 4
SSD_CHUNK = 128
ATTN_HEADS = 16
ATTN_HEAD_DIM = 128
ROPE_THETA = 500000.0
IDX_HEADS = 8
IDX_DIM = 64
IDX_TOPK = 256
Q_BLOCK = 128
N_EXPERTS = 32
TOP_K = 4
D_EXPERT = 2048
SWIGLU_ALPHA = 1.702
SWIGLU_LIMIT = 7.0

LANES = 128
SUBLANES = 8
VMEM_LIMIT_BYTES = 56 * 2**20
INT32_MIN = -2**31
MASKED_LOGIT = -1e30

ROW_TILE = 512
ATTN_Q_TILE = 256
ATTN_K_TILE = 1024
ATTN_HEAD_GROUP = 4
INDEXER_KEY_CHUNK = 512
INDEXER_ROW_GROUPS = 4
ROUTER_ROW_TILE = 256
EXPERT_ROW_TILE = 256
EXPERT_HIDDEN_TILE = 1024
EXPERT_OUT_TILE = 2048
DISPATCH_TOKENS = 512
COMBINE_TOKENS = 128

F32 = jnp.float32
BF16 = jnp.bfloat16
I32 = jnp.int32


def _params(*sem):
    return pltpu.CompilerParams(dimension_semantics=sem, vmem_limit_bytes=VMEM_LIMIT_BYTES)


def _sigmoid(x):
    return 1.0 / (1.0 + jnp.exp(-x))


def _dot(a, b):
    return jnp.dot(a, b, preferred_element_type=F32)


def _dot_nt(a, b):
    return lax.dot_general(a, b, (((1,), (1,)), ((), ())), preferred_element_type=F32)


def _split3(x):
    hi = x.astype(BF16)
    r1 = x - hi.astype(F32)
    mid = r1.astype(BF16)
    lo = (r1 - mid.astype(F32)).astype(BF16)
    return hi, mid, lo


def _rmsnorm_body(x_ref, w_ref, o_ref):
    x = x_ref[...]
    ms = jnp.mean(x * x, axis=-1, keepdims=True)
    o_ref[...] = (x * lax.rsqrt(ms + EPS) * w_ref[...]).astype(o_ref.dtype)


def _rmsnorm(x, w, out_dtype, tm):
    t, d = x.shape
    return pl.pallas_call(
        _rmsnorm_body,
        out_shape=jax.ShapeDtypeStruct((t, d), out_dtype),
        grid=(t // tm,),
        in_specs=[pl.BlockSpec((tm, d), lambda i: (i, 0)), pl.BlockSpec((1, d), lambda i: (0, 0))],
        out_specs=pl.BlockSpec((tm, d), lambda i: (i, 0)),
        compiler_params=_params("parallel"),
        name="rmsnorm",
    )(x, w.reshape(1, d))


def _ep_identity(acc):
    return acc


def _ep_rope(acc, c, s1, s2, *, shift):
    outs = []
    for g in range(acc.shape[1] // LANES):
        o = acc[:, g * LANES:(g + 1) * LANES]
        outs.append(o * c + pltpu.roll(o, LANES - shift, 1) * s1 + pltpu.roll(o, shift, 1) * s2)
    return outs[0] if len(outs) == 1 else jnp.concatenate(outs, axis=1)


def _ep_gate(acc, g):
    return _sigmoid(g) * acc


def _ep_gate_add(acc, g, p):
    return p + _sigmoid(g) * acc


def _ep_residual(acc, x):
    return x + acc


def _mm_body(*refs, epilogue):
    a_ref, w_ref = refs[0], refs[1]
    o_ref = refs[-1]
    acc = _dot(a_ref[...], w_ref[...])
    o_ref[...] = epilogue(acc, *[r[...] for r in refs[2:-1]]).astype(o_ref.dtype)


MAX_COL_TILE = 2048


def _col_tile(n):
    return max(t for t in range(LANES, min(n, MAX_COL_TILE) + 1, LANES) if n % t == 0)


def _matmul(a, w, out_dtype, name, *, tm, epilogue=_ep_identity, extras=(), extra_specs=()):
    m, k = a.shape
    n = w.shape[1]
    tn = _col_tile(n)
    return pl.pallas_call(
        functools.partial(_mm_body, epilogue=epilogue),
        out_shape=jax.ShapeDtypeStruct((m, n), out_dtype),
        grid=(n // tn, m // tm),
        in_specs=[pl.BlockSpec((tm, k), lambda j, i: (i, 0)),
                  pl.BlockSpec((k, tn), lambda j, i: (0, j)), *[f(tm, tn) for f in extra_specs]],
        out_specs=pl.BlockSpec((tm, tn), lambda j, i: (i, j)),
        compiler_params=_params("parallel", "parallel"),
        name=name,
    )(a, w, *extras)


def _rope_tables(length, rot_dim, head_dim):
    half = rot_dim // 2
    inv = ROPE_THETA ** (-jnp.arange(0, rot_dim, 2, dtype=F32) / rot_dim)
    ang = jnp.arange(length, dtype=F32)[:, None] * inv[None, :]
    cos, sin = jnp.cos(ang), jnp.sin(ang)
    zeros = lambda n: jnp.zeros((length, n), F32)
    c = jnp.concatenate([cos, cos, jnp.ones((length, head_dim - rot_dim), F32)], axis=1)
    s1 = jnp.concatenate([-sin, zeros(head_dim - half)], axis=1)
    s2 = jnp.concatenate([zeros(half), sin, zeros(head_dim - rot_dim)], axis=1)
    reps = LANES // head_dim
    return tuple(jnp.tile(t, (1, reps)) for t in (c, s1, s2))


def _rope_matmul(a, w, tables, rot_dim, seq, name, *, tm):
    nblk = seq // tm
    spec = lambda tm_, tn_: pl.BlockSpec((tm_, LANES), lambda j, i: (i % nblk, 0))
    return _matmul(a, w, BF16, name, tm=tm,
                   epilogue=functools.partial(_ep_rope, shift=rot_dim // 2),
                   extras=tables, extra_specs=(spec, spec, spec))


def _ssd_body(xbc_ref, z_ref, misc_ref, cw_ref, cb_ref, dtb_ref, alog_ref, dsk_ref, nw_ref,
              ltri_ref, exp_ref, o_ref, tail_ref, state_ref, *, inner, groups, heads):
    ch = SSD_CHUNK
    n = SSD_STATE
    gw = inner // groups
    c_idx = pl.program_id(1)
    tail = SSD_CONV - 1

    @pl.when(c_idx == 0)
    def _():
        tail_ref[...] = jnp.zeros(tail_ref.shape, F32)
        state_ref[...] = jnp.zeros(state_ref.shape, F32)

    cur = xbc_ref[...]
    prev_rows = tail_ref[...]
    row8 = lax.broadcasted_iota(I32, (SUBLANES, cur.shape[1]), 0)
    conv = cb_ref[...] + cur * cw_ref[tail:tail + 1, :]
    for j in range(tail):
        shift = tail - j
        rolled = pltpu.roll(cur, shift, 0)
        head = jnp.where(row8 < shift, pltpu.roll(prev_rows, shift, 0), rolled[0:SUBLANES, :])
        conv = conv + jnp.concatenate([head, rolled[SUBLANES:, :]], axis=0) * cw_ref[j:j + 1, :]
    tail_ref[...] = cur[ch - SUBLANES:, :]
    xbc = conv * _sigmoid(conv)
    xs = xbc[:, :inner]
    bm = xbc[:, inner:inner + groups * n]
    cm = xbc[:, inner + groups * n:]

    lane = lax.broadcasted_iota(I32, (1, LANES), 1)
    head_lane = lane < heads
    a = jnp.where(head_lane, -jnp.exp(alog_ref[...]), 0.0)
    dtr = misc_ref[...] + dtb_ref[...]
    dt = jnp.maximum(dtr, 0.0) + jnp.log1p(jnp.exp(-jnp.abs(dtr)))
    da = dt * a

    ltri = ltri_ref[...]
    cs = sum(_dot(ltri, p) for p in _split3(da))
    expand = exp_ref[...]
    dt_e = sum(_dot(p, expand) for p in _split3(dt))
    cs_e = sum(_dot(p, expand) for p in _split3(cs))
    cs_last = cs_e[ch - 1:ch, :]
    ecs = jnp.exp(cs_e)
    dte = jnp.exp(cs_last - cs_e)
    chunk_decay = jnp.exp(cs_last)

    xdt = xs * dt_e
    xdt_b = xdt.astype(BF16)
    xw_b = (xdt * dte).astype(BF16)
    cs_t = cs.T

    row = lax.broadcasted_iota(I32, (ch, ch), 0)
    col = lax.broadcasted_iota(I32, (ch, ch), 1)
    causal = row >= col
    first_half = lax.broadcasted_iota(I32, (ch, LANES), 1) < SSD_HEAD_DIM
    heads_per_group = heads // groups

    diag_cols, off_cols = [], []
    for g in range(groups):
        bg = bm[:, g * n:(g + 1) * n]
        cg_b = cm[:, g * n:(g + 1) * n].astype(BF16)
        cb = _dot_nt(cg_b, bg.astype(BF16))
        bg_t = bg.T.astype(BF16)
        for pr in range(heads_per_group // 2):
            h0 = g * heads_per_group + 2 * pr
            xp = xdt_b[:, h0 * SSD_HEAD_DIM:(h0 + 2) * SSD_HEAD_DIM]
            res = []
            for h in (h0, h0 + 1):
                seg = cs[:, h:h + 1] - cs_t[h:h + 1, :]
                decay = jnp.exp(jnp.where(causal, seg, -jnp.inf))
                res.append(_dot((cb * decay).astype(BF16), xp))
            diag_cols.append(jnp.where(first_half, res[0], res[1]))
        prev = state_ref[g]
        off_cols.append(_dot(cg_b, prev.astype(BF16)) * ecs[:, g * gw:(g + 1) * gw])
        states = _dot(bg_t, xw_b[:, g * gw:(g + 1) * gw])
        state_ref[g] = prev * chunk_decay[:, g * gw:(g + 1) * gw] + states

    y = jnp.concatenate(diag_cols, axis=1) + jnp.concatenate(off_cols, axis=1) + xs * dsk_ref[...]

    z = z_ref[...]
    gated = y * (z * _sigmoid(z))
    ms = jnp.mean(gated * gated, axis=-1, keepdims=True)
    o_ref[...] = (gated * lax.rsqrt(ms + EPS) * nw_ref[...]).astype(o_ref.dtype)


def _ssd(xbc, z, misc, conv_w, conv_b, dt_bias, a_log, d_skip, norm_w, batch, seq):
    t, conv_dim = xbc.shape
    inner = z.shape[1]
    heads = dt_bias.shape[0]
    groups = SSD_GROUPS
    ch = SSD_CHUNK
    nc = seq // ch
    pad = lambda v: jnp.pad(v.astype(F32), (0, LANES - heads)).reshape(1, LANES)
    ltri = jnp.tril(jnp.ones((ch, ch), F32)).astype(BF16)
    head_of_lane = jnp.arange(inner) // SSD_HEAD_DIM
    expand = (jnp.arange(LANES)[:, None] == head_of_lane[None, :]).astype(BF16)
    d_exp = jnp.repeat(d_skip.astype(F32), SSD_HEAD_DIM).reshape(1, inner)
    row = lambda b, c: (b * nc + c, 0)
    const = lambda b, c: (0, 0)
    return pl.pallas_call(
        functools.partial(_ssd_body, inner=inner, groups=groups, heads=heads),
        out_shape=jax.ShapeDtypeStruct((t, inner), BF16),
        grid=(batch, nc),
        in_specs=[pl.BlockSpec((ch, conv_dim), row), pl.BlockSpec((ch, inner), row),
                  pl.BlockSpec((ch, LANES), row),
                  pl.BlockSpec((SSD_CONV, conv_dim), const), pl.BlockSpec((1, conv_dim), const),
                  pl.BlockSpec((1, LANES), const), pl.BlockSpec((1, LANES), const),
                  pl.BlockSpec((1, inner), const), pl.BlockSpec((1, inner), const),
                  pl.BlockSpec((ch, ch), const), pl.BlockSpec((LANES, inner), const)],
        out_specs=pl.BlockSpec((ch, inner), row),
        scratch_shapes=[pltpu.VMEM((SUBLANES, conv_dim), F32),
                        pltpu.VMEM((groups, SSD_STATE, inner // groups), F32)],
        compiler_params=_params("arbitrary", "arbitrary"),
        name="ssd_scan",
    )(xbc, z, misc, conv_w, conv_b.reshape(1, conv_dim), pad(dt_bias), pad(a_log), d_exp,
      norm_w.reshape(1, inner), ltri, expand)


def _indexer_body(qi_ref, ki_ref, misc_ref, u_ref, ones_ref, mask_ref, keys_ref, *,
                  seq, kc, rows, group_rows, topk, w_lane, idx_scale):
    qb = pl.program_id(1)
    q0 = qb * rows
    n_chunks = (q0 + rows + kc - 1) // kc
    n_unmasked = (q0 + 1) // kc
    n_lane_chunks = kc // LANES
    lane_chunk = lambda a, j: a[:, j * LANES:(j + 1) * LANES]
    w = misc_ref[...] * idx_scale
    qi = qi_ref[...]
    ones = ones_ref[...]

    def score_chunk(c, carry, *, masked):
        off = pl.multiple_of(c * kc, kc)
        k_lo = ki_ref[pl.ds(off, kc), 0:LANES]
        k_hi = ki_ref[pl.ds(off, kc), LANES:2 * LANES]
        s = jnp.zeros((rows, kc), F32)
        for j in range(IDX_HEADS // 2):
            qj = lane_chunk(qi, j)
            w0 = w[:, w_lane + 2 * j:w_lane + 2 * j + 1]
            w1 = w[:, w_lane + 2 * j + 1:w_lane + 2 * j + 2]
            s = s + w0 * jnp.maximum(_dot_nt(qj, k_lo), 0.0) + w1 * jnp.maximum(_dot_nt(qj, k_hi), 0.0)
        bits = lax.bitcast_convert_type(s, I32)
        key = jnp.where(bits >= 0, bits, bits ^ jnp.int32(0x7FFFFFFF))
        if masked:
            qpos = q0 + lax.broadcasted_iota(I32, (rows, kc), 0)
            kpos = off + lax.broadcasted_iota(I32, (rows, kc), 1)
            key = jnp.where(kpos <= qpos, key, jnp.int32(INT32_MIN))
        keys_ref[:, pl.ds(off, kc)] = key
        return carry

    lax.fori_loop(0, n_unmasked, functools.partial(score_chunk, masked=False), 0)
    lax.fori_loop(n_unmasked, n_chunks, functools.partial(score_chunk, masked=True), 0)

    groups = [slice(g * group_rows, (g + 1) * group_rows) for g in range(rows // group_rows)]

    def count(compare, thresh):
        accs = []
        for g, rows_g in enumerate(groups):
            thresh_g = thresh[rows_g]

            def body(c, acc, rows_g=rows_g, thresh_g=thresh_g):
                k = keys_ref[rows_g, pl.ds(pl.multiple_of(c * kc, kc), kc)]
                for j in range(n_lane_chunks):
                    acc = acc + jnp.where(compare(lane_chunk(k, j), thresh_g), 1, 0)
                return acc

            chunks_g = (q0 + (g + 1) * group_rows + kc - 1) // kc
            accs.append(lax.fori_loop(0, chunks_g, body, jnp.zeros((group_rows, LANES), I32)))
        acc = jnp.concatenate(accs, axis=0)
        return _dot(acc.astype(F32).astype(BF16), ones)

    def bit_body(i, state):
        t_u, n_ge = state
        cand_u = t_u | jnp.left_shift(jnp.int32(1), 31 - i)
        cnt = count(lambda k, c: k >= c, cand_u ^ jnp.int32(INT32_MIN))
        keep = cnt >= topk
        return jnp.where(keep, cand_u, t_u), jnp.where(keep, cnt, n_ge)

    n_admissible = (q0 + lax.broadcasted_iota(I32, (rows, LANES), 0) + 1).astype(F32)
    t_u, n_ge = lax.fori_loop(0, 32, bit_body, (jnp.zeros((rows, LANES), I32), n_admissible))
    t_s = t_u ^ jnp.int32(INT32_MIN)
    n_gt = count(lambda k, c: k > c, t_s)
    select_all = t_u == 0
    n_tie = jnp.where(select_all, 0.0, topk - n_gt)
    excess_ties = jnp.max(jnp.where(select_all, 0.0, n_ge - topk)) > 0.0

    @pl.when(jnp.logical_not(excess_ties))
    def _():
        def mask_chunk(c, carry):
            sl = pl.ds(pl.multiple_of(c * kc, kc), kc)
            k = keys_ref[:, sl]
            sel = [(lane_chunk(k, j) > t_s) | ((lane_chunk(k, j) == t_s) & jnp.logical_not(select_all))
                   for j in range(n_lane_chunks)]
            mask_ref[:, sl] = jnp.where(jnp.concatenate(sel, axis=1), 1, 0).astype(jnp.int8)
            return carry
        lax.fori_loop(0, n_chunks, mask_chunk, 0)

    @pl.when(excess_ties)
    def _():
        t_col = t_s[:, 0:1]
        n_tie_col = n_tie[:, 0:1]

        def mask_chunk(c, seen):
            sl = pl.ds(pl.multiple_of(c * kc, kc), kc)
            k = keys_ref[:, sl]
            tie = k == t_col
            tie_f = jnp.where(tie, 1.0, 0.0)
            rank = seen + _dot(tie_f.astype(BF16), u_ref[...])
            sel = (k > t_col) | (tie & (rank <= n_tie_col))
            mask_ref[:, sl] = jnp.where(sel, 1, 0).astype(jnp.int8)
            return seen + jnp.sum(tie_f, axis=1, keepdims=True)
        lax.fori_loop(0, n_chunks, mask_chunk, jnp.zeros((rows, 1), F32))

    def zero_chunk(c, carry):
        mask_ref[:, pl.ds(pl.multiple_of(c * kc, kc), kc)] = jnp.zeros((rows, kc), jnp.int8)
        return carry

    lax.fori_loop(n_chunks, seq // kc, zero_chunk, 0)


def _indexer_mask(qi, ki, misc, batch, seq, w_lane):
    t = qi.shape[0]
    group_rows = min(Q_BLOCK, seq)
    rows = min(INDEXER_ROW_GROUPS * group_rows, seq)
    nq = seq // rows
    kc = min(INDEXER_KEY_CHUNK, seq)
    topk = min(IDX_TOPK, seq // 4)
    assert seq // LANES <= 256, "per-lane counts must stay exact in bf16"
    upper = jnp.triu(jnp.ones((kc, kc), F32)).astype(BF16)
    ones = jnp.ones((LANES, LANES), BF16)
    idx_scale = (IDX_DIM ** -0.5) * (IDX_HEADS ** -0.5)
    return pl.pallas_call(
        functools.partial(_indexer_body, seq=seq, kc=kc, rows=rows, group_rows=group_rows, topk=topk, w_lane=w_lane,
                          idx_scale=idx_scale),
        out_shape=jax.ShapeDtypeStruct((t, seq), jnp.int8),
        grid=(batch, nq),
        in_specs=[pl.BlockSpec((rows, qi.shape[1]), lambda b, q: (b * nq + q, 0)),
                  pl.BlockSpec((seq, ki.shape[1]), lambda b, q: (b, 0)),
                  pl.BlockSpec((rows, LANES), lambda b, q: (b * nq + q, 0)),
                  pl.BlockSpec((kc, kc), lambda b, q: (0, 0)),
                  pl.BlockSpec((LANES, LANES), lambda b, q: (0, 0))],
        out_specs=pl.BlockSpec((rows, seq), lambda b, q: (b * nq + q, 0)),
        scratch_shapes=[pltpu.VMEM((rows, seq), I32)],
        compiler_params=_params("parallel", "arbitrary"),
        name="indexer_topk_mask",
    )(qi, ki, misc, upper, ones)


def _attn_body(q_ref, k_ref, v_ref, mask_ref, o_ref, acc_ref, m_ref, l_ref, *, tq, tk, heads, group, scale_log2e):
    qb = pl.program_id(1)
    kb = pl.program_id(2)
    last = ((qb + 1) * tq - 1) // tk
    hd = ATTN_HEAD_DIM

    @pl.when(kb == 0)
    def _():
        acc_ref[...] = jnp.zeros(acc_ref.shape, F32)
        m_ref[...] = jnp.full(m_ref.shape, MASKED_LOGIT, F32)
        l_ref[...] = jnp.zeros(l_ref.shape, F32)

    @pl.when(kb <= last)
    def _():
        bias = jnp.where(mask_ref[...].astype(I32) != 0, 0.0, MASKED_LOGIT)
        for h0 in range(0, heads, group):
            hs = range(h0, min(h0 + group, heads))
            col = lambda h: slice(h * hd, (h + 1) * hd)
            s = [_dot_nt(q_ref[:, col(h)], k_ref[:, col(h)]) * scale_log2e + bias for h in hs]
            m_new = [jnp.maximum(m_ref[h], jnp.max(sh, axis=1, keepdims=True)) for h, sh in zip(hs, s)]
            p = [jnp.exp2(sh - mh[:, 0:1]) for sh, mh in zip(s, m_new)]
            for h, ph, mh in zip(hs, p, m_new):
                alpha = jnp.exp2(m_ref[h] - mh)
                l_ref[h] = alpha * l_ref[h] + jnp.sum(ph, axis=1, keepdims=True)
                acc_ref[:, col(h)] = acc_ref[:, col(h)] * alpha + _dot(ph.astype(BF16), v_ref[:, col(h)])
                m_ref[h] = mh

    @pl.when(kb == pl.num_programs(2) - 1)
    def _():
        for h in range(heads):
            cols = slice(h * hd, (h + 1) * hd)
            o_ref[:, cols] = (acc_ref[:, cols] / l_ref[h]).astype(o_ref.dtype)


def _attention(qk, v, mask, batch, seq, *, tq, tk, group=ATTN_HEAD_GROUP):
    t, inner = v.shape
    heads = inner // ATTN_HEAD_DIM
    tq, tk = min(tq, seq), min(tk, seq)
    nq, nk = seq // tq, seq // tk
    last = lambda q: ((q + 1) * tq - 1) // tk
    return pl.pallas_call(
        functools.partial(_attn_body, tq=tq, tk=tk, heads=heads, group=group,
                          scale_log2e=ATTN_HEAD_DIM ** -0.5 * math.log2(math.e)),
        out_shape=jax.ShapeDtypeStruct((t, inner), BF16),
        grid=(batch, nq, nk),
        in_specs=[pl.BlockSpec((tq, inner), lambda b, q, k: (b * nq + q, 0)),
                  pl.BlockSpec((tk, inner), lambda b, q, k: (b * nk + jnp.minimum(k, last(q)), 1)),
                  pl.BlockSpec((tk, inner), lambda b, q, k: (b * nk + jnp.minimum(k, last(q)), 0)),
                  pl.BlockSpec((tq, tk), lambda b, q, k: (b * nq + q, jnp.minimum(k, last(q))))],
        out_specs=pl.BlockSpec((tq, inner), lambda b, q, k: (b * nq + q, 0)),
        scratch_shapes=[pltpu.VMEM((tq, inner), F32),
                        pltpu.VMEM((heads, tq, LANES), F32),
                        pltpu.VMEM((heads, tq, LANES), F32)],
        compiler_params=_params("parallel", "parallel", "arbitrary"),
        name="masked_attention",
    )(qk, qk, v, mask)


def _pack_bf16_pairs(x):
    h = x.shape[1] // 2
    hi = lax.bitcast_convert_type(x[:, :h].astype(F32), jnp.uint32)
    lo = lax.bitcast_convert_type(x[:, h:].astype(F32), jnp.uint32)
    return lax.bitcast_convert_type(hi | (lo >> 16), I32)


def _unpack_bf16_pairs(p):
    u = lax.bitcast_convert_type(p, jnp.uint32)
    hi = lax.bitcast_convert_type(u & jnp.uint32(0xFFFF0000), F32).astype(BF16)
    lo = lax.bitcast_convert_type(u << 16, F32).astype(BF16)
    return hi, lo


def _router_body(h_ref, nw_ref, rw_ref, rb_ref, ltri_ref, tn_ref, ids_ref, gates_ref, rank_ref, cnt_ref,
                 carry_ref):
    @pl.when(pl.program_id(0) == 0)
    def _():
        carry_ref[...] = jnp.zeros(carry_ref.shape, F32)

    x = h_ref[...]
    ms = jnp.mean(x * x, axis=-1, keepdims=True)
    tn = (x * lax.rsqrt(ms + EPS) * nw_ref[...]).astype(BF16)
    tn_ref[...] = _pack_bf16_pairs(tn)
    tm = x.shape[0]
    lane = lax.broadcasted_iota(I32, (tm, LANES), 1)
    logits = _dot(tn, rw_ref[...]) + rb_ref[...]
    work = jnp.where(lane < N_EXPERTS, logits, -jnp.inf)
    vals, hits = [], []
    for _ in range(TOP_K):
        mx = jnp.max(work, axis=1, keepdims=True)
        idx = jnp.min(jnp.where(work == mx, lane, LANES), axis=1, keepdims=True)
        hit = lane == idx
        work = jnp.where(hit, -jnp.inf, work)
        vals.append(mx)
        hits.append(hit)
    es = [jnp.exp(v - vals[0]) for v in vals]
    tot = sum(es)
    onehot = sum(jnp.where(h, 1.0, 0.0) for h in hits)
    before = _dot(ltri_ref[...], onehot.astype(BF16)) + carry_ref[...]
    lane_f = lane.astype(F32)
    ids = jnp.zeros((tm, LANES), F32)
    gates = jnp.zeros((tm, LANES), F32)
    ranks = jnp.zeros((tm, LANES), F32)
    for k in range(TOP_K):
        slot = lane == k
        ids = jnp.where(slot, jnp.sum(jnp.where(hits[k], lane_f, 0.0), axis=1, keepdims=True), ids)
        gates = jnp.where(slot, es[k] / tot, gates)
        ranks = jnp.where(slot, jnp.sum(jnp.where(hits[k], before, 0.0), axis=1, keepdims=True), ranks)
    ids_ref[...] = ids.astype(I32)
    gates_ref[...] = gates
    rank_ref[...] = ranks.astype(I32)
    carry_ref[...] = carry_ref[...] + jnp.sum(onehot, axis=0, keepdims=True)
    cnt_ref[...] = carry_ref[...]


def _router(h1, norm_w, router_w, router_b, tm):
    t, d = h1.shape
    rw = jnp.pad(router_w, ((0, 0), (0, LANES - N_EXPERTS))).astype(BF16)
    rb = jnp.pad(router_b.astype(F32), (0, LANES - N_EXPERTS)).reshape(1, LANES)
    ltri = jnp.tril(jnp.ones((tm, tm), F32), -1).astype(BF16)
    row = lambda i: (i, 0)
    const = lambda i: (0, 0)
    return pl.pallas_call(
        _router_body,
        out_shape=(jax.ShapeDtypeStruct((t, d // 2), I32), jax.ShapeDtypeStruct((t, LANES), I32),
                   jax.ShapeDtypeStruct((t, LANES), F32), jax.ShapeDtypeStruct((t, LANES), I32),
                   jax.ShapeDtypeStruct((1, LANES), F32)),
        grid=(t // tm,),
        in_specs=[pl.BlockSpec((tm, d), row), pl.BlockSpec((1, d), const), pl.BlockSpec((d, LANES), const),
                  pl.BlockSpec((1, LANES), const), pl.BlockSpec((tm, tm), const)],
        out_specs=(pl.BlockSpec((tm, d // 2), row), pl.BlockSpec((tm, LANES), row), pl.BlockSpec((tm, LANES), row),
                   pl.BlockSpec((tm, LANES), row), pl.BlockSpec((1, LANES), const)),
        scratch_shapes=[pltpu.VMEM((1, LANES), F32)],
        compiler_params=_params("arbitrary"),
        name="moe_router",
    )(h1, norm_w.reshape(1, d), rw, rb, ltri)


def _row_copy(src, src_row, dst, dst_row, sem):
    return pltpu.make_async_copy(src.at[pl.ds(src_row, 1)], dst.at[pl.ds(dst_row, 1)], sem)


def _dispatch_body(pos_ref, pad_ref, tn_ref, xs_ref, zeros_ref, sem, zero_sem, *, tb, pad_rows, n_fills):
    base = pl.program_id(0) * tb * TOP_K

    @pl.when(pl.program_id(0) == 0)
    def _():
        zeros_ref[...] = jnp.zeros(zeros_ref.shape, zeros_ref.dtype)
        fill = lambda e: pltpu.make_async_copy(
            zeros_ref, xs_ref.at[pl.ds(pl.multiple_of(pad_ref[e], pad_rows), pad_rows)], zero_sem)
        for e in range(n_fills):
            pl.when(pad_ref[e] >= 0)(lambda e=e: fill(e).start())
        for e in range(n_fills):
            pl.when(pad_ref[e] >= 0)(lambda e=e: fill(e).wait())

    def issue(tok, carry):
        for k in range(TOP_K):
            _row_copy(tn_ref, tok, xs_ref, pos_ref[base + tok * TOP_K + k], sem).start()
        return carry

    def drain(tok, carry):
        for _ in range(TOP_K):
            _row_copy(tn_ref, 0, xs_ref, 0, sem).wait()
        return carry

    lax.fori_loop(0, tb, issue, 0, unroll=4)
    lax.fori_loop(0, tb, drain, 0, unroll=4)


def _dispatch(pos_flat, pad_start, tn, n_rows, tb, pad_rows):
    t, d = tn.shape
    grid_spec = pltpu.PrefetchScalarGridSpec(
        num_scalar_prefetch=2, grid=(t // tb,),
        in_specs=[pl.BlockSpec((tb, d), lambda i, pos, pad: (i, 0))],
        out_specs=pl.BlockSpec(memory_space=pl.ANY),
        scratch_shapes=[pltpu.VMEM((pad_rows, d), tn.dtype), pltpu.SemaphoreType.DMA, pltpu.SemaphoreType.DMA])
    return pl.pallas_call(
        functools.partial(_dispatch_body, tb=tb, pad_rows=pad_rows, n_fills=pad_start.shape[0]),
        out_shape=jax.ShapeDtypeStruct((n_rows, d), tn.dtype),
        grid_spec=grid_spec,
        compiler_params=_params("arbitrary"),
        name="moe_dispatch",
    )(pos_flat, pad_start, tn)


def _swiglu(hg, hl):
    glu = jnp.minimum(hg, SWIGLU_LIMIT)
    lin = jnp.clip(hl, -SWIGLU_LIMIT, SWIGLU_LIMIT)
    return glu * _sigmoid(SWIGLU_ALPHA * glu) * (lin + 1.0)


def _expert_changed(te_ref, i):
    return (i == 0) | (te_ref[i] != te_ref[jnp.maximum(i - 1, 0)])


def _gemm1_body(te_ref, tv_ref, x_ref, wg_ref, wl_ref, bg_ref, bl_ref, h_ref, wgb_ref, wlb_ref):
    i = pl.program_id(1)

    @pl.when(_expert_changed(te_ref, i))
    def _():
        wgb_ref[...] = wg_ref[0].astype(BF16)
        wlb_ref[...] = wl_ref[0].astype(BF16)

    @pl.when(tv_ref[i] == 1)
    def _():
        x_hi, x_lo = _unpack_bf16_pairs(x_ref[...])
        half = x_hi.shape[1]
        hg = _dot(x_hi, wgb_ref[0:half, :]) + _dot(x_lo, wgb_ref[half:, :]) + bg_ref[0]
        hl = _dot(x_hi, wlb_ref[0:half, :]) + _dot(x_lo, wlb_ref[half:, :]) + bl_ref[0]
        h_ref[...] = _swiglu(hg, hl).astype(h_ref.dtype)

    @pl.when(tv_ref[i] == 0)
    def _():
        h_ref[...] = jnp.zeros(h_ref.shape, h_ref.dtype)


def _gemm2_body(te_ref, tv_ref, h_ref, w_ref, b_ref, y_ref, wb_ref):
    i = pl.program_id(1)

    @pl.when(_expert_changed(te_ref, i))
    def _():
        wb_ref[...] = w_ref[0].astype(BF16)

    @pl.when(tv_ref[i] == 1)
    def _():
        y_ref[...] = _dot(h_ref[...], wb_ref[...]) + b_ref[0]

    @pl.when(tv_ref[i] == 0)
    def _():
        y_ref[...] = jnp.zeros(y_ref.shape, y_ref.dtype)


def _expert_gemms(xs, tile_expert, tile_valid, w1, b1, w2, b2, *, tm, tf, tn):
    n_rows = xs.shape[0]
    n_exp, d, f2 = w1.shape
    f = f2 // 2
    tf, tn = min(tf, f), min(tn, d)
    nf = f // tf
    n_tiles = n_rows // tm
    b1r = b1.reshape(n_exp, 1, f2)
    b2r = b2.reshape(n_exp, 1, d)
    grid1 = pltpu.PrefetchScalarGridSpec(
        num_scalar_prefetch=2, grid=(nf, n_tiles),
        in_specs=[pl.BlockSpec((tm, d // 2), lambda j, i, te, tv: (i, 0)),
                  pl.BlockSpec((1, d, tf), lambda j, i, te, tv: (te[i], 0, j)),
                  pl.BlockSpec((1, d, tf), lambda j, i, te, tv: (te[i], 0, nf + j)),
                  pl.BlockSpec((1, 1, tf), lambda j, i, te, tv: (te[i], 0, j)),
                  pl.BlockSpec((1, 1, tf), lambda j, i, te, tv: (te[i], 0, nf + j))],
        out_specs=pl.BlockSpec((tm, tf), lambda j, i, te, tv: (i, j)),
        scratch_shapes=[pltpu.VMEM((d, tf), BF16), pltpu.VMEM((d, tf), BF16)])
    hidden = pl.pallas_call(
        _gemm1_body, out_shape=jax.ShapeDtypeStruct((n_rows, f), BF16), grid_spec=grid1,
        compiler_params=_params("arbitrary", "arbitrary"), name="moe_gemm1",
    )(tile_expert, tile_valid, xs, w1, w1, b1r, b1r)
    grid2 = pltpu.PrefetchScalarGridSpec(
        num_scalar_prefetch=2, grid=(d // tn, n_tiles),
        in_specs=[pl.BlockSpec((tm, f), lambda j, i, te, tv: (i, 0)),
                  pl.BlockSpec((1, f, tn), lambda j, i, te, tv: (te[i], 0, j)),
                  pl.BlockSpec((1, 1, tn), lambda j, i, te, tv: (te[i], 0, j))],
        out_specs=pl.BlockSpec((tm, tn), lambda j, i, te, tv: (i, j)),
        scratch_shapes=[pltpu.VMEM((f, tn), BF16)])
    return pl.pallas_call(
        _gemm2_body, out_shape=jax.ShapeDtypeStruct((n_rows, d), F32), grid_spec=grid2,
        compiler_params=_params("arbitrary", "arbitrary"), name="moe_gemm2",
    )(tile_expert, tile_valid, hidden, w2, b2r)


def _combine_body(pos_ref, y_ref, gates_ref, h_ref, nw_ref, o_ref, buf_ref, sem, *, tb, normalize):
    base = pl.program_id(0) * tb * TOP_K

    def issue(tok, carry):
        for k in range(TOP_K):
            _row_copy(y_ref, pos_ref[base + tok * TOP_K + k], buf_ref.at[k], tok, sem).start()
        return carry

    def drain(tok, carry):
        for _ in range(TOP_K):
            _row_copy(y_ref, 0, buf_ref.at[0], 0, sem).wait()
        return carry

    lax.fori_loop(0, tb, issue, 0, unroll=4)
    lax.fori_loop(0, tb, drain, 0, unroll=4)
    gates = gates_ref[...]
    out = h_ref[...]
    for k in range(TOP_K):
        out = out + gates[:, k:k + 1] * buf_ref[k]
    if normalize:
        ms = jnp.mean(out * out, axis=-1, keepdims=True)
        out = out * lax.rsqrt(ms + EPS) * nw_ref[...]
    o_ref[...] = out


def _combine(pos_flat, y, gates, h1, final_w, tb, normalize):
    t, d = h1.shape
    grid_spec = pltpu.PrefetchScalarGridSpec(
        num_scalar_prefetch=1, grid=(t // tb,),
        in_specs=[pl.BlockSpec(memory_space=pl.ANY),
                  pl.BlockSpec((tb, LANES), lambda i, pos: (i, 0)),
                  pl.BlockSpec((tb, d), lambda i, pos: (i, 0)),
                  pl.BlockSpec((1, d), lambda i, pos: (0, 0))],
        out_specs=pl.BlockSpec((tb, d), lambda i, pos: (i, 0)),
        scratch_shapes=[pltpu.VMEM((TOP_K, tb, d), F32), pltpu.SemaphoreType.DMA])
    return pl.pallas_call(
        functools.partial(_combine_body, tb=tb, normalize=normalize),
        out_shape=jax.ShapeDtypeStruct((t, d), F32),
        grid_spec=grid_spec,
        compiler_params=_params("arbitrary"),
        name="moe_combine",
    )(pos_flat, y, gates, h1, final_w.reshape(1, d))


def _moe(h1, norm_w, router_w, router_b, w1, b1, w2, b2, final_w, normalize, *, tm_router, tm_expert,
         tf, tn, tb_dispatch, tb):
    t, d = h1.shape
    tn_tokens, ids, gates, rank, cnt = _router(h1, norm_w, router_w, router_b, tm_router)
    counts = cnt[0, :N_EXPERTS].astype(I32)
    padded = (counts + tm_expert - 1) // tm_expert * tm_expert
    seg_end = jnp.cumsum(padded)
    seg_start = seg_end - padded
    expert_of = ids[:, :TOP_K, None] == jnp.arange(N_EXPERTS, dtype=I32)
    pos = (jnp.sum(jnp.where(expert_of, seg_start, 0), axis=-1) + rank[:, :TOP_K]).reshape(-1)
    n_rows = t * TOP_K + N_EXPERTS * tm_expert
    n_tiles = n_rows // tm_expert
    tile_ids = jnp.arange(n_tiles, dtype=I32)
    tiles_done = jnp.sum((tile_ids[:, None] >= (seg_end // tm_expert)[None, :]).astype(I32), axis=1)
    tile_expert = jnp.minimum(tiles_done, N_EXPERTS - 1)
    tile_valid = (tile_ids < seg_end[-1] // tm_expert).astype(I32)
    expert_pad = jnp.where(padded > 0, seg_end - tm_expert, -1)
    tail_pad = seg_end[-1] + jnp.arange(N_EXPERTS, dtype=I32) * tm_expert
    tail_pad = jnp.where(tail_pad < n_rows, tail_pad, -1)
    xs = _dispatch(pos, jnp.concatenate([expert_pad, tail_pad]), tn_tokens, n_rows, tb_dispatch, tm_expert)
    y = _expert_gemms(xs, tile_expert, tile_valid, w1, b1, w2, b2, tm=tm_expert, tf=tf, tn=tn)
    return _combine(pos, y, gates, h1, final_w, tb, normalize)


def _layer(x2, batch, seq, w_in, conv_w, conv_b, dt_bias, a_log, d_skip, ssd_norm_w, w_ssd_branch, w_attn_branch,
           w_out, norm_mix_w, norm_ffn_w, router_w, router_b, w1, b1, w2, b2, final_w, normalize):
    t, d = x2.shape
    ssd_inner = dt_bias.shape[0] * SSD_HEAD_DIM
    conv_dim = conv_w.shape[1]
    attn_inner = w_attn_branch.shape[0]
    idx_q = IDX_HEADS * IDX_DIM
    tm = min(ROW_TILE, seq)

    sizes = (ssd_inner, conv_dim, dt_bias.shape[0], attn_inner, attn_inner, attn_inner,
             idx_q, IDX_DIM, IDX_HEADS, d, d)
    offs = [0]
    for s in sizes:
        offs.append(offs[-1] + s)
    col = lambda i: w_in[:, offs[i]:offs[i + 1]]
    w_z, w_xbc, w_dt, w_q, w_k, w_v, w_qi, w_ki, w_wi, w_gs, w_ga = (col(i) for i in range(len(sizes)))
    zeros_k = jnp.zeros_like(w_ki)
    w_ki2 = jnp.concatenate([w_ki, zeros_k, zeros_k, w_ki], axis=1)
    n_misc = w_dt.shape[1] + w_wi.shape[1]
    w_misc = jnp.pad(jnp.concatenate([w_dt, w_wi], axis=1), ((0, 0), (0, LANES - n_misc)))
    b16 = lambda w: w.astype(BF16)

    xn = _rmsnorm(x2, norm_mix_w, BF16, tm)
    mm = functools.partial(_matmul, tm=tm)
    z = mm(xn, b16(w_z), F32, "proj_z")
    xbc = mm(xn, b16(w_xbc), F32, "proj_xbc")
    gate_logits = mm(xn, b16(jnp.concatenate([w_gs, w_ga], axis=1)), F32, "proj_gates")
    misc = mm(xn, b16(w_misc), F32, "proj_misc")
    v = mm(xn, b16(w_v), BF16, "proj_v")
    attn_tabs = _rope_tables(seq, ATTN_HEAD_DIM // 4, ATTN_HEAD_DIM)
    idx_tabs = _rope_tables(seq, IDX_DIM // 4, IDX_DIM)
    qk = _rope_matmul(xn, b16(jnp.concatenate([w_q, w_k], axis=1)), attn_tabs, ATTN_HEAD_DIM // 4, seq,
                      "proj_qk", tm=tm)
    qi = _rope_matmul(xn, b16(w_qi), idx_tabs, IDX_DIM // 4, seq, "proj_qi", tm=tm)
    ki = _rope_matmul(xn, b16(w_ki2), idx_tabs, IDX_DIM // 4, seq, "proj_ki", tm=tm)

    y_ssd = _ssd(xbc, z, misc, conv_w, conv_b, dt_bias, a_log, d_skip, ssd_norm_w, batch, seq)
    mask = _indexer_mask(qi, ki, misc, batch, seq, w_lane=w_dt.shape[1])
    y_attn = _attention(qk, v, mask, batch, seq, tq=ATTN_Q_TILE, tk=ATTN_K_TILE)

    g_spec = lambda off: lambda tm_, tn_: pl.BlockSpec((tm_, tn_), lambda j, i: (i, off * (d // tn_) + j))
    same = lambda tm_, tn_: pl.BlockSpec((tm_, tn_), lambda j, i: (i, j))
    part = mm(y_ssd, b16(w_ssd_branch), F32, "merge_ssd", epilogue=_ep_gate, extras=(gate_logits,),
              extra_specs=(g_spec(0),))
    merged = mm(y_attn, b16(w_attn_branch), BF16, "merge_attn", epilogue=_ep_gate_add,
                extras=(gate_logits, part), extra_specs=(g_spec(1), same))
    h1 = mm(merged, b16(w_out), F32, "out_proj", epilogue=_ep_residual, extras=(x2,), extra_specs=(same,))

    return _moe(h1, norm_ffn_w, router_w, router_b, w1, b1, w2, b2, final_w, normalize,
                tm_router=min(ROUTER_ROW_TILE, t), tm_expert=EXPERT_ROW_TILE, tf=EXPERT_HIDDEN_TILE,
                tn=EXPERT_OUT_TILE, tb_dispatch=min(DISPATCH_TOKENS, t), tb=min(COMBINE_TOKENS, t))


def kernel(x, w_in, conv_w, conv_b, dt_bias, a_log, d_skip, ssd_norm_w, w_ssd_branch, w_attn_branch, w_out,
           norm_mix_w, norm_ffn_w, router_w, router_b, w_expert_in, b_expert_in, w_expert_out, b_expert_out,
           final_norm_w):
    batch, seq, d = x.shape
    depth = w_in.shape[0]
    per_layer = (w_in, conv_w, conv_b, dt_bias, a_log, d_skip, ssd_norm_w, w_ssd_branch, w_attn_branch, w_out,
                 norm_mix_w, norm_ffn_w, router_w, router_b, w_expert_in, b_expert_in, w_expert_out, b_expert_out)
    h = x.reshape(batch * seq, d)
    for layer in range(depth):
        h = _layer(h, batch, seq, *(p[layer] for p in per_layer), final_norm_w, layer == depth - 1)
    return h.reshape(batch, seq, d)
```

```python
import functools
import math

import jax
import jax.numpy as jnp
from jax import lax
from jax.experimental import pallas as pl
from jax.experimental.pallas import tpu as pltpu

EPS = 1e-5
SSD_HEADS = 32
SSD_HEAD_DIM = 64
SSD_GROUPS = 4
SSD_STATE = 128
SSD_CONV = 4
SSD_CHUNK = 128
ATTN_HEADS = 16
ATTN_HEAD_DIM = 128
ROPE_THETA = 500000.0
IDX_HEADS = 8
IDX_DIM = 64
IDX_TOPK = 256
Q_BLOCK = 128
N_EXPERTS = 32
TOP_K = 4
D_EXPERT = 2048
SWIGLU_ALPHA = 1.702
SWIGLU_LIMIT = 7.0

LANES = 128
SUBLANES = 8
VMEM_LIMIT_BYTES = 56 * 2**20
INT32_MIN = -2**31
MASKED_LOGIT = -1e30

ROW_TILE = 512
ATTN_Q_TILE = 256
ATTN_K_TILE = 1024
ATTN_HEAD_GROUP = 4
INDEXER_KEY_CHUNK = 512
INDEXER_ROW_GROUPS = 4
ROUTER_ROW_TILE = 256
EXPERT_ROW_TILE = 512
EXPERT_HIDDEN_TILE = 1024
EXPERT_OUT_TILE = 2048
DISPATCH_TOKENS = 512
COMBINE_TOKENS = 128

F32 = jnp.float32
BF16 = jnp.bfloat16
I32 = jnp.int32


def _params(*sem):
    return pltpu.CompilerParams(dimension_semantics=sem, vmem_limit_bytes=VMEM_LIMIT_BYTES)


def _sigmoid(x):
    return 1.0 / (1.0 + jnp.exp(-x))


def _dot(a, b):
    return jnp.dot(a, b, preferred_element_type=F32)


def _dot_nt(a, b):
    return lax.dot_general(a, b, (((1,), (1,)), ((), ())), preferred_element_type=F32)


def _split3(x):
    hi = x.astype(BF16)
    r1 = x - hi.astype(F32)
    mid = r1.astype(BF16)
    lo = (r1 - mid.astype(F32)).astype(BF16)
    return hi, mid, lo


def _rmsnorm_body(x_ref, w_ref, o_ref):
    x = x_ref[...]
    ms = jnp.mean(x * x, axis=-1, keepdims=True)
    o_ref[...] = (x * lax.rsqrt(ms + EPS) * w_ref[...]).astype(o_ref.dtype)


def _rmsnorm(x, w, out_dtype, tm):
    t, d = x.shape
    return pl.pallas_call(
        _rmsnorm_body,
        out_shape=jax.ShapeDtypeStruct((t, d), out_dtype),
        grid=(t // tm,),
        in_specs=[pl.BlockSpec((tm, d), lambda i: (i, 0)), pl.BlockSpec((1, d), lambda i: (0, 0))],
        out_specs=pl.BlockSpec((tm, d), lambda i: (i, 0)),
        compiler_params=_params("parallel"),
        name="rmsnorm",
    )(x, w.reshape(1, d))


def _ep_identity(acc):
    return acc


def _ep_rope(acc, c, s1, s2, *, shift):
    outs = []
    for g in range(acc.shape[1] // LANES):
        o = acc[:, g * LANES:(g + 1) * LANES]
        outs.append(o * c + pltpu.roll(o, LANES - shift, 1) * s1 + pltpu.roll(o, shift, 1) * s2)
    return outs[0] if len(outs) == 1 else jnp.concatenate(outs, axis=1)


def _ep_gate(acc, g):
    return _sigmoid(g) * acc


def _ep_gate_add(acc, g, p):
    return p + _sigmoid(g) * acc


def _ep_residual(acc, x):
    return x + acc


def _mm_body(*refs, epilogue):
    a_ref, w_ref = refs[0], refs[1]
    o_ref = refs[-1]
    acc = _dot(a_ref[...], w_ref[...])
    o_ref[...] = epilogue(acc, *[r[...] for r in refs[2:-1]]).astype(o_ref.dtype)


MAX_COL_TILE = 2048


def _col_tile(n):
    return max(t for t in range(LANES, min(n, MAX_COL_TILE) + 1, LANES) if n % t == 0)


def _matmul(a, w, out_dtype, name, *, tm, epilogue=_ep_identity, extras=(), extra_specs=()):
    m, k = a.shape
    n = w.shape[1]
    tn = _col_tile(n)
    return pl.pallas_call(
        functools.partial(_mm_body, epilogue=epilogue),
        out_shape=jax.ShapeDtypeStruct((m, n), out_dtype),
        grid=(n // tn, m // tm),
        in_specs=[pl.BlockSpec((tm, k), lambda j, i: (i, 0)),
                  pl.BlockSpec((k, tn), lambda j, i: (0, j)), *[f(tm, tn) for f in extra_specs]],
        out_specs=pl.BlockSpec((tm, tn), lambda j, i: (i, j)),
        compiler_params=_params("parallel", "parallel"),
        name=name,
    )(a, w, *extras)


def _rope_tables(length, rot_dim, head_dim):
    half = rot_dim // 2
    inv = ROPE_THETA ** (-jnp.arange(0, rot_dim, 2, dtype=F32) / rot_dim)
    ang = jnp.arange(length, dtype=F32)[:, None] * inv[None, :]
    cos, sin = jnp.cos(ang), jnp.sin(ang)
    zeros = lambda n: jnp.zeros((length, n), F32)
    c = jnp.concatenate([cos, cos, jnp.ones((length, head_dim - rot_dim), F32)], axis=1)
    s1 = jnp.concatenate([-sin, zeros(head_dim - half)], axis=1)
    s2 = jnp.concatenate([zeros(half), sin, zeros(head_dim - rot_dim)], axis=1)
    reps = LANES // head_dim
    return tuple(jnp.tile(t, (1, reps)) for t in (c, s1, s2))


def _rope_matmul(a, w, tables, rot_dim, seq, name, *, tm):
    nblk = seq // tm
    spec = lambda tm_, tn_: pl.BlockSpec((tm_, LANES), lambda j, i: (i % nblk, 0))
    return _matmul(a, w, BF16, name, tm=tm,
                   epilogue=functools.partial(_ep_rope, shift=rot_dim // 2),
                   extras=tables, extra_specs=(spec, spec, spec))


def _ssd_body(xbc_ref, z_ref, misc_ref, cw_ref, cb_ref, dtb_ref, alog_ref, dsk_ref, nw_ref,
              ltri_ref, exp_ref, o_ref, tail_ref, state_ref, *, inner, groups, heads):
    ch = SSD_CHUNK
    n = SSD_STATE
    gw = inner // groups
    c_idx = pl.program_id(1)
    tail = SSD_CONV - 1

    @pl.when(c_idx == 0)
    def _():
        tail_ref[...] = jnp.zeros(tail_ref.shape, F32)
        state_ref[...] = jnp.zeros(state_ref.shape, F32)

    cur = xbc_ref[...]
    prev_rows = tail_ref[...]
    row8 = lax.broadcasted_iota(I32, (SUBLANES, cur.shape[1]), 0)
    conv = cb_ref[...] + cur * cw_ref[tail:tail + 1, :]
    for j in range(tail):
        shift = tail - j
        rolled = pltpu.roll(cur, shift, 0)
        head = jnp.where(row8 < shift, pltpu.roll(prev_rows, shift, 0), rolled[0:SUBLANES, :])
        conv = conv + jnp.concatenate([head, rolled[SUBLANES:, :]], axis=0) * cw_ref[j:j + 1, :]
    tail_ref[...] = cur[ch - SUBLANES:, :]
    xbc = conv * _sigmoid(conv)
    xs = xbc[:, :inner]
    bm = xbc[:, inner:inner + groups * n]
    cm = xbc[:, inner + groups * n:]

    lane = lax.broadcasted_iota(I32, (1, LANES), 1)
    head_lane = lane < heads
    a = jnp.where(head_lane, -jnp.exp(alog_ref[...]), 0.0)
    dtr = misc_ref[...] + dtb_ref[...]
    dt = jnp.maximum(dtr, 0.0) + jnp.log1p(jnp.exp(-jnp.abs(dtr)))
    da = dt * a

    ltri = ltri_ref[...]
    cs = sum(_dot(ltri, p) for p in _split3(da))
    expand = exp_ref[...]
    dt_e = sum(_dot(p, expand) for p in _split3(dt))
    cs_e = sum(_dot(p, expand) for p in _split3(cs))
    cs_last = cs_e[ch - 1:ch, :]
    ecs = jnp.exp(cs_e)
    dte = jnp.exp(cs_last - cs_e)
    chunk_decay = jnp.exp(cs_last)

    xdt = xs * dt_e
    xdt_b = xdt.astype(BF16)
    xw_b = (xdt * dte).astype(BF16)
    cs_t = cs.T

    row = lax.broadcasted_iota(I32, (ch, ch), 0)
    col = lax.broadcasted_iota(I32, (ch, ch), 1)
    causal = row >= col
    first_half = lax.broadcasted_iota(I32, (ch, LANES), 1) < SSD_HEAD_DIM
    heads_per_group = heads // groups

    diag_cols, off_cols = [], []
    for g in range(groups):
        bg = bm[:, g * n:(g + 1) * n]
        cg_b = cm[:, g * n:(g + 1) * n].astype(BF16)
        cb = _dot_nt(cg_b, bg.astype(BF16))
        bg_t = bg.T.astype(BF16)
        for pr in range(heads_per_group // 2):
            h0 = g * heads_per_group + 2 * pr
            xp = xdt_b[:, h0 * SSD_HEAD_DIM:(h0 + 2) * SSD_HEAD_DIM]
            res = []
            for h in (h0, h0 + 1):
                seg = cs[:, h:h + 1] - cs_t[h:h + 1, :]
                decay = jnp.exp(jnp.where(causal, seg, -jnp.inf))
                res.append(_dot((cb * decay).astype(BF16), xp))
            diag_cols.append(jnp.where(first_half, res[0], res[1]))
        prev = state_ref[g]
        off_cols.append(_dot(cg_b, prev.astype(BF16)) * ecs[:, g * gw:(g + 1) * gw])
        states = _dot(bg_t, xw_b[:, g * gw:(g + 1) * gw])
        state_ref[g] = prev * chunk_decay[:, g * gw:(g + 1) * gw] + states

    y = jnp.concatenate(diag_cols, axis=1) + jnp.concatenate(off_cols, axis=1) + xs * dsk_ref[...]

    z = z_ref[...]
    gated = y * (z * _sigmoid(z))
    ms = jnp.mean(gated * gated, axis=-1, keepdims=True)
    o_ref[...] = (gated * lax.rsqrt(ms + EPS) * nw_ref[...]).astype(o_ref.dtype)


def _ssd(xbc, z, misc, conv_w, conv_b, dt_bias, a_log, d_skip, norm_w, batch, seq):
    t, conv_dim = xbc.shape
    inner = z.shape[1]
    heads = dt_bias.shape[0]
    groups = SSD_GROUPS
    ch = SSD_CHUNK
    nc = seq // ch
    pad = lambda v: jnp.pad(v.astype(F32), (0, LANES - heads)).reshape(1, LANES)
    ltri = jnp.tril(jnp.ones((ch, ch), F32)).astype(BF16)
    head_of_lane = jnp.arange(inner) // SSD_HEAD_DIM
    expand = (jnp.arange(LANES)[:, None] == head_of_lane[None, :]).astype(BF16)
    d_exp = jnp.repeat(d_skip.astype(F32), SSD_HEAD_DIM).reshape(1, inner)
    row = lambda b, c: (b * nc + c, 0)
    const = lambda b, c: (0, 0)
    return pl.pallas_call(
        functools.partial(_ssd_body, inner=inner, groups=groups, heads=heads),
        out_shape=jax.ShapeDtypeStruct((t, inner), BF16),
        grid=(batch, nc),
        in_specs=[pl.BlockSpec((ch, conv_dim), row), pl.BlockSpec((ch, inner), row),
                  pl.BlockSpec((ch, LANES), row),
                  pl.BlockSpec((SSD_CONV, conv_dim), const), pl.BlockSpec((1, conv_dim), const),
                  pl.BlockSpec((1, LANES), const), pl.BlockSpec((1, LANES), const),
                  pl.BlockSpec((1, inner), const), pl.BlockSpec((1, inner), const),
                  pl.BlockSpec((ch, ch), const), pl.BlockSpec((LANES, inner), const)],
        out_specs=pl.BlockSpec((ch, inner), row),
        scratch_shapes=[pltpu.VMEM((SUBLANES, conv_dim), F32),
                        pltpu.VMEM((groups, SSD_STATE, inner // groups), F32)],
        compiler_params=_params("arbitrary", "arbitrary"),
        name="ssd_scan",
    )(xbc, z, misc, conv_w, conv_b.reshape(1, conv_dim), pad(dt_bias), pad(a_log), d_exp,
      norm_w.reshape(1, inner), ltri, expand)


def _indexer_body(qi_ref, ki_ref, misc_ref, u_ref, ones_ref, mask_ref, keys_ref, *,
                  seq, kc, rows, group_rows, topk, w_lane, idx_scale):
    qb = pl.program_id(1)
    q0 = qb * rows
    n_chunks = (q0 + rows + kc - 1) // kc
    n_unmasked = (q0 + 1) // kc
    n_lane_chunks = kc // LANES
    lane_chunk = lambda a, j: a[:, j * LANES:(j + 1) * LANES]
    w = misc_ref[...] * idx_scale
    qi = qi_ref[...]
    ones = ones_ref[...]

    def score_chunk(c, carry, *, masked):
        off = pl.multiple_of(c * kc, kc)
        k_lo = ki_ref[pl.ds(off, kc), 0:LANES]
        k_hi = ki_ref[pl.ds(off, kc), LANES:2 * LANES]
        s = jnp.zeros((rows, kc), F32)
        for j in range(IDX_HEADS // 2):
            qj = lane_chunk(qi, j)
            w0 = w[:, w_lane + 2 * j:w_lane + 2 * j + 1]
            w1 = w[:, w_lane + 2 * j + 1:w_lane + 2 * j + 2]
            s = s + w0 * jnp.maximum(_dot_nt(qj, k_lo), 0.0) + w1 * jnp.maximum(_dot_nt(qj, k_hi), 0.0)
        bits = lax.bitcast_convert_type(s, I32)
        key = jnp.where(bits >= 0, bits, bits ^ jnp.int32(0x7FFFFFFF))
        if masked:
            qpos = q0 + lax.broadcasted_iota(I32, (rows, kc), 0)
            kpos = off + lax.broadcasted_iota(I32, (rows, kc), 1)
            key = jnp.where(kpos <= qpos, key, jnp.int32(INT32_MIN))
        keys_ref[:, pl.ds(off, kc)] = key
        return carry

    lax.fori_loop(0, n_unmasked, functools.partial(score_chunk, masked=False), 0)
    lax.fori_loop(n_unmasked, n_chunks, functools.partial(score_chunk, masked=True), 0)

    groups = [slice(g * group_rows, (g + 1) * group_rows) for g in range(rows // group_rows)]

    def count(compare, thresh):
        accs = []
        for g, rows_g in enumerate(groups):
            thresh_g = thresh[rows_g]

            def body(c, acc, rows_g=rows_g, thresh_g=thresh_g):
                k = keys_ref[rows_g, pl.ds(pl.multiple_of(c * kc, kc), kc)]
                for j in range(n_lane_chunks):
                    acc = acc + jnp.where(compare(lane_chunk(k, j), thresh_g), 1, 0)
                return acc

            chunks_g = (q0 + (g + 1) * group_rows + kc - 1) // kc
            accs.append(lax.fori_loop(0, chunks_g, body, jnp.zeros((group_rows, LANES), I32)))
        acc = jnp.concatenate(accs, axis=0)
        return _dot(acc.astype(F32).astype(BF16), ones)

    def bit_body(i, state):
        t_u, n_ge = state
        cand_u = t_u | jnp.left_shift(jnp.int32(1), 31 - i)
        cnt = count(lambda k, c: k >= c, cand_u ^ jnp.int32(INT32_MIN))
        keep = cnt >= topk
        return jnp.where(keep, cand_u, t_u), jnp.where(keep, cnt, n_ge)

    n_admissible = (q0 + lax.broadcasted_iota(I32, (rows, LANES), 0) + 1).astype(F32)
    t_u, n_ge = lax.fori_loop(0, 32, bit_body, (jnp.zeros((rows, LANES), I32), n_admissible))
    t_s = t_u ^ jnp.int32(INT32_MIN)
    n_gt = count(lambda k, c: k > c, t_s)
    select_all = t_u == 0
    n_tie = jnp.where(select_all, 0.0, topk - n_gt)
    excess_ties = jnp.max(jnp.where(select_all, 0.0, n_ge - topk)) > 0.0

    @pl.when(jnp.logical_not(excess_ties))
    def _():
        def mask_chunk(c, carry):
            sl = pl.ds(pl.multiple_of(c * kc, kc), kc)
            k = keys_ref[:, sl]
            sel = [(lane_chunk(k, j) > t_s) | ((lane_chunk(k, j) == t_s) & jnp.logical_not(select_all))
                   for j in range(n_lane_chunks)]
            mask_ref[:, sl] = jnp.where(jnp.concatenate(sel, axis=1), 1, 0).astype(jnp.int8)
            return carry
        lax.fori_loop(0, n_chunks, mask_chunk, 0)

    @pl.when(excess_ties)
    def _():
        t_col = t_s[:, 0:1]
        n_tie_col = n_tie[:, 0:1]

        def mask_chunk(c, seen):
            sl = pl.ds(pl.multiple_of(c * kc, kc), kc)
            k = keys_ref[:, sl]
            tie = k == t_col
            tie_f = jnp.where(tie, 1.0, 0.0)
            rank = seen + _dot(tie_f.astype(BF16), u_ref[...])
            sel = (k > t_col) | (tie & (rank <= n_tie_col))
            mask_ref[:, sl] = jnp.where(sel, 1, 0).astype(jnp.int8)
            return seen + jnp.sum(tie_f, axis=1, keepdims=True)
        lax.fori_loop(0, n_chunks, mask_chunk, jnp.zeros((rows, 1), F32))

    def zero_chunk(c, carry):
        mask_ref[:, pl.ds(pl.multiple_of(c * kc, kc), kc)] = jnp.zeros((rows, kc), jnp.int8)
        return carry

    lax.fori_loop(n_chunks, seq // kc, zero_chunk, 0)


def _indexer_mask(qi, ki, misc, batch, seq, w_lane):
    t = qi.shape[0]
    group_rows = min(Q_BLOCK, seq)
    rows = min(INDEXER_ROW_GROUPS * group_rows, seq)
    nq = seq // rows
    kc = min(INDEXER_KEY_CHUNK, seq)
    topk = min(IDX_TOPK, seq // 4)
    assert seq // LANES <= 256, "per-lane counts must stay exact in bf16"
    upper = jnp.triu(jnp.ones((kc, kc), F32)).astype(BF16)
    ones = jnp.ones((LANES, LANES), BF16)
    idx_scale = (IDX_DIM ** -0.5) * (IDX_HEADS ** -0.5)
    return pl.pallas_call(
        functools.partial(_indexer_body, seq=seq, kc=kc, rows=rows, group_rows=group_rows, topk=topk, w_lane=w_lane,
                          idx_scale=idx_scale),
        out_shape=jax.ShapeDtypeStruct((t, seq), jnp.int8),
        grid=(batch, nq),
        in_specs=[pl.BlockSpec((rows, qi.shape[1]), lambda b, q: (b * nq + q, 0)),
                  pl.BlockSpec((seq, ki.shape[1]), lambda b, q: (b, 0)),
                  pl.BlockSpec((rows, LANES), lambda b, q: (b * nq + q, 0)),
                  pl.BlockSpec((kc, kc), lambda b, q: (0, 0)),
                  pl.BlockSpec((LANES, LANES), lambda b, q: (0, 0))],
        out_specs=pl.BlockSpec((rows, seq), lambda b, q: (b * nq + q, 0)),
        scratch_shapes=[pltpu.VMEM((rows, seq), I32)],
        compiler_params=_params("parallel", "arbitrary"),
        name="indexer_topk_mask",
    )(qi, ki, misc, upper, ones)


def _attn_body(q_ref, k_ref, v_ref, mask_ref, o_ref, acc_ref, m_ref, l_ref, *, tq, tk, heads, group, scale_log2e):
    qb = pl.program_id(1)
    kb = pl.program_id(2)
    last = ((qb + 1) * tq - 1) // tk
    hd = ATTN_HEAD_DIM

    @pl.when(kb == 0)
    def _():
        acc_ref[...] = jnp.zeros(acc_ref.shape, F32)
        m_ref[...] = jnp.full(m_ref.shape, MASKED_LOGIT, F32)
        l_ref[...] = jnp.zeros(l_ref.shape, F32)

    @pl.when(kb <= last)
    def _():
        bias = jnp.where(mask_ref[...].astype(I32) != 0, 0.0, MASKED_LOGIT)
        for h0 in range(0, heads, group):
            hs = range(h0, min(h0 + group, heads))
            col = lambda h: slice(h * hd, (h + 1) * hd)
            s = [_dot_nt(q_ref[:, col(h)], k_ref[:, col(h)]) * scale_log2e + bias for h in hs]
            m_new = [jnp.maximum(m_ref[h], jnp.max(sh, axis=1, keepdims=True)) for h, sh in zip(hs, s)]
            p = [jnp.exp2(sh - mh[:, 0:1]) for sh, mh in zip(s, m_new)]
            for h, ph, mh in zip(hs, p, m_new):
                alpha = jnp.exp2(m_ref[h] - mh)
                l_ref[h] = alpha * l_ref[h] + jnp.sum(ph, axis=1, keepdims=True)
                acc_ref[:, col(h)] = acc_ref[:, col(h)] * alpha + _dot(ph.astype(BF16), v_ref[:, col(h)])
                m_ref[h] = mh

    @pl.when(kb == pl.num_programs(2) - 1)
    def _():
        for h in range(heads):
            cols = slice(h * hd, (h + 1) * hd)
            o_ref[:, cols] = (acc_ref[:, cols] / l_ref[h]).astype(o_ref.dtype)


def _attention(qk, v, mask, batch, seq, *, tq, tk, group=ATTN_HEAD_GROUP):
    t, inner = v.shape
    heads = inner // ATTN_HEAD_DIM
    tq, tk = min(tq, seq), min(tk, seq)
    nq, nk = seq // tq, seq // tk
    last = lambda q: ((q + 1) * tq - 1) // tk
    return pl.pallas_call(
        functools.partial(_attn_body, tq=tq, tk=tk, heads=heads, group=group,
                          scale_log2e=ATTN_HEAD_DIM ** -0.5 * math.log2(math.e)),
        out_shape=jax.ShapeDtypeStruct((t, inner), BF16),
        grid=(batch, nq, nk),
        in_specs=[pl.BlockSpec((tq, inner), lambda b, q, k: (b * nq + q, 0)),
                  pl.BlockSpec((tk, inner), lambda b, q, k: (b * nk + jnp.minimum(k, last(q)), 1)),
                  pl.BlockSpec((tk, inner), lambda b, q, k: (b * nk + jnp.minimum(k, last(q)), 0)),
                  pl.BlockSpec((tq, tk), lambda b, q, k: (b * nq + q, jnp.minimum(k, last(q))))],
        out_specs=pl.BlockSpec((tq, inner), lambda b, q, k: (b * nq + q, 0)),
        scratch_shapes=[pltpu.VMEM((tq, inner), F32),
                        pltpu.VMEM((heads, tq, LANES), F32),
                        pltpu.VMEM((heads, tq, LANES), F32)],
        compiler_params=_params("parallel", "parallel", "arbitrary"),
        name="masked_attention",
    )(qk, qk, v, mask)


def _pack_bf16_pairs(x):
    h = x.shape[1] // 2
    hi = lax.bitcast_convert_type(x[:, :h].astype(F32), jnp.uint32)
    lo = lax.bitcast_convert_type(x[:, h:].astype(F32), jnp.uint32)
    return lax.bitcast_convert_type(hi | (lo >> 16), I32)


def _unpack_bf16_pairs(p):
    u = lax.bitcast_convert_type(p, jnp.uint32)
    hi = lax.bitcast_convert_type(u & jnp.uint32(0xFFFF0000), F32).astype(BF16)
    lo = lax.bitcast_convert_type(u << 16, F32).astype(BF16)
    return hi, lo


def _router_body(h_ref, nw_ref, rw_ref, rb_ref, ltri_ref, tn_ref, ids_ref, gates_ref, rank_ref, cnt_ref,
                 carry_ref):
    @pl.when(pl.program_id(0) == 0)
    def _():
        carry_ref[...] = jnp.zeros(carry_ref.shape, F32)

    x = h_ref[...]
    ms = jnp.mean(x * x, axis=-1, keepdims=True)
    tn = (x * lax.rsqrt(ms + EPS) * nw_ref[...]).astype(BF16)
    tn_ref[...] = _pack_bf16_pairs(tn)
    tm = x.shape[0]
    lane = lax.broadcasted_iota(I32, (tm, LANES), 1)
    logits = _dot(tn, rw_ref[...]) + rb_ref[...]
    work = jnp.where(lane < N_EXPERTS, logits, -jnp.inf)
    vals, hits = [], []
    for _ in range(TOP_K):
        mx = jnp.max(work, axis=1, keepdims=True)
        idx = jnp.min(jnp.where(work == mx, lane, LANES), axis=1, keepdims=True)
        hit = lane == idx
        work = jnp.where(hit, -jnp.inf, work)
        vals.append(mx)
        hits.append(hit)
    es = [jnp.exp(v - vals[0]) for v in vals]
    tot = sum(es)
    onehot = sum(jnp.where(h, 1.0, 0.0) for h in hits)
    before = _dot(ltri_ref[...], onehot.astype(BF16)) + carry_ref[...]
    lane_f = lane.astype(F32)
    ids = jnp.zeros((tm, LANES), F32)
    gates = jnp.zeros((tm, LANES), F32)
    ranks = jnp.zeros((tm, LANES), F32)
    for k in range(TOP_K):
        slot = lane == k
        ids = jnp.where(slot, jnp.sum(jnp.where(hits[k], lane_f, 0.0), axis=1, keepdims=True), ids)
        gates = jnp.where(slot, es[k] / tot, gates)
        ranks = jnp.where(slot, jnp.sum(jnp.where(hits[k], before, 0.0), axis=1, keepdims=True), ranks)
    ids_ref[...] = ids.astype(I32)
    gates_ref[...] = gates
    rank_ref[...] = ranks.astype(I32)
    carry_ref[...] = carry_ref[...] + jnp.sum(onehot, axis=0, keepdims=True)
    cnt_ref[...] = carry_ref[...]


def _router(h1, norm_w, router_w, router_b, tm):
    t, d = h1.shape
    rw = jnp.pad(router_w, ((0, 0), (0, LANES - N_EXPERTS))).astype(BF16)
    rb = jnp.pad(router_b.astype(F32), (0, LANES - N_EXPERTS)).reshape(1, LANES)
    ltri = jnp.tril(jnp.ones((tm, tm), F32), -1).astype(BF16)
    row = lambda i: (i, 0)
    const = lambda i: (0, 0)
    return pl.pallas_call(
        _router_body,
        out_shape=(jax.ShapeDtypeStruct((t, d // 2), I32), jax.ShapeDtypeStruct((t, LANES), I32),
                   jax.ShapeDtypeStruct((t, LANES), F32), jax.ShapeDtypeStruct((t, LANES), I32),
                   jax.ShapeDtypeStruct((1, LANES), F32)),
        grid=(t // tm,),
        in_specs=[pl.BlockSpec((tm, d), row), pl.BlockSpec((1, d), const), pl.BlockSpec((d, LANES), const),
                  pl.BlockSpec((1, LANES), const), pl.BlockSpec((tm, tm), const)],
        out_specs=(pl.BlockSpec((tm, d // 2), row), pl.BlockSpec((tm, LANES), row), pl.BlockSpec((tm, LANES), row),
                   pl.BlockSpec((tm, LANES), row), pl.BlockSpec((1, LANES), const)),
        scratch_shapes=[pltpu.VMEM((1, LANES), F32)],
        compiler_params=_params("arbitrary"),
        name="moe_router",
    )(h1, norm_w.reshape(1, d), rw, rb, ltri)


def _row_copy(src, src_row, dst, dst_row, sem):
    return pltpu.make_async_copy(src.at[pl.ds(src_row, 1)], dst.at[pl.ds(dst_row, 1)], sem)


def _dispatch_body(pos_ref, pad_ref, tn_ref, xs_ref, zeros_ref, sem, zero_sem, *, tb, pad_rows, n_fills):
    base = pl.program_id(0) * tb * TOP_K

    @pl.when(pl.program_id(0) == 0)
    def _():
        zeros_ref[...] = jnp.zeros(zeros_ref.shape, zeros_ref.dtype)
        fill = lambda e: pltpu.make_async_copy(
            zeros_ref, xs_ref.at[pl.ds(pl.multiple_of(pad_ref[e], pad_rows), pad_rows)], zero_sem)
        for e in range(n_fills):
            pl.when(pad_ref[e] >= 0)(lambda e=e: fill(e).start())
        for e in range(n_fills):
            pl.when(pad_ref[e] >= 0)(lambda e=e: fill(e).wait())

    def issue(tok, carry):
        for k in range(TOP_K):
            _row_copy(tn_ref, tok, xs_ref, pos_ref[base + tok * TOP_K + k], sem).start()
        return carry

    def drain(tok, carry):
        for _ in range(TOP_K):
            _row_copy(tn_ref, 0, xs_ref, 0, sem).wait()
        return carry

    lax.fori_loop(0, tb, issue, 0, unroll=4)
    lax.fori_loop(0, tb, drain, 0, unroll=4)


def _dispatch(pos_flat, pad_start, tn, n_rows, tb, pad_rows):
    t, d = tn.shape
    grid_spec = pltpu.PrefetchScalarGridSpec(
        num_scalar_prefetch=2, grid=(t // tb,),
        in_specs=[pl.BlockSpec((tb, d), lambda i, pos, pad: (i, 0))],
        out_specs=pl.BlockSpec(memory_space=pl.ANY),
        scratch_shapes=[pltpu.VMEM((pad_rows, d), tn.dtype), pltpu.SemaphoreType.DMA, pltpu.SemaphoreType.DMA])
    return pl.pallas_call(
        functools.partial(_dispatch_body, tb=tb, pad_rows=pad_rows, n_fills=pad_start.shape[0]),
        out_shape=jax.ShapeDtypeStruct((n_rows, d), tn.dtype),
        grid_spec=grid_spec,
        compiler_params=_params("arbitrary"),
        name="moe_dispatch",
    )(pos_flat, pad_start, tn)


def _swiglu(hg, hl):
    glu = jnp.minimum(hg, SWIGLU_LIMIT)
    lin = jnp.clip(hl, -SWIGLU_LIMIT, SWIGLU_LIMIT)
    return glu * _sigmoid(SWIGLU_ALPHA * glu) * (lin + 1.0)


def _expert_changed(te_ref, i):
    return (i == 0) | (te_ref[i] != te_ref[jnp.maximum(i - 1, 0)])


def _gemm1_body(te_ref, tv_ref, x_ref, wg_ref, wl_ref, bg_ref, bl_ref, h_ref, wgb_ref, wlb_ref):
    i = pl.program_id(1)

    @pl.when(_expert_changed(te_ref, i))
    def _():
        wgb_ref[...] = wg_ref[0].astype(BF16)
        wlb_ref[...] = wl_ref[0].astype(BF16)

    @pl.when(tv_ref[i] == 1)
    def _():
        x_hi, x_lo = _unpack_bf16_pairs(x_ref[...])
        half = x_hi.shape[1]
        hg = _dot(x_hi, wgb_ref[0:half, :]) + _dot(x_lo, wgb_ref[half:, :]) + bg_ref[0]
        hl = _dot(x_hi, wlb_ref[0:half, :]) + _dot(x_lo, wlb_ref[half:, :]) + bl_ref[0]
        h_ref[...] = _swiglu(hg, hl).astype(h_ref.dtype)

    @pl.when(tv_ref[i] == 0)
    def _():
        h_ref[...] = jnp.zeros(h_ref.shape, h_ref.dtype)


def _gemm2_body(te_ref, tv_ref, h_ref, w_ref, b_ref, y_ref, wb_ref):
    i = pl.program_id(1)

    @pl.when(_expert_changed(te_ref, i))
    def _():
        wb_ref[...] = w_ref[0].astype(BF16)

    @pl.when(tv_ref[i] == 1)
    def _():
        y_ref[...] = _dot(h_ref[...], wb_ref[...]) + b_ref[0]

    @pl.when(tv_ref[i] == 0)
    def _():
        y_ref[...] = jnp.zeros(y_ref.shape, y_ref.dtype)


def _expert_gemms(xs, tile_expert, tile_valid, w1, b1, w2, b2, *, tm, tf, tn):
    n_rows = xs.shape[0]
    n_exp, d, f2 = w1.shape
    f = f2 // 2
    tf, tn = min(tf, f), min(tn, d)
    nf = f // tf
    n_tiles = n_rows // tm
    b1r = b1.reshape(n_exp, 1, f2)
    b2r = b2.reshape(n_exp, 1, d)
    grid1 = pltpu.PrefetchScalarGridSpec(
        num_scalar_prefetch=2, grid=(nf, n_tiles),
        in_specs=[pl.BlockSpec((tm, d // 2), lambda j, i, te, tv: (i, 0)),
                  pl.BlockSpec((1, d, tf), lambda j, i, te, tv: (te[i], 0, j)),
                  pl.BlockSpec((1, d, tf), lambda j, i, te, tv: (te[i], 0, nf + j)),
                  pl.BlockSpec((1, 1, tf), lambda j, i, te, tv: (te[i], 0, j)),
                  pl.BlockSpec((1, 1, tf), lambda j, i, te, tv: (te[i], 0, nf + j))],
        out_specs=pl.BlockSpec((tm, tf), lambda j, i, te, tv: (i, j)),
        scratch_shapes=[pltpu.VMEM((d, tf), BF16), pltpu.VMEM((d, tf), BF16)])
    hidden = pl.pallas_call(
        _gemm1_body, out_shape=jax.ShapeDtypeStruct((n_rows, f), BF16), grid_spec=grid1,
        compiler_params=_params("arbitrary", "arbitrary"), name="moe_gemm1",
    )(tile_expert, tile_valid, xs, w1, w1, b1r, b1r)
    grid2 = pltpu.PrefetchScalarGridSpec(
        num_scalar_prefetch=2, grid=(d // tn, n_tiles),
        in_specs=[pl.BlockSpec((tm, f), lambda j, i, te, tv: (i, 0)),
                  pl.BlockSpec((1, f, tn), lambda j, i, te, tv: (te[i], 0, j)),
                  pl.BlockSpec((1, 1, tn), lambda j, i, te, tv: (te[i], 0, j))],
        out_specs=pl.BlockSpec((tm, tn), lambda j, i, te, tv: (i, j)),
        scratch_shapes=[pltpu.VMEM((f, tn), BF16)])
    return pl.pallas_call(
        _gemm2_body, out_shape=jax.ShapeDtypeStruct((n_rows, d), F32), grid_spec=grid2,
        compiler_params=_params("arbitrary", "arbitrary"), name="moe_gemm2",
    )(tile_expert, tile_valid, hidden, w2, b2r)


def _combine_body(pos_ref, y_ref, gates_ref, h_ref, nw_ref, o_ref, buf_ref, sem, *, tb, normalize):
    base = pl.program_id(0) * tb * TOP_K

    def issue(tok, carry):
        for k in range(TOP_K):
            _row_copy(y_ref, pos_ref[base + tok * TOP_K + k], buf_ref.at[k], tok, sem).start()
        return carry

    def drain(tok, carry):
        for _ in range(TOP_K):
            _row_copy(y_ref, 0, buf_ref.at[0], 0, sem).wait()
        return carry

    lax.fori_loop(0, tb, issue, 0, unroll=4)
    lax.fori_loop(0, tb, drain, 0, unroll=4)
    gates = gates_ref[...]
    out = h_ref[...]
    for k in range(TOP_K):
        out = out + gates[:, k:k + 1] * buf_ref[k]
    if normalize:
        ms = jnp.mean(out * out, axis=-1, keepdims=True)
        out = out * lax.rsqrt(ms + EPS) * nw_ref[...]
    o_ref[...] = out


def _combine(pos_flat, y, gates, h1, final_w, tb, normalize):
    t, d = h1.shape
    grid_spec = pltpu.PrefetchScalarGridSpec(
        num_scalar_prefetch=1, grid=(t // tb,),
        in_specs=[pl.BlockSpec(memory_space=pl.ANY),
                  pl.BlockSpec((tb, LANES), lambda i, pos: (i, 0)),
                  pl.BlockSpec((tb, d), lambda i, pos: (i, 0)),
                  pl.BlockSpec((1, d), lambda i, pos: (0, 0))],
        out_specs=pl.BlockSpec((tb, d), lambda i, pos: (i, 0)),
        scratch_shapes=[pltpu.VMEM((TOP_K, tb, d), F32), pltpu.SemaphoreType.DMA])
    return pl.pallas_call(
        functools.partial(_combine_body, tb=tb, normalize=normalize),
        out_shape=jax.ShapeDtypeStruct((t, d), F32),
        grid_spec=grid_spec,
        compiler_params=_params("arbitrary"),
        name="moe_combine",
    )(pos_flat, y, gates, h1, final_w.reshape(1, d))


def _moe(h1, norm_w, router_w, router_b, w1, b1, w2, b2, final_w, normalize, *, tm_router, tm_expert,
         tf, tn, tb_dispatch, tb):
    t, d = h1.shape
    tn_tokens, ids, gates, rank, cnt = _router(h1, norm_w, router_w, router_b, tm_router)
    counts = cnt[0, :N_EXPERTS].astype(I32)
    padded = (counts + tm_expert - 1) // tm_expert * tm_expert
    seg_end = jnp.cumsum(padded)
    seg_start = seg_end - padded
    expert_of = ids[:, :TOP_K, None] == jnp.arange(N_EXPERTS, dtype=I32)
    pos = (jnp.sum(jnp.where(expert_of, seg_start, 0), axis=-1) + rank[:, :TOP_K]).reshape(-1)
    n_rows = t * TOP_K + N_EXPERTS * tm_expert
    n_tiles = n_rows // tm_expert
    tile_ids = jnp.arange(n_tiles, dtype=I32)
    tiles_done = jnp.sum((tile_ids[:, None] >= (seg_end // tm_expert)[None, :]).astype(I32), axis=1)
    tile_expert = jnp.minimum(tiles_done, N_EXPERTS - 1)
    tile_valid = (tile_ids < seg_end[-1] // tm_expert).astype(I32)
    expert_pad = jnp.where(padded > 0, seg_end - tm_expert, -1)
    tail_pad = seg_end[-1] + jnp.arange(N_EXPERTS, dtype=I32) * tm_expert
    tail_pad = jnp.where(tail_pad < n_rows, tail_pad, -1)
    xs = _dispatch(pos, jnp.concatenate([expert_pad, tail_pad]), tn_tokens, n_rows, tb_dispatch, tm_expert)
    y = _expert_gemms(xs, tile_expert, tile_valid, w1, b1, w2, b2, tm=tm_expert, tf=tf, tn=tn)
    return _combine(pos, y, gates, h1, final_w, tb, normalize)


def _layer(x2, batch, seq, w_in, conv_w, conv_b, dt_bias, a_log, d_skip, ssd_norm_w, w_ssd_branch, w_attn_branch,
           w_out, norm_mix_w, norm_ffn_w, router_w, router_b, w1, b1, w2, b2, final_w, normalize):
    t, d = x2.shape
    ssd_inner = dt_bias.shape[0] * SSD_HEAD_DIM
    conv_dim = conv_w.shape[1]
    attn_inner = w_attn_branch.shape[0]
    idx_q = IDX_HEADS * IDX_DIM
    tm = min(ROW_TILE, seq)

    sizes = (ssd_inner, conv_dim, dt_bias.shape[0], attn_inner, attn_inner, attn_inner,
             idx_q, IDX_DIM, IDX_HEADS, d, d)
    offs = [0]
    for s in sizes:
        offs.append(offs[-1] + s)
    col = lambda i: w_in[:, offs[i]:offs[i + 1]]
    w_z, w_xbc, w_dt, w_q, w_k, w_v, w_qi, w_ki, w_wi, w_gs, w_ga = (col(i) for i in range(len(sizes)))
    zeros_k = jnp.zeros_like(w_ki)
    w_ki2 = jnp.concatenate([w_ki, zeros_k, zeros_k, w_ki], axis=1)
    n_misc = w_dt.shape[1] + w_wi.shape[1]
    w_misc = jnp.pad(jnp.concatenate([w_dt, w_wi], axis=1), ((0, 0), (0, LANES - n_misc)))
    b16 = lambda w: w.astype(BF16)

    xn = _rmsnorm(x2, norm_mix_w, BF16, tm)
    mm = functools.partial(_matmul, tm=tm)
    z = mm(xn, b16(w_z), F32, "proj_z")
    xbc = mm(xn, b16(w_xbc), F32, "proj_xbc")
    gate_logits = mm(xn, b16(jnp.concatenate([w_gs, w_ga], axis=1)), F32, "proj_gates")
    misc = mm(xn, b16(w_misc), F32, "proj_misc")
    v = mm(xn, b16(w_v), BF16, "proj_v")
    attn_tabs = _rope_tables(seq, ATTN_HEAD_DIM // 4, ATTN_HEAD_DIM)
    idx_tabs = _rope_tables(seq, IDX_DIM // 4, IDX_DIM)
    qk = _rope_matmul(xn, b16(jnp.concatenate([w_q, w_k], axis=1)), attn_tabs, ATTN_HEAD_DIM // 4, seq,
                      "proj_qk", tm=tm)
    qi = _rope_matmul(xn, b16(w_qi), idx_tabs, IDX_DIM // 4, seq, "proj_qi", tm=tm)
    ki = _rope_matmul(xn, b16(w_ki2), idx_tabs, IDX_DIM // 4, seq, "proj_ki", tm=tm)

    y_ssd = _ssd(xbc, z, misc, conv_w, conv_b, dt_bias, a_log, d_skip, ssd_norm_w, batch, seq)
    mask = _indexer_mask(qi, ki, misc, batch, seq, w_lane=w_dt.shape[1])
    y_attn = _attention(qk, v, mask, batch, seq, tq=ATTN_Q_TILE, tk=ATTN_K_TILE)

    g_spec = lambda off: lambda tm_, tn_: pl.BlockSpec((tm_, tn_), lambda j, i: (i, off * (d // tn_) + j))
    same = lambda tm_, tn_: pl.BlockSpec((tm_, tn_), lambda j, i: (i, j))
    part = mm(y_ssd, b16(w_ssd_branch), F32, "merge_ssd", epilogue=_ep_gate, extras=(gate_logits,),
              extra_specs=(g_spec(0),))
    merged = mm(y_attn, b16(w_attn_branch), BF16, "merge_attn", epilogue=_ep_gate_add,
                extras=(gate_logits, part), extra_specs=(g_spec(1), same))
    h1 = mm(merged, b16(w_out), F32, "out_proj", epilogue=_ep_residual, extras=(x2,), extra_specs=(same,))

    return _moe(h1, norm_ffn_w, router_w, router_b, w1, b1, w2, b2, final_w, normalize,
                tm_router=min(ROUTER_ROW_TILE, t), tm_expert=EXPERT_ROW_TILE, tf=EXPERT_HIDDEN_TILE,
                tn=EXPERT_OUT_TILE, tb_dispatch=min(DISPATCH_TOKENS, t), tb=min(COMBINE_TOKENS, t))


def kernel(x, w_in, conv_w, conv_b, dt_bias, a_log, d_skip, ssd_norm_w, w_ssd_branch, w_attn_branch, w_out,
           norm_mix_w, norm_ffn_w, router_w, router_b, w_expert_in, b_expert_in, w_expert_out, b_expert_out,
           final_norm_w):
    batch, seq, d = x.shape
    depth = w_in.shape[0]
    per_layer = (w_in, conv_w, conv_b, dt_bias, a_log, d_skip, ssd_norm_w, w_ssd_branch, w_attn_branch, w_out,
                 norm_mix_w, norm_ffn_w, router_w, router_b, w_expert_in, b_expert_in, w_expert_out, b_expert_out)
    h = x.reshape(batch * seq, d)
    for layer in range(depth):
        h = _layer(h, batch, seq, *(p[layer] for p in per_layer), final_norm_w, layer == depth - 1)
    return h.reshape(batch, seq, d)
```

```python
import functools
import math

import jax
import jax.numpy as jnp
from jax import lax
from jax.experimental import pallas as pl
from jax.experimental.pallas import tpu as pltpu

EPS = 1e-5
SSD_HEADS = 32
SSD_HEAD_DIM = 64
SSD_GROUPS = 4
SSD_STATE = 128
SSD_CONV = 4
SSD_CHUNK = 128
ATTN_HEADS = 16
ATTN_HEAD_DIM = 128
ROPE_THETA = 500000.0
IDX_HEADS = 8
IDX_DIM = 64
IDX_TOPK = 256
Q_BLOCK = 128
N_EXPERTS = 32
TOP_K = 4
D_EXPERT = 2048
SWIGLU_ALPHA = 1.702
SWIGLU_LIMIT = 7.0

LANES = 128
SUBLANES = 8
VMEM_LIMIT_BYTES = 56 * 2**20
INT32_MIN = -2**31
MASKED_LOGIT = -1e30

ROW_TILE = 512
ATTN_Q_TILE = 256
ATTN_K_TILE = 1024
ATTN_HEAD_GROUP = 4
INDEXER_KEY_CHUNK = 512
INDEXER_ROW_GROUPS = 4
ROUTER_ROW_TILE = 256
EXPERT_ROW_TILE = 512
EXPERT_HIDDEN_TILE = 1024
EXPERT_OUT_TILE = 2048
DISPATCH_TOKENS = 512
COMBINE_TOKENS = 256

F32 = jnp.float32
BF16 = jnp.bfloat16
I32 = jnp.int32


def _params(*sem):
    return pltpu.CompilerParams(dimension_semantics=sem, vmem_limit_bytes=VMEM_LIMIT_BYTES)


def _sigmoid(x):
    return 1.0 / (1.0 + jnp.exp(-x))


def _dot(a, b):
    return jnp.dot(a, b, preferred_element_type=F32)


def _dot_nt(a, b):
    return lax.dot_general(a, b, (((1,), (1,)), ((), ())), preferred_element_type=F32)


def _split3(x):
    hi = x.astype(BF16)
    r1 = x - hi.astype(F32)
    mid = r1.astype(BF16)
    lo = (r1 - mid.astype(F32)).astype(BF16)
    return hi, mid, lo


def _rmsnorm_body(x_ref, w_ref, o_ref):
    x = x_ref[...]
    ms = jnp.mean(x * x, axis=-1, keepdims=True)
    o_ref[...] = (x * lax.rsqrt(ms + EPS) * w_ref[...]).astype(o_ref.dtype)


def _rmsnorm(x, w, out_dtype, tm):
    t, d = x.shape
    return pl.pallas_call(
        _rmsnorm_body,
        out_shape=jax.ShapeDtypeStruct((t, d), out_dtype),
        grid=(t // tm,),
        in_specs=[pl.BlockSpec((tm, d), lambda i: (i, 0)), pl.BlockSpec((1, d), lambda i: (0, 0))],
        out_specs=pl.BlockSpec((tm, d), lambda i: (i, 0)),
        compiler_params=_params("parallel"),
        name="rmsnorm",
    )(x, w.reshape(1, d))


def _ep_identity(acc):
    return acc


def _ep_rope(acc, c, s1, s2, *, shift):
    outs = []
    for g in range(acc.shape[1] // LANES):
        o = acc[:, g * LANES:(g + 1) * LANES]
        outs.append(o * c + pltpu.roll(o, LANES - shift, 1) * s1 + pltpu.roll(o, shift, 1) * s2)
    return outs[0] if len(outs) == 1 else jnp.concatenate(outs, axis=1)


def _ep_gate(acc, g):
    return _sigmoid(g) * acc


def _ep_gate_add(acc, g, p):
    return p + _sigmoid(g) * acc


def _ep_residual(acc, x):
    return x + acc


def _mm_body(*refs, epilogue):
    a_ref, w_ref = refs[0], refs[1]
    o_ref = refs[-1]
    acc = _dot(a_ref[...], w_ref[...])
    o_ref[...] = epilogue(acc, *[r[...] for r in refs[2:-1]]).astype(o_ref.dtype)


MAX_COL_TILE = 2048


def _col_tile(n):
    return max(t for t in range(LANES, min(n, MAX_COL_TILE) + 1, LANES) if n % t == 0)


def _matmul(a, w, out_dtype, name, *, tm, epilogue=_ep_identity, extras=(), extra_specs=()):
    m, k = a.shape
    n = w.shape[1]
    tn = _col_tile(n)
    return pl.pallas_call(
        functools.partial(_mm_body, epilogue=epilogue),
        out_shape=jax.ShapeDtypeStruct((m, n), out_dtype),
        grid=(n // tn, m // tm),
        in_specs=[pl.BlockSpec((tm, k), lambda j, i: (i, 0)),
                  pl.BlockSpec((k, tn), lambda j, i: (0, j)), *[f(tm, tn) for f in extra_specs]],
        out_specs=pl.BlockSpec((tm, tn), lambda j, i: (i, j)),
        compiler_params=_params("parallel", "parallel"),
        name=name,
    )(a, w, *extras)


def _rope_tables(length, rot_dim, head_dim):
    half = rot_dim // 2
    inv = ROPE_THETA ** (-jnp.arange(0, rot_dim, 2, dtype=F32) / rot_dim)
    ang = jnp.arange(length, dtype=F32)[:, None] * inv[None, :]
    cos, sin = jnp.cos(ang), jnp.sin(ang)
    zeros = lambda n: jnp.zeros((length, n), F32)
    c = jnp.concatenate([cos, cos, jnp.ones((length, head_dim - rot_dim), F32)], axis=1)
    s1 = jnp.concatenate([-sin, zeros(head_dim - half)], axis=1)
    s2 = jnp.concatenate([zeros(half), sin, zeros(head_dim - rot_dim)], axis=1)
    reps = LANES // head_dim
    return tuple(jnp.tile(t, (1, reps)) for t in (c, s1, s2))


def _rope_matmul(a, w, tables, rot_dim, seq, name, *, tm):
    nblk = seq // tm
    spec = lambda tm_, tn_: pl.BlockSpec((tm_, LANES), lambda j, i: (i % nblk, 0))
    return _matmul(a, w, BF16, name, tm=tm,
                   epilogue=functools.partial(_ep_rope, shift=rot_dim // 2),
                   extras=tables, extra_specs=(spec, spec, spec))


def _ssd_body(xbc_ref, z_ref, misc_ref, cw_ref, cb_ref, dtb_ref, alog_ref, dsk_ref, nw_ref,
              ltri_ref, exp_ref, o_ref, tail_ref, state_ref, *, inner, groups, heads):
    ch = SSD_CHUNK
    n = SSD_STATE
    gw = inner // groups
    c_idx = pl.program_id(1)
    tail = SSD_CONV - 1

    @pl.when(c_idx == 0)
    def _():
        tail_ref[...] = jnp.zeros(tail_ref.shape, F32)
        state_ref[...] = jnp.zeros(state_ref.shape, F32)

    cur = xbc_ref[...]
    prev_rows = tail_ref[...]
    row8 = lax.broadcasted_iota(I32, (SUBLANES, cur.shape[1]), 0)
    conv = cb_ref[...] + cur * cw_ref[tail:tail + 1, :]
    for j in range(tail):
        shift = tail - j
        rolled = pltpu.roll(cur, shift, 0)
        head = jnp.where(row8 < shift, pltpu.roll(prev_rows, shift, 0), rolled[0:SUBLANES, :])
        conv = conv + jnp.concatenate([head, rolled[SUBLANES:, :]], axis=0) * cw_ref[j:j + 1, :]
    tail_ref[...] = cur[ch - SUBLANES:, :]
    xbc = conv * _sigmoid(conv)
    xs = xbc[:, :inner]
    bm = xbc[:, inner:inner + groups * n]
    cm = xbc[:, inner + groups * n:]

    lane = lax.broadcasted_iota(I32, (1, LANES), 1)
    head_lane = lane < heads
    a = jnp.where(head_lane, -jnp.exp(alog_ref[...]), 0.0)
    dtr = misc_ref[...] + dtb_ref[...]
    dt = jnp.maximum(dtr, 0.0) + jnp.log1p(jnp.exp(-jnp.abs(dtr)))
    da = dt * a

    ltri = ltri_ref[...]
    cs = sum(_dot(ltri, p) for p in _split3(da))
    expand = exp_ref[...]
    dt_e = sum(_dot(p, expand) for p in _split3(dt))
    cs_e = sum(_dot(p, expand) for p in _split3(cs))
    cs_last = cs_e[ch - 1:ch, :]
    ecs = jnp.exp(cs_e)
    dte = jnp.exp(cs_last - cs_e)
    chunk_decay = jnp.exp(cs_last)

    xdt = xs * dt_e
    xdt_b = xdt.astype(BF16)
    xw_b = (xdt * dte).astype(BF16)
    cs_t = cs.T

    row = lax.broadcasted_iota(I32, (ch, ch), 0)
    col = lax.broadcasted_iota(I32, (ch, ch), 1)
    causal = row >= col
    first_half = lax.broadcasted_iota(I32, (ch, LANES), 1) < SSD_HEAD_DIM
    heads_per_group = heads // groups

    diag_cols, off_cols = [], []
    for g in range(groups):
        bg = bm[:, g * n:(g + 1) * n]
        cg_b = cm[:, g * n:(g + 1) * n].astype(BF16)
        cb = _dot_nt(cg_b, bg.astype(BF16))
        bg_t = bg.T.astype(BF16)
        for pr in range(heads_per_group // 2):
            h0 = g * heads_per_group + 2 * pr
            xp = xdt_b[:, h0 * SSD_HEAD_DIM:(h0 + 2) * SSD_HEAD_DIM]
            res = []
            for h in (h0, h0 + 1):
                seg = cs[:, h:h + 1] - cs_t[h:h + 1, :]
                decay = jnp.exp(jnp.where(causal, seg, -jnp.inf))
                res.append(_dot((cb * decay).astype(BF16), xp))
            diag_cols.append(jnp.where(first_half, res[0], res[1]))
        prev = state_ref[g]
        off_cols.append(_dot(cg_b, prev.astype(BF16)) * ecs[:, g * gw:(g + 1) * gw])
        states = _dot(bg_t, xw_b[:, g * gw:(g + 1) * gw])
        state_ref[g] = prev * chunk_decay[:, g * gw:(g + 1) * gw] + states

    y = jnp.concatenate(diag_cols, axis=1) + jnp.concatenate(off_cols, axis=1) + xs * dsk_ref[...]

    z = z_ref[...]
    gated = y * (z * _sigmoid(z))
    ms = jnp.mean(gated * gated, axis=-1, keepdims=True)
    o_ref[...] = (gated * lax.rsqrt(ms + EPS) * nw_ref[...]).astype(o_ref.dtype)


def _ssd(xbc, z, misc, conv_w, conv_b, dt_bias, a_log, d_skip, norm_w, batch, seq):
    t, conv_dim = xbc.shape
    inner = z.shape[1]
    heads = dt_bias.shape[0]
    groups = SSD_GROUPS
    ch = SSD_CHUNK
    nc = seq // ch
    pad = lambda v: jnp.pad(v.astype(F32), (0, LANES - heads)).reshape(1, LANES)
    ltri = jnp.tril(jnp.ones((ch, ch), F32)).astype(BF16)
    head_of_lane = jnp.arange(inner) // SSD_HEAD_DIM
    expand = (jnp.arange(LANES)[:, None] == head_of_lane[None, :]).astype(BF16)
    d_exp = jnp.repeat(d_skip.astype(F32), SSD_HEAD_DIM).reshape(1, inner)
    row = lambda b, c: (b * nc + c, 0)
    const = lambda b, c: (0, 0)
    return pl.pallas_call(
        functools.partial(_ssd_body, inner=inner, groups=groups, heads=heads),
        out_shape=jax.ShapeDtypeStruct((t, inner), BF16),
        grid=(batch, nc),
        in_specs=[pl.BlockSpec((ch, conv_dim), row), pl.BlockSpec((ch, inner), row),
                  pl.BlockSpec((ch, LANES), row),
                  pl.BlockSpec((SSD_CONV, conv_dim), const), pl.BlockSpec((1, conv_dim), const),
                  pl.BlockSpec((1, LANES), const), pl.BlockSpec((1, LANES), const),
                  pl.BlockSpec((1, inner), const), pl.BlockSpec((1, inner), const),
                  pl.BlockSpec((ch, ch), const), pl.BlockSpec((LANES, inner), const)],
        out_specs=pl.BlockSpec((ch, inner), row),
        scratch_shapes=[pltpu.VMEM((SUBLANES, conv_dim), F32),
                        pltpu.VMEM((groups, SSD_STATE, inner // groups), F32)],
        compiler_params=_params("arbitrary", "arbitrary"),
        name="ssd_scan",
    )(xbc, z, misc, conv_w, conv_b.reshape(1, conv_dim), pad(dt_bias), pad(a_log), d_exp,
      norm_w.reshape(1, inner), ltri, expand)


def _indexer_body(qi_ref, ki_ref, misc_ref, u_ref, ones_ref, mask_ref, keys_ref, *,
                  seq, kc, rows, group_rows, topk, w_lane, idx_scale):
    qb = pl.program_id(1)
    q0 = qb * rows
    n_chunks = (q0 + rows + kc - 1) // kc
    n_unmasked = (q0 + 1) // kc
    n_lane_chunks = kc // LANES
    lane_chunk = lambda a, j: a[:, j * LANES:(j + 1) * LANES]
    w = misc_ref[...] * idx_scale
    qi = qi_ref[...]
    ones = ones_ref[...]

    def score_chunk(c, carry, *, masked):
        off = pl.multiple_of(c * kc, kc)
        k_lo = ki_ref[pl.ds(off, kc), 0:LANES]
        k_hi = ki_ref[pl.ds(off, kc), LANES:2 * LANES]
        s = jnp.zeros((rows, kc), F32)
        for j in range(IDX_HEADS // 2):
            qj = lane_chunk(qi, j)
            w0 = w[:, w_lane + 2 * j:w_lane + 2 * j + 1]
            w1 = w[:, w_lane + 2 * j + 1:w_lane + 2 * j + 2]
            s = s + w0 * jnp.maximum(_dot_nt(qj, k_lo), 0.0) + w1 * jnp.maximum(_dot_nt(qj, k_hi), 0.0)
        bits = lax.bitcast_convert_type(s, I32)
        key = jnp.where(bits >= 0, bits, bits ^ jnp.int32(0x7FFFFFFF))
        if masked:
            qpos = q0 + lax.broadcasted_iota(I32, (rows, kc), 0)
            kpos = off + lax.broadcasted_iota(I32, (rows, kc), 1)
            key = jnp.where(kpos <= qpos, key, jnp.int32(INT32_MIN))
        keys_ref[:, pl.ds(off, kc)] = key
        return carry

    lax.fori_loop(0, n_unmasked, functools.partial(score_chunk, masked=False), 0)
    lax.fori_loop(n_unmasked, n_chunks, functools.partial(score_chunk, masked=True), 0)

    groups = [slice(g * group_rows, (g + 1) * group_rows) for g in range(rows // group_rows)]

    def count(compare, thresh):
        accs = []
        for g, rows_g in enumerate(groups):
            thresh_g = thresh[rows_g]

            def body(c, acc, rows_g=rows_g, thresh_g=thresh_g):
                k = keys_ref[rows_g, pl.ds(pl.multiple_of(c * kc, kc), kc)]
                for j in range(n_lane_chunks):
                    acc = acc + jnp.where(compare(lane_chunk(k, j), thresh_g), 1, 0)
                return acc

            chunks_g = (q0 + (g + 1) * group_rows + kc - 1) // kc
            accs.append(lax.fori_loop(0, chunks_g, body, jnp.zeros((group_rows, LANES), I32)))
        acc = jnp.concatenate(accs, axis=0)
        return _dot(acc.astype(F32).astype(BF16), ones)

    def bit_body(i, state):
        t_u, n_ge = state
        cand_u = t_u | jnp.left_shift(jnp.int32(1), 31 - i)
        cnt = count(lambda k, c: k >= c, cand_u ^ jnp.int32(INT32_MIN))
        keep = cnt >= topk
        return jnp.where(keep, cand_u, t_u), jnp.where(keep, cnt, n_ge)

    n_admissible = (q0 + lax.broadcasted_iota(I32, (rows, LANES), 0) + 1).astype(F32)
    t_u, n_ge = lax.fori_loop(0, 32, bit_body, (jnp.zeros((rows, LANES), I32), n_admissible))
    t_s = t_u ^ jnp.int32(INT32_MIN)
    n_gt = count(lambda k, c: k > c, t_s)
    select_all = t_u == 0
    n_tie = jnp.where(select_all, 0.0, topk - n_gt)
    excess_ties = jnp.max(jnp.where(select_all, 0.0, n_ge - topk)) > 0.0

    @pl.when(jnp.logical_not(excess_ties))
    def _():
        def mask_chunk(c, carry):
            sl = pl.ds(pl.multiple_of(c * kc, kc), kc)
            k = keys_ref[:, sl]
            sel = [(lane_chunk(k, j) > t_s) | ((lane_chunk(k, j) == t_s) & jnp.logical_not(select_all))
                   for j in range(n_lane_chunks)]
            mask_ref[:, sl] = jnp.where(jnp.concatenate(sel, axis=1), 1, 0).astype(jnp.int8)
            return carry
        lax.fori_loop(0, n_chunks, mask_chunk, 0)

    @pl.when(excess_ties)
    def _():
        t_col = t_s[:, 0:1]
        n_tie_col = n_tie[:, 0:1]

        def mask_chunk(c, seen):
            sl = pl.ds(pl.multiple_of(c * kc, kc), kc)
            k = keys_ref[:, sl]
            tie = k == t_col
            tie_f = jnp.where(tie, 1.0, 0.0)
            rank = seen + _dot(tie_f.astype(BF16), u_ref[...])
            sel = (k > t_col) | (tie & (rank <= n_tie_col))
            mask_ref[:, sl] = jnp.where(sel, 1, 0).astype(jnp.int8)
            return seen + jnp.sum(tie_f, axis=1, keepdims=True)
        lax.fori_loop(0, n_chunks, mask_chunk, jnp.zeros((rows, 1), F32))

    def zero_chunk(c, carry):
        mask_ref[:, pl.ds(pl.multiple_of(c * kc, kc), kc)] = jnp.zeros((rows, kc), jnp.int8)
        return carry

    lax.fori_loop(n_chunks, seq // kc, zero_chunk, 0)


def _indexer_mask(qi, ki, misc, batch, seq, w_lane):
    t = qi.shape[0]
    group_rows = min(Q_BLOCK, seq)
    rows = min(INDEXER_ROW_GROUPS * group_rows, seq)
    nq = seq // rows
    kc = min(INDEXER_KEY_CHUNK, seq)
    topk = min(IDX_TOPK, seq // 4)
    assert seq // LANES <= 256, "per-lane counts must stay exact in bf16"
    upper = jnp.triu(jnp.ones((kc, kc), F32)).astype(BF16)
    ones = jnp.ones((LANES, LANES), BF16)
    idx_scale = (IDX_DIM ** -0.5) * (IDX_HEADS ** -0.5)
    return pl.pallas_call(
        functools.partial(_indexer_body, seq=seq, kc=kc, rows=rows, group_rows=group_rows, topk=topk, w_lane=w_lane,
                          idx_scale=idx_scale),
        out_shape=jax.ShapeDtypeStruct((t, seq), jnp.int8),
        grid=(batch, nq),
        in_specs=[pl.BlockSpec((rows, qi.shape[1]), lambda b, q: (b * nq + q, 0)),
                  pl.BlockSpec((seq, ki.shape[1]), lambda b, q: (b, 0)),
                  pl.BlockSpec((rows, LANES), lambda b, q: (b * nq + q, 0)),
                  pl.BlockSpec((kc, kc), lambda b, q: (0, 0)),
                  pl.BlockSpec((LANES, LANES), lambda b, q: (0, 0))],
        out_specs=pl.BlockSpec((rows, seq), lambda b, q: (b * nq + q, 0)),
        scratch_shapes=[pltpu.VMEM((rows, seq), I32)],
        compiler_params=_params("parallel", "arbitrary"),
        name="indexer_topk_mask",
    )(qi, ki, misc, upper, ones)


def _attn_body(q_ref, k_ref, v_ref, mask_ref, o_ref, acc_ref, m_ref, l_ref, *, tq, tk, heads, group, scale_log2e):
    qb = pl.program_id(1)
    kb = pl.program_id(2)
    last = ((qb + 1) * tq - 1) // tk
    hd = ATTN_HEAD_DIM

    @pl.when(kb == 0)
    def _():
        acc_ref[...] = jnp.zeros(acc_ref.shape, F32)
        m_ref[...] = jnp.full(m_ref.shape, MASKED_LOGIT, F32)
        l_ref[...] = jnp.zeros(l_ref.shape, F32)

    @pl.when(kb <= last)
    def _():
        bias = jnp.where(mask_ref[...].astype(I32) != 0, 0.0, MASKED_LOGIT)
        for h0 in range(0, heads, group):
            hs = range(h0, min(h0 + group, heads))
            col = lambda h: slice(h * hd, (h + 1) * hd)
            s = [_dot_nt(q_ref[:, col(h)], k_ref[:, col(h)]) * scale_log2e + bias for h in hs]
            m_new = [jnp.maximum(m_ref[h], jnp.max(sh, axis=1, keepdims=True)) for h, sh in zip(hs, s)]
            p = [jnp.exp2(sh - mh[:, 0:1]) for sh, mh in zip(s, m_new)]
            for h, ph, mh in zip(hs, p, m_new):
                alpha = jnp.exp2(m_ref[h] - mh)
                l_ref[h] = alpha * l_ref[h] + jnp.sum(ph, axis=1, keepdims=True)
                acc_ref[:, col(h)] = acc_ref[:, col(h)] * alpha + _dot(ph.astype(BF16), v_ref[:, col(h)])
                m_ref[h] = mh

    @pl.when(kb == pl.num_programs(2) - 1)
    def _():
        for h in range(heads):
            cols = slice(h * hd, (h + 1) * hd)
            o_ref[:, cols] = (acc_ref[:, cols] / l_ref[h]).astype(o_ref.dtype)


def _attention(qk, v, mask, batch, seq, *, tq, tk, group=ATTN_HEAD_GROUP):
    t, inner = v.shape
    heads = inner // ATTN_HEAD_DIM
    tq, tk = min(tq, seq), min(tk, seq)
    nq, nk = seq // tq, seq // tk
    last = lambda q: ((q + 1) * tq - 1) // tk
    return pl.pallas_call(
        functools.partial(_attn_body, tq=tq, tk=tk, heads=heads, group=group,
                          scale_log2e=ATTN_HEAD_DIM ** -0.5 * math.log2(math.e)),
        out_shape=jax.ShapeDtypeStruct((t, inner), BF16),
        grid=(batch, nq, nk),
        in_specs=[pl.BlockSpec((tq, inner), lambda b, q, k: (b * nq + q, 0)),
                  pl.BlockSpec((tk, inner), lambda b, q, k: (b * nk + jnp.minimum(k, last(q)), 1)),
                  pl.BlockSpec((tk, inner), lambda b, q, k: (b * nk + jnp.minimum(k, last(q)), 0)),
                  pl.BlockSpec((tq, tk), lambda b, q, k: (b * nq + q, jnp.minimum(k, last(q))))],
        out_specs=pl.BlockSpec((tq, inner), lambda b, q, k: (b * nq + q, 0)),
        scratch_shapes=[pltpu.VMEM((tq, inner), F32),
                        pltpu.VMEM((heads, tq, LANES), F32),
                        pltpu.VMEM((heads, tq, LANES), F32)],
        compiler_params=_params("parallel", "parallel", "arbitrary"),
        name="masked_attention",
    )(qk, qk, v, mask)


def _pack_bf16_pairs(x):
    h = x.shape[1] // 2
    hi = lax.bitcast_convert_type(x[:, :h].astype(F32), jnp.uint32)
    lo = lax.bitcast_convert_type(x[:, h:].astype(F32), jnp.uint32)
    return lax.bitcast_convert_type(hi | (lo >> 16), I32)


def _unpack_bf16_pairs(p):
    u = lax.bitcast_convert_type(p, jnp.uint32)
    hi = lax.bitcast_convert_type(u & jnp.uint32(0xFFFF0000), F32).astype(BF16)
    lo = lax.bitcast_convert_type(u << 16, F32).astype(BF16)
    return hi, lo


def _router_body(h_ref, nw_ref, rw_ref, rb_ref, ltri_ref, tn_ref, ids_ref, gates_ref, rank_ref, cnt_ref,
                 carry_ref):
    @pl.when(pl.program_id(0) == 0)
    def _():
        carry_ref[...] = jnp.zeros(carry_ref.shape, F32)

    x = h_ref[...]
    ms = jnp.mean(x * x, axis=-1, keepdims=True)
    tn = (x * lax.rsqrt(ms + EPS) * nw_ref[...]).astype(BF16)
    tn_ref[...] = _pack_bf16_pairs(tn)
    tm = x.shape[0]
    lane = lax.broadcasted_iota(I32, (tm, LANES), 1)
    logits = _dot(tn, rw_ref[...]) + rb_ref[...]
    work = jnp.where(lane < N_EXPERTS, logits, -jnp.inf)
    vals, hits = [], []
    for _ in range(TOP_K):
        mx = jnp.max(work, axis=1, keepdims=True)
        idx = jnp.min(jnp.where(work == mx, lane, LANES), axis=1, keepdims=True)
        hit = lane == idx
        work = jnp.where(hit, -jnp.inf, work)
        vals.append(mx)
        hits.append(hit)
    es = [jnp.exp(v - vals[0]) for v in vals]
    tot = sum(es)
    onehot = sum(jnp.where(h, 1.0, 0.0) for h in hits)
    before = _dot(ltri_ref[...], onehot.astype(BF16)) + carry_ref[...]
    lane_f = lane.astype(F32)
    ids = jnp.zeros((tm, LANES), F32)
    gates = jnp.zeros((tm, LANES), F32)
    ranks = jnp.zeros((tm, LANES), F32)
    for k in range(TOP_K):
        slot = lane == k
        ids = jnp.where(slot, jnp.sum(jnp.where(hits[k], lane_f, 0.0), axis=1, keepdims=True), ids)
        gates = jnp.where(slot, es[k] / tot, gates)
        ranks = jnp.where(slot, jnp.sum(jnp.where(hits[k], before, 0.0), axis=1, keepdims=True), ranks)
    ids_ref[...] = ids.astype(I32)
    gates_ref[...] = gates
    rank_ref[...] = ranks.astype(I32)
    carry_ref[...] = carry_ref[...] + jnp.sum(onehot, axis=0, keepdims=True)
    cnt_ref[...] = carry_ref[...]


def _router(h1, norm_w, router_w, router_b, tm):
    t, d = h1.shape
    rw = jnp.pad(router_w, ((0, 0), (0, LANES - N_EXPERTS))).astype(BF16)
    rb = jnp.pad(router_b.astype(F32), (0, LANES - N_EXPERTS)).reshape(1, LANES)
    ltri = jnp.tril(jnp.ones((tm, tm), F32), -1).astype(BF16)
    row = lambda i: (i, 0)
    const = lambda i: (0, 0)
    return pl.pallas_call(
        _router_body,
        out_shape=(jax.ShapeDtypeStruct((t, d // 2), I32), jax.ShapeDtypeStruct((t, LANES), I32),
                   jax.ShapeDtypeStruct((t, LANES), F32), jax.ShapeDtypeStruct((t, LANES), I32),
                   jax.ShapeDtypeStruct((1, LANES), F32)),
        grid=(t // tm,),
        in_specs=[pl.BlockSpec((tm, d), row), pl.BlockSpec((1, d), const), pl.BlockSpec((d, LANES), const),
                  pl.BlockSpec((1, LANES), const), pl.BlockSpec((tm, tm), const)],
        out_specs=(pl.BlockSpec((tm, d // 2), row), pl.BlockSpec((tm, LANES), row), pl.BlockSpec((tm, LANES), row),
                   pl.BlockSpec((tm, LANES), row), pl.BlockSpec((1, LANES), const)),
        scratch_shapes=[pltpu.VMEM((1, LANES), F32)],
        compiler_params=_params("arbitrary"),
        name="moe_router",
    )(h1, norm_w.reshape(1, d), rw, rb, ltri)


def _row_copy(src, src_row, dst, dst_row, sem):
    return pltpu.make_async_copy(src.at[pl.ds(src_row, 1)], dst.at[pl.ds(dst_row, 1)], sem)


def _dispatch_body(pos_ref, pad_ref, tn_ref, xs_ref, zeros_ref, sem, zero_sem, *, tb, pad_rows, n_fills):
    base = pl.program_id(0) * tb * TOP_K

    @pl.when(pl.program_id(0) == 0)
    def _():
        zeros_ref[...] = jnp.zeros(zeros_ref.shape, zeros_ref.dtype)
        fill = lambda e: pltpu.make_async_copy(
            zeros_ref, xs_ref.at[pl.ds(pl.multiple_of(pad_ref[e], pad_rows), pad_rows)], zero_sem)
        for e in range(n_fills):
            pl.when(pad_ref[e] >= 0)(lambda e=e: fill(e).start())
        for e in range(n_fills):
            pl.when(pad_ref[e] >= 0)(lambda e=e: fill(e).wait())

    def issue(tok, carry):
        for k in range(TOP_K):
            _row_copy(tn_ref, tok, xs_ref, pos_ref[base + tok * TOP_K + k], sem).start()
        return carry

    def drain(tok, carry):
        for _ in range(TOP_K):
            _row_copy(tn_ref, 0, xs_ref, 0, sem).wait()
        return carry

    lax.fori_loop(0, tb, issue, 0, unroll=4)
    lax.fori_loop(0, tb, drain, 0, unroll=4)


def _dispatch(pos_flat, pad_start, tn, n_rows, tb, pad_rows):
    t, d = tn.shape
    grid_spec = pltpu.PrefetchScalarGridSpec(
        num_scalar_prefetch=2, grid=(t // tb,),
        in_specs=[pl.BlockSpec((tb, d), lambda i, pos, pad: (i, 0))],
        out_specs=pl.BlockSpec(memory_space=pl.ANY),
        scratch_shapes=[pltpu.VMEM((pad_rows, d), tn.dtype), pltpu.SemaphoreType.DMA, pltpu.SemaphoreType.DMA])
    return pl.pallas_call(
        functools.partial(_dispatch_body, tb=tb, pad_rows=pad_rows, n_fills=pad_start.shape[0]),
        out_shape=jax.ShapeDtypeStruct((n_rows, d), tn.dtype),
        grid_spec=grid_spec,
        compiler_params=_params("arbitrary"),
        name="moe_dispatch",
    )(pos_flat, pad_start, tn)


def _swiglu(hg, hl):
    glu = jnp.minimum(hg, SWIGLU_LIMIT)
    lin = jnp.clip(hl, -SWIGLU_LIMIT, SWIGLU_LIMIT)
    return glu * _sigmoid(SWIGLU_ALPHA * glu) * (lin + 1.0)


def _expert_changed(te_ref, i):
    return (i == 0) | (te_ref[i] != te_ref[jnp.maximum(i - 1, 0)])


def _gemm1_body(te_ref, tv_ref, x_ref, wg_ref, wl_ref, bg_ref, bl_ref, h_ref, wgb_ref, wlb_ref):
    i = pl.program_id(1)

    @pl.when(_expert_changed(te_ref, i))
    def _():
        wgb_ref[...] = wg_ref[0].astype(BF16)
        wlb_ref[...] = wl_ref[0].astype(BF16)

    @pl.when(tv_ref[i] == 1)
    def _():
        x_hi, x_lo = _unpack_bf16_pairs(x_ref[...])
        half = x_hi.shape[1]
        hg = _dot(x_hi, wgb_ref[0:half, :]) + _dot(x_lo, wgb_ref[half:, :]) + bg_ref[0]
        hl = _dot(x_hi, wlb_ref[0:half, :]) + _dot(x_lo, wlb_ref[half:, :]) + bl_ref[0]
        h_ref[...] = _swiglu(hg, hl).astype(h_ref.dtype)

    @pl.when(tv_ref[i] == 0)
    def _():
        h_ref[...] = jnp.zeros(h_ref.shape, h_ref.dtype)


def _gemm2_body(te_ref, tv_ref, h_ref, w_ref, b_ref, y_ref, wb_ref):
    i = pl.program_id(1)

    @pl.when(_expert_changed(te_ref, i))
    def _():
        wb_ref[...] = w_ref[0].astype(BF16)

    @pl.when(tv_ref[i] == 1)
    def _():
        y_ref[...] = _dot(h_ref[...], wb_ref[...]) + b_ref[0]

    @pl.when(tv_ref[i] == 0)
    def _():
        y_ref[...] = jnp.zeros(y_ref.shape, y_ref.dtype)


def _expert_gemms(xs, tile_expert, tile_valid, w1, b1, w2, b2, *, tm, tf, tn):
    n_rows = xs.shape[0]
    n_exp, d, f2 = w1.shape
    f = f2 // 2
    tf, tn = min(tf, f), min(tn, d)
    nf = f // tf
    n_tiles = n_rows // tm
    b1r = b1.reshape(n_exp, 1, f2)
    b2r = b2.reshape(n_exp, 1, d)
    grid1 = pltpu.PrefetchScalarGridSpec(
        num_scalar_prefetch=2, grid=(nf, n_tiles),
        in_specs=[pl.BlockSpec((tm, d // 2), lambda j, i, te, tv: (i, 0)),
                  pl.BlockSpec((1, d, tf), lambda j, i, te, tv: (te[i], 0, j)),
                  pl.BlockSpec((1, d, tf), lambda j, i, te, tv: (te[i], 0, nf + j)),
                  pl.BlockSpec((1, 1, tf), lambda j, i, te, tv: (te[i], 0, j)),
                  pl.BlockSpec((1, 1, tf), lambda j, i, te, tv: (te[i], 0, nf + j))],
        out_specs=pl.BlockSpec((tm, tf), lambda j, i, te, tv: (i, j)),
        scratch_shapes=[pltpu.VMEM((d, tf), BF16), pltpu.VMEM((d, tf), BF16)])
    hidden = pl.pallas_call(
        _gemm1_body, out_shape=jax.ShapeDtypeStruct((n_rows, f), BF16), grid_spec=grid1,
        compiler_params=_params("arbitrary", "arbitrary"), name="moe_gemm1",
    )(tile_expert, tile_valid, xs, w1, w1, b1r, b1r)
    grid2 = pltpu.PrefetchScalarGridSpec(
        num_scalar_prefetch=2, grid=(d // tn, n_tiles),
        in_specs=[pl.BlockSpec((tm, f), lambda j, i, te, tv: (i, 0)),
                  pl.BlockSpec((1, f, tn), lambda j, i, te, tv: (te[i], 0, j)),
                  pl.BlockSpec((1, 1, tn), lambda j, i, te, tv: (te[i], 0, j))],
        out_specs=pl.BlockSpec((tm, tn), lambda j, i, te, tv: (i, j)),
        scratch_shapes=[pltpu.VMEM((f, tn), BF16)])
    return pl.pallas_call(
        _gemm2_body, out_shape=jax.ShapeDtypeStruct((n_rows, d), F32), grid_spec=grid2,
        compiler_params=_params("arbitrary", "arbitrary"), name="moe_gemm2",
    )(tile_expert, tile_valid, hidden, w2, b2r)


def _combine_body(pos_ref, y_ref, gates_ref, h_ref, nw_ref, o_ref, buf_ref, sem, *, tb, normalize):
    base = pl.program_id(0) * tb * TOP_K

    def issue(tok, carry):
        for k in range(TOP_K):
            _row_copy(y_ref, pos_ref[base + tok * TOP_K + k], buf_ref.at[k], tok, sem).start()
        return carry

    def drain(tok, carry):
        for _ in range(TOP_K):
            _row_copy(y_ref, 0, buf_ref.at[0], 0, sem).wait()
        return carry

    lax.fori_loop(0, tb, issue, 0, unroll=4)
    lax.fori_loop(0, tb, drain, 0, unroll=4)
    gates = gates_ref[...]
    out = h_ref[...]
    for k in range(TOP_K):
        out = out + gates[:, k:k + 1] * buf_ref[k]
    if normalize:
        ms = jnp.mean(out * out, axis=-1, keepdims=True)
        out = out * lax.rsqrt(ms + EPS) * nw_ref[...]
    o_ref[...] = out


def _combine(pos_flat, y, gates, h1, final_w, tb, normalize):
    t, d = h1.shape
    grid_spec = pltpu.PrefetchScalarGridSpec(
        num_scalar_prefetch=1, grid=(t // tb,),
        in_specs=[pl.BlockSpec(memory_space=pl.ANY),
                  pl.BlockSpec((tb, LANES), lambda i, pos: (i, 0)),
                  pl.BlockSpec((tb, d), lambda i, pos: (i, 0)),
                  pl.BlockSpec((1, d), lambda i, pos: (0, 0))],
        out_specs=pl.BlockSpec((tb, d), lambda i, pos: (i, 0)),
        scratch_shapes=[pltpu.VMEM((TOP_K, tb, d), F32), pltpu.SemaphoreType.DMA])
    return pl.pallas_call(
        functools.partial(_combine_body, tb=tb, normalize=normalize),
        out_shape=jax.ShapeDtypeStruct((t, d), F32),
        grid_spec=grid_spec,
        compiler_params=_params("arbitrary"),
        name="moe_combine",
    )(pos_flat, y, gates, h1, final_w.reshape(1, d))


def _moe(h1, norm_w, router_w, router_b, w1, b1, w2, b2, final_w, normalize, *, tm_router, tm_expert,
         tf, tn, tb_dispatch, tb):
    t, d = h1.shape
    tn_tokens, ids, gates, rank, cnt = _router(h1, norm_w, router_w, router_b, tm_router)
    counts = cnt[0, :N_EXPERTS].astype(I32)
    padded = (counts + tm_expert - 1) // tm_expert * tm_expert
    seg_end = jnp.cumsum(padded)
    seg_start = seg_end - padded
    expert_of = ids[:, :TOP_K, None] == jnp.arange(N_EXPERTS, dtype=I32)
    pos = (jnp.sum(jnp.where(expert_of, seg_start, 0), axis=-1) + rank[:, :TOP_K]).reshape(-1)
    n_rows = t * TOP_K + N_EXPERTS * tm_expert
    n_tiles = n_rows // tm_expert
    tile_ids = jnp.arange(n_tiles, dtype=I32)
    tiles_done = jnp.sum((tile_ids[:, None] >= (seg_end // tm_expert)[None, :]).astype(I32), axis=1)
    tile_expert = jnp.minimum(tiles_done, N_EXPERTS - 1)
    tile_valid = (tile_ids < seg_end[-1] // tm_expert).astype(I32)
    expert_pad = jnp.where(padded > 0, seg_end - tm_expert, -1)
    tail_pad = seg_end[-1] + jnp.arange(N_EXPERTS, dtype=I32) * tm_expert
    tail_pad = jnp.where(tail_pad < n_rows, tail_pad, -1)
    xs = _dispatch(pos, jnp.concatenate([expert_pad, tail_pad]), tn_tokens, n_rows, tb_dispatch, tm_expert)
    y = _expert_gemms(xs, tile_expert, tile_valid, w1, b1, w2, b2, tm=tm_expert, tf=tf, tn=tn)
    return _combine(pos, y, gates, h1, final_w, tb, normalize)


def _layer(x2, batch, seq, w_in, conv_w, conv_b, dt_bias, a_log, d_skip, ssd_norm_w, w_ssd_branch, w_attn_branch,
           w_out, norm_mix_w, norm_ffn_w, router_w, router_b, w1, b1, w2, b2, final_w, normalize):
    t, d = x2.shape
    ssd_inner = dt_bias.shape[0] * SSD_HEAD_DIM
    conv_dim = conv_w.shape[1]
    attn_inner = w_attn_branch.shape[0]
    idx_q = IDX_HEADS * IDX_DIM
    tm = min(ROW_TILE, seq)

    sizes = (ssd_inner, conv_dim, dt_bias.shape[0], attn_inner, attn_inner, attn_inner,
             idx_q, IDX_DIM, IDX_HEADS, d, d)
    offs = [0]
    for s in sizes:
        offs.append(offs[-1] + s)
    col = lambda i: w_in[:, offs[i]:offs[i + 1]]
    w_z, w_xbc, w_dt, w_q, w_k, w_v, w_qi, w_ki, w_wi, w_gs, w_ga = (col(i) for i in range(len(sizes)))
    zeros_k = jnp.zeros_like(w_ki)
    w_ki2 = jnp.concatenate([w_ki, zeros_k, zeros_k, w_ki], axis=1)
    n_misc = w_dt.shape[1] + w_wi.shape[1]
    w_misc = jnp.pad(jnp.concatenate([w_dt, w_wi], axis=1), ((0, 0), (0, LANES - n_misc)))
    b16 = lambda w: w.astype(BF16)

    xn = _rmsnorm(x2, norm_mix_w, BF16, tm)
    mm = functools.partial(_matmul, tm=tm)
    z = mm(xn, b16(w_z), F32, "proj_z")
    xbc = mm(xn, b16(w_xbc), F32, "proj_xbc")
    gate_logits = mm(xn, b16(jnp.concatenate([w_gs, w_ga], axis=1)), F32, "proj_gates")
    misc = mm(xn, b16(w_misc), F32, "proj_misc")
    v = mm(xn, b16(w_v), BF16, "proj_v")
    attn_tabs = _rope_tables(seq, ATTN_HEAD_DIM // 4, ATTN_HEAD_DIM)
    idx_tabs = _rope_tables(seq, IDX_DIM // 4, IDX_DIM)
    qk = _rope_matmul(xn, b16(jnp.concatenate([w_q, w_k], axis=1)), attn_tabs, ATTN_HEAD_DIM // 4, seq,
                      "proj_qk", tm=tm)
    qi = _rope_matmul(xn, b16(w_qi), idx_tabs, IDX_DIM // 4, seq, "proj_qi", tm=tm)
    ki = _rope_matmul(xn, b16(w_ki2), idx_tabs, IDX_DIM // 4, seq, "proj_ki", tm=tm)

    y_ssd = _ssd(xbc, z, misc, conv_w, conv_b, dt_bias, a_log, d_skip, ssd_norm_w, batch, seq)
    mask = _indexer_mask(qi, ki, misc, batch, seq, w_lane=w_dt.shape[1])
    y_attn = _attention(qk, v, mask, batch, seq, tq=ATTN_Q_TILE, tk=ATTN_K_TILE)

    g_spec = lambda off: lambda tm_, tn_: pl.BlockSpec((tm_, tn_), lambda j, i: (i, off * (d // tn_) + j))
    same = lambda tm_, tn_: pl.BlockSpec((tm_, tn_), lambda j, i: (i, j))
    part = mm(y_ssd, b16(w_ssd_branch), F32, "merge_ssd", epilogue=_ep_gate, extras=(gate_logits,),
              extra_specs=(g_spec(0),))
    merged = mm(y_attn, b16(w_attn_branch), BF16, "merge_attn", epilogue=_ep_gate_add,
                extras=(gate_logits, part), extra_specs=(g_spec(1), same))
    h1 = mm(merged, b16(w_out), F32, "out_proj", epilogue=_ep_residual, extras=(x2,), extra_specs=(same,))

    return _moe(h1, norm_ffn_w, router_w, router_b, w1, b1, w2, b2, final_w, normalize,
                tm_router=min(ROUTER_ROW_TILE, t), tm_expert=EXPERT_ROW_TILE, tf=EXPERT_HIDDEN_TILE,
                tn=EXPERT_OUT_TILE, tb_dispatch=min(DISPATCH_TOKENS, t), tb=min(COMBINE_TOKENS, t))


def kernel(x, w_in, conv_w, conv_b, dt_bias, a_log, d_skip, ssd_norm_w, w_ssd_branch, w_attn_branch, w_out,
           norm_mix_w, norm_ffn_w, router_w, router_b, w_expert_in, b_expert_in, w_expert_out, b_expert_out,
           final_norm_w):
    batch, seq, d = x.shape
    depth = w_in.shape[0]
    per_layer = (w_in, conv_w, conv_b, dt_bias, a_log, d_skip, ssd_norm_w, w_ssd_branch, w_attn_branch, w_out,
                 norm_mix_w, norm_ffn_w, router_w, router_b, w_expert_in, b_expert_in, w_expert_out, b_expert_out)
    h = x.reshape(batch * seq, d)
    for layer in range(depth):
        h = _layer(h, batch, seq, *(p[layer] for p in per_layer), final_norm_w, layer == depth - 1)
    return h.reshape(batch, seq, d)
```

```python
import functools
import math

import jax
import jax.numpy as jnp
from jax import lax
from jax.experimental import pallas as pl
from jax.experimental.pallas import tpu as pltpu

EPS = 1e-5
SSD_HEADS = 32
SSD_HEAD_DIM = 64
SSD_GROUPS = 4
SSD_STATE = 128
SSD_CONV = 4
SSD_CHUNK = 128
ATTN_HEADS = 16
ATTN_HEAD_DIM = 128
ROPE_THETA = 500000.0
IDX_HEADS = 8
IDX_DIM = 64
IDX_TOPK = 256
Q_BLOCK = 128
N_EXPERTS = 32
TOP_K = 4
D_EXPERT = 2048
SWIGLU_ALPHA = 1.702
SWIGLU_LIMIT = 7.0

LANES = 128
SUBLANES = 8
VMEM_LIMIT_BYTES = 56 * 2**20
INT32_MIN = -2**31
MASKED_LOGIT = -1e30

ROW_TILE = 512
ATTN_Q_TILE = 256
ATTN_K_TILE = 1024
ATTN_HEAD_GROUP = 4
INDEXER_KEY_CHUNK = 512
INDEXER_ROW_GROUPS = 4
ROUTER_ROW_TILE = 256
EXPERT_ROW_TILE = 512
EXPERT_HIDDEN_TILE = 1024
EXPERT_OUT_TILE = 2048
DISPATCH_TOKENS = 512
COMBINE_TOKENS = 512

F32 = jnp.float32
BF16 = jnp.bfloat16
I32 = jnp.int32


def _params(*sem):
    return pltpu.CompilerParams(dimension_semantics=sem, vmem_limit_bytes=VMEM_LIMIT_BYTES)


def _sigmoid(x):
    return 1.0 / (1.0 + jnp.exp(-x))


def _dot(a, b):
    return jnp.dot(a, b, preferred_element_type=F32)


def _dot_nt(a, b):
    return lax.dot_general(a, b, (((1,), (1,)), ((), ())), preferred_element_type=F32)


def _split3(x):
    hi = x.astype(BF16)
    r1 = x - hi.astype(F32)
    mid = r1.astype(BF16)
    lo = (r1 - mid.astype(F32)).astype(BF16)
    return hi, mid, lo


def _rmsnorm_body(x_ref, w_ref, o_ref):
    x = x_ref[...]
    ms = jnp.mean(x * x, axis=-1, keepdims=True)
    o_ref[...] = (x * lax.rsqrt(ms + EPS) * w_ref[...]).astype(o_ref.dtype)


def _rmsnorm(x, w, out_dtype, tm):
    t, d = x.shape
    return pl.pallas_call(
        _rmsnorm_body,
        out_shape=jax.ShapeDtypeStruct((t, d), out_dtype),
        grid=(t // tm,),
        in_specs=[pl.BlockSpec((tm, d), lambda i: (i, 0)), pl.BlockSpec((1, d), lambda i: (0, 0))],
        out_specs=pl.BlockSpec((tm, d), lambda i: (i, 0)),
        compiler_params=_params("parallel"),
        name="rmsnorm",
    )(x, w.reshape(1, d))


def _ep_identity(acc):
    return acc


def _ep_rope(acc, c, s1, s2, *, shift):
    outs = []
    for g in range(acc.shape[1] // LANES):
        o = acc[:, g * LANES:(g + 1) * LANES]
        outs.append(o * c + pltpu.roll(o, LANES - shift, 1) * s1 + pltpu.roll(o, shift, 1) * s2)
    return outs[0] if len(outs) == 1 else jnp.concatenate(outs, axis=1)


def _ep_gate(acc, g):
    return _sigmoid(g) * acc


def _ep_gate_add(acc, g, p):
    return p + _sigmoid(g) * acc


def _ep_residual(acc, x):
    return x + acc


def _mm_body(*refs, epilogue):
    a_ref, w_ref = refs[0], refs[1]
    o_ref = refs[-1]
    acc = _dot(a_ref[...], w_ref[...])
    o_ref[...] = epilogue(acc, *[r[...] for r in refs[2:-1]]).astype(o_ref.dtype)


MAX_COL_TILE = 2048


def _col_tile(n):
    return max(t for t in range(LANES, min(n, MAX_COL_TILE) + 1, LANES) if n % t == 0)


def _matmul(a, w, out_dtype, name, *, tm, epilogue=_ep_identity, extras=(), extra_specs=()):
    m, k = a.shape
    n = w.shape[1]
    tn = _col_tile(n)
    return pl.pallas_call(
        functools.partial(_mm_body, epilogue=epilogue),
        out_shape=jax.ShapeDtypeStruct((m, n), out_dtype),
        grid=(n // tn, m // tm),
        in_specs=[pl.BlockSpec((tm, k), lambda j, i: (i, 0)),
                  pl.BlockSpec((k, tn), lambda j, i: (0, j)), *[f(tm, tn) for f in extra_specs]],
        out_specs=pl.BlockSpec((tm, tn), lambda j, i: (i, j)),
        compiler_params=_params("parallel", "parallel"),
        name=name,
    )(a, w, *extras)


def _rope_tables(length, rot_dim, head_dim):
    half = rot_dim // 2
    inv = ROPE_THETA ** (-jnp.arange(0, rot_dim, 2, dtype=F32) / rot_dim)
    ang = jnp.arange(length, dtype=F32)[:, None] * inv[None, :]
    cos, sin = jnp.cos(ang), jnp.sin(ang)
    zeros = lambda n: jnp.zeros((length, n), F32)
    c = jnp.concatenate([cos, cos, jnp.ones((length, head_dim - rot_dim), F32)], axis=1)
    s1 = jnp.concatenate([-sin, zeros(head_dim - half)], axis=1)
    s2 = jnp.concatenate([zeros(half), sin, zeros(head_dim - rot_dim)], axis=1)
    reps = LANES // head_dim
    return tuple(jnp.tile(t, (1, reps)) for t in (c, s1, s2))


def _rope_matmul(a, w, tables, rot_dim, seq, name, *, tm):
    nblk = seq // tm
    spec = lambda tm_, tn_: pl.BlockSpec((tm_, LANES), lambda j, i: (i % nblk, 0))
    return _matmul(a, w, BF16, name, tm=tm,
                   epilogue=functools.partial(_ep_rope, shift=rot_dim // 2),
                   extras=tables, extra_specs=(spec, spec, spec))


def _ssd_body(xbc_ref, z_ref, misc_ref, cw_ref, cb_ref, dtb_ref, alog_ref, dsk_ref, nw_ref,
              ltri_ref, exp_ref, o_ref, tail_ref, state_ref, *, inner, groups, heads):
    ch = SSD_CHUNK
    n = SSD_STATE
    gw = inner // groups
    c_idx = pl.program_id(1)
    tail = SSD_CONV - 1

    @pl.when(c_idx == 0)
    def _():
        tail_ref[...] = jnp.zeros(tail_ref.shape, F32)
        state_ref[...] = jnp.zeros(state_ref.shape, F32)

    cur = xbc_ref[...]
    prev_rows = tail_ref[...]
    row8 = lax.broadcasted_iota(I32, (SUBLANES, cur.shape[1]), 0)
    conv = cb_ref[...] + cur * cw_ref[tail:tail + 1, :]
    for j in range(tail):
        shift = tail - j
        rolled = pltpu.roll(cur, shift, 0)
        head = jnp.where(row8 < shift, pltpu.roll(prev_rows, shift, 0), rolled[0:SUBLANES, :])
        conv = conv + jnp.concatenate([head, rolled[SUBLANES:, :]], axis=0) * cw_ref[j:j + 1, :]
    tail_ref[...] = cur[ch - SUBLANES:, :]
    xbc = conv * _sigmoid(conv)
    xs = xbc[:, :inner]
    bm = xbc[:, inner:inner + groups * n]
    cm = xbc[:, inner + groups * n:]

    lane = lax.broadcasted_iota(I32, (1, LANES), 1)
    head_lane = lane < heads
    a = jnp.where(head_lane, -jnp.exp(alog_ref[...]), 0.0)
    dtr = misc_ref[...] + dtb_ref[...]
    dt = jnp.maximum(dtr, 0.0) + jnp.log1p(jnp.exp(-jnp.abs(dtr)))
    da = dt * a

    ltri = ltri_ref[...]
    cs = sum(_dot(ltri, p) for p in _split3(da))
    expand = exp_ref[...]
    dt_e = sum(_dot(p, expand) for p in _split3(dt))
    cs_e = sum(_dot(p, expand) for p in _split3(cs))
    cs_last = cs_e[ch - 1:ch, :]
    ecs = jnp.exp(cs_e)
    dte = jnp.exp(cs_last - cs_e)
    chunk_decay = jnp.exp(cs_last)

    xdt = xs * dt_e
    xdt_b = xdt.astype(BF16)
    xw_b = (xdt * dte).astype(BF16)
    cs_t = cs.T

    row = lax.broadcasted_iota(I32, (ch, ch), 0)
    col = lax.broadcasted_iota(I32, (ch, ch), 1)
    causal = row >= col
    first_half = lax.broadcasted_iota(I32, (ch, LANES), 1) < SSD_HEAD_DIM
    heads_per_group = heads // groups

    diag_cols, off_cols = [], []
    for g in range(groups):
        bg = bm[:, g * n:(g + 1) * n]
        cg_b = cm[:, g * n:(g + 1) * n].astype(BF16)
        cb = _dot_nt(cg_b, bg.astype(BF16))
        bg_t = bg.T.astype(BF16)
        for pr in range(heads_per_group // 2):
            h0 = g * heads_per_group + 2 * pr
            xp = xdt_b[:, h0 * SSD_HEAD_DIM:(h0 + 2) * SSD_HEAD_DIM]
            res = []
            for h in (h0, h0 + 1):
                seg = cs[:, h:h + 1] - cs_t[h:h + 1, :]
                decay = jnp.exp(jnp.where(causal, seg, -jnp.inf))
                res.append(_dot((cb * decay).astype(BF16), xp))
            diag_cols.append(jnp.where(first_half, res[0], res[1]))
        prev = state_ref[g]
        off_cols.append(_dot(cg_b, prev.astype(BF16)) * ecs[:, g * gw:(g + 1) * gw])
        states = _dot(bg_t, xw_b[:, g * gw:(g + 1) * gw])
        state_ref[g] = prev * chunk_decay[:, g * gw:(g + 1) * gw] + states

    y = jnp.concatenate(diag_cols, axis=1) + jnp.concatenate(off_cols, axis=1) + xs * dsk_ref[...]

    z = z_ref[...]
    gated = y * (z * _sigmoid(z))
    ms = jnp.mean(gated * gated, axis=-1, keepdims=True)
    o_ref[...] = (gated * lax.rsqrt(ms + EPS) * nw_ref[...]).astype(o_ref.dtype)


def _ssd(xbc, z, misc, conv_w, conv_b, dt_bias, a_log, d_skip, norm_w, batch, seq):
    t, conv_dim = xbc.shape
    inner = z.shape[1]
    heads = dt_bias.shape[0]
    groups = SSD_GROUPS
    ch = SSD_CHUNK
    nc = seq // ch
    pad = lambda v: jnp.pad(v.astype(F32), (0, LANES - heads)).reshape(1, LANES)
    ltri = jnp.tril(jnp.ones((ch, ch), F32)).astype(BF16)
    head_of_lane = jnp.arange(inner) // SSD_HEAD_DIM
    expand = (jnp.arange(LANES)[:, None] == head_of_lane[None, :]).astype(BF16)
    d_exp = jnp.repeat(d_skip.astype(F32), SSD_HEAD_DIM).reshape(1, inner)
    row = lambda b, c: (b * nc + c, 0)
    const = lambda b, c: (0, 0)
    return pl.pallas_call(
        functools.partial(_ssd_body, inner=inner, groups=groups, heads=heads),
        out_shape=jax.ShapeDtypeStruct((t, inner), BF16),
        grid=(batch, nc),
        in_specs=[pl.BlockSpec((ch, conv_dim), row), pl.BlockSpec((ch, inner), row),
                  pl.BlockSpec((ch, LANES), row),
                  pl.BlockSpec((SSD_CONV, conv_dim), const), pl.BlockSpec((1, conv_dim), const),
                  pl.BlockSpec((1, LANES), const), pl.BlockSpec((1, LANES), const),
                  pl.BlockSpec((1, inner), const), pl.BlockSpec((1, inner), const),
                  pl.BlockSpec((ch, ch), const), pl.BlockSpec((LANES, inner), const)],
        out_specs=pl.BlockSpec((ch, inner), row),
        scratch_shapes=[pltpu.VMEM((SUBLANES, conv_dim), F32),
                        pltpu.VMEM((groups, SSD_STATE, inner // groups), F32)],
        compiler_params=_params("arbitrary", "arbitrary"),
        name="ssd_scan",
    )(xbc, z, misc, conv_w, conv_b.reshape(1, conv_dim), pad(dt_bias), pad(a_log), d_exp,
      norm_w.reshape(1, inner), ltri, expand)


def _indexer_body(qi_ref, ki_ref, misc_ref, u_ref, ones_ref, mask_ref, keys_ref, *,
                  seq, kc, rows, group_rows, topk, w_lane, idx_scale):
    qb = pl.program_id(1)
    q0 = qb * rows
    n_chunks = (q0 + rows + kc - 1) // kc
    n_unmasked = (q0 + 1) // kc
    n_lane_chunks = kc // LANES
    lane_chunk = lambda a, j: a[:, j * LANES:(j + 1) * LANES]
    w = misc_ref[...] * idx_scale
    qi = qi_ref[...]
    ones = ones_ref[...]

    def score_chunk(c, carry, *, masked):
        off = pl.multiple_of(c * kc, kc)
        k_lo = ki_ref[pl.ds(off, kc), 0:LANES]
        k_hi = ki_ref[pl.ds(off, kc), LANES:2 * LANES]
        s = jnp.zeros((rows, kc), F32)
        for j in range(IDX_HEADS // 2):
            qj = lane_chunk(qi, j)
            w0 = w[:, w_lane + 2 * j:w_lane + 2 * j + 1]
            w1 = w[:, w_lane + 2 * j + 1:w_lane + 2 * j + 2]
            s = s + w0 * jnp.maximum(_dot_nt(qj, k_lo), 0.0) + w1 * jnp.maximum(_dot_nt(qj, k_hi), 0.0)
        bits = lax.bitcast_convert_type(s, I32)
        key = jnp.where(bits >= 0, bits, bits ^ jnp.int32(0x7FFFFFFF))
        if masked:
            qpos = q0 + lax.broadcasted_iota(I32, (rows, kc), 0)
            kpos = off + lax.broadcasted_iota(I32, (rows, kc), 1)
            key = jnp.where(kpos <= qpos, key, jnp.int32(INT32_MIN))
        keys_ref[:, pl.ds(off, kc)] = key
        return carry

    lax.fori_loop(0, n_unmasked, functools.partial(score_chunk, masked=False), 0)
    lax.fori_loop(n_unmasked, n_chunks, functools.partial(score_chunk, masked=True), 0)

    groups = [slice(g * group_rows, (g + 1) * group_rows) for g in range(rows // group_rows)]

    def count(compare, thresh):
        accs = []
        for g, rows_g in enumerate(groups):
            thresh_g = thresh[rows_g]

            def body(c, acc, rows_g=rows_g, thresh_g=thresh_g):
                k = keys_ref[rows_g, pl.ds(pl.multiple_of(c * kc, kc), kc)]
                for j in range(n_lane_chunks):
                    acc = acc + jnp.where(compare(lane_chunk(k, j), thresh_g), 1, 0)
                return acc

            chunks_g = (q0 + (g + 1) * group_rows + kc - 1) // kc
            accs.append(lax.fori_loop(0, chunks_g, body, jnp.zeros((group_rows, LANES), I32)))
        acc = jnp.concatenate(accs, axis=0)
        return _dot(acc.astype(F32).astype(BF16), ones)

    def bit_body(i, state):
        t_u, n_ge = state
        cand_u = t_u | jnp.left_shift(jnp.int32(1), 31 - i)
        cnt = count(lambda k, c: k >= c, cand_u ^ jnp.int32(INT32_MIN))
        keep = cnt >= topk
        return jnp.where(keep, cand_u, t_u), jnp.where(keep, cnt, n_ge)

    n_admissible = (q0 + lax.broadcasted_iota(I32, (rows, LANES), 0) + 1).astype(F32)
    t_u, n_ge = lax.fori_loop(0, 32, bit_body, (jnp.zeros((rows, LANES), I32), n_admissible))
    t_s = t_u ^ jnp.int32(INT32_MIN)
    n_gt = count(lambda k, c: k > c, t_s)
    select_all = t_u == 0
    n_tie = jnp.where(select_all, 0.0, topk - n_gt)
    excess_ties = jnp.max(jnp.where(select_all, 0.0, n_ge - topk)) > 0.0

    @pl.when(jnp.logical_not(excess_ties))
    def _():
        def mask_chunk(c, carry):
            sl = pl.ds(pl.multiple_of(c * kc, kc), kc)
            k = keys_ref[:, sl]
            sel = [(lane_chunk(k, j) > t_s) | ((lane_chunk(k, j) == t_s) & jnp.logical_not(select_all))
                   for j in range(n_lane_chunks)]
            mask_ref[:, sl] = jnp.where(jnp.concatenate(sel, axis=1), 1, 0).astype(jnp.int8)
            return carry
        lax.fori_loop(0, n_chunks, mask_chunk, 0)

    @pl.when(excess_ties)
    def _():
        t_col = t_s[:, 0:1]
        n_tie_col = n_tie[:, 0:1]

        def mask_chunk(c, seen):
            sl = pl.ds(pl.multiple_of(c * kc, kc), kc)
            k = keys_ref[:, sl]
            tie = k == t_col
            tie_f = jnp.where(tie, 1.0, 0.0)
            rank = seen + _dot(tie_f.astype(BF16), u_ref[...])
            sel = (k > t_col) | (tie & (rank <= n_tie_col))
            mask_ref[:, sl] = jnp.where(sel, 1, 0).astype(jnp.int8)
            return seen + jnp.sum(tie_f, axis=1, keepdims=True)
        lax.fori_loop(0, n_chunks, mask_chunk, jnp.zeros((rows, 1), F32))

    def zero_chunk(c, carry):
        mask_ref[:, pl.ds(pl.multiple_of(c * kc, kc), kc)] = jnp.zeros((rows, kc), jnp.int8)
        return carry

    lax.fori_loop(n_chunks, seq // kc, zero_chunk, 0)


def _indexer_mask(qi, ki, misc, batch, seq, w_lane):
    t = qi.shape[0]
    group_rows = min(Q_BLOCK, seq)
    rows = min(INDEXER_ROW_GROUPS * group_rows, seq)
    nq = seq // rows
    kc = min(INDEXER_KEY_CHUNK, seq)
    topk = min(IDX_TOPK, seq // 4)
    assert seq // LANES <= 256, "per-lane counts must stay exact in bf16"
    upper = jnp.triu(jnp.ones((kc, kc), F32)).astype(BF16)
    ones = jnp.ones((LANES, LANES), BF16)
    idx_scale = (IDX_DIM ** -0.5) * (IDX_HEADS ** -0.5)
    return pl.pallas_call(
        functools.partial(_indexer_body, seq=seq, kc=kc, rows=rows, group_rows=group_rows, topk=topk, w_lane=w_lane,
                          idx_scale=idx_scale),
        out_shape=jax.ShapeDtypeStruct((t, seq), jnp.int8),
        grid=(batch, nq),
        in_specs=[pl.BlockSpec((rows, qi.shape[1]), lambda b, q: (b * nq + q, 0)),
                  pl.BlockSpec((seq, ki.shape[1]), lambda b, q: (b, 0)),
                  pl.BlockSpec((rows, LANES), lambda b, q: (b * nq + q, 0)),
                  pl.BlockSpec((kc, kc), lambda b, q: (0, 0)),
                  pl.BlockSpec((LANES, LANES), lambda b, q: (0, 0))],
        out_specs=pl.BlockSpec((rows, seq), lambda b, q: (b * nq + q, 0)),
        scratch_shapes=[pltpu.VMEM((rows, seq), I32)],
        compiler_params=_params("parallel", "arbitrary"),
        name="indexer_topk_mask",
    )(qi, ki, misc, upper, ones)


def _attn_body(q_ref, k_ref, v_ref, mask_ref, o_ref, acc_ref, m_ref, l_ref, *, tq, tk, heads, group, scale_log2e):
    qb = pl.program_id(1)
    kb = pl.program_id(2)
    last = ((qb + 1) * tq - 1) // tk
    hd = ATTN_HEAD_DIM

    @pl.when(kb == 0)
    def _():
        acc_ref[...] = jnp.zeros(acc_ref.shape, F32)
        m_ref[...] = jnp.full(m_ref.shape, MASKED_LOGIT, F32)
        l_ref[...] = jnp.zeros(l_ref.shape, F32)

    @pl.when(kb <= last)
    def _():
        bias = jnp.where(mask_ref[...].astype(I32) != 0, 0.0, MASKED_LOGIT)
        for h0 in range(0, heads, group):
            hs = range(h0, min(h0 + group, heads))
            col = lambda h: slice(h * hd, (h + 1) * hd)
            s = [_dot_nt(q_ref[:, col(h)], k_ref[:, col(h)]) * scale_log2e + bias for h in hs]
            m_new = [jnp.maximum(m_ref[h], jnp.max(sh, axis=1, keepdims=True)) for h, sh in zip(hs, s)]
            p = [jnp.exp2(sh - mh[:, 0:1]) for sh, mh in zip(s, m_new)]
            for h, ph, mh in zip(hs, p, m_new):
                alpha = jnp.exp2(m_ref[h] - mh)
                l_ref[h] = alpha * l_ref[h] + jnp.sum(ph, axis=1, keepdims=True)
                acc_ref[:, col(h)] = acc_ref[:, col(h)] * alpha + _dot(ph.astype(BF16), v_ref[:, col(h)])
                m_ref[h] = mh

    @pl.when(kb == pl.num_programs(2) - 1)
    def _():
        for h in range(heads):
            cols = slice(h * hd, (h + 1) * hd)
            o_ref[:, cols] = (acc_ref[:, cols] / l_ref[h]).astype(o_ref.dtype)


def _attention(qk, v, mask, batch, seq, *, tq, tk, group=ATTN_HEAD_GROUP):
    t, inner = v.shape
    heads = inner // ATTN_HEAD_DIM
    tq, tk = min(tq, seq), min(tk, seq)
    nq, nk = seq // tq, seq // tk
    last = lambda q: ((q + 1) * tq - 1) // tk
    return pl.pallas_call(
        functools.partial(_attn_body, tq=tq, tk=tk, heads=heads, group=group,
                          scale_log2e=ATTN_HEAD_DIM ** -0.5 * math.log2(math.e)),
        out_shape=jax.ShapeDtypeStruct((t, inner), BF16),
        grid=(batch, nq, nk),
        in_specs=[pl.BlockSpec((tq, inner), lambda b, q, k: (b * nq + q, 0)),
                  pl.BlockSpec((tk, inner), lambda b, q, k: (b * nk + jnp.minimum(k, last(q)), 1)),
                  pl.BlockSpec((tk, inner), lambda b, q, k: (b * nk + jnp.minimum(k, last(q)), 0)),
                  pl.BlockSpec((tq, tk), lambda b, q, k: (b * nq + q, jnp.minimum(k, last(q))))],
        out_specs=pl.BlockSpec((tq, inner), lambda b, q, k: (b * nq + q, 0)),
        scratch_shapes=[pltpu.VMEM((tq, inner), F32),
                        pltpu.VMEM((heads, tq, LANES), F32),
                        pltpu.VMEM((heads, tq, LANES), F32)],
        compiler_params=_params("parallel", "parallel", "arbitrary"),
        name="masked_attention",
    )(qk, qk, v, mask)


def _pack_bf16_pairs(x):
    h = x.shape[1] // 2
    hi = lax.bitcast_convert_type(x[:, :h].astype(F32), jnp.uint32)
    lo = lax.bitcast_convert_type(x[:, h:].astype(F32), jnp.uint32)
    return lax.bitcast_convert_type(hi | (lo >> 16), I32)


def _unpack_bf16_pairs(p):
    u = lax.bitcast_convert_type(p, jnp.uint32)
    hi = lax.bitcast_convert_type(u & jnp.uint32(0xFFFF0000), F32).astype(BF16)
    lo = lax.bitcast_convert_type(u << 16, F32).astype(BF16)
    return hi, lo


def _router_body(h_ref, nw_ref, rw_ref, rb_ref, ltri_ref, tn_ref, ids_ref, gates_ref, rank_ref, cnt_ref,
                 carry_ref):
    @pl.when(pl.program_id(0) == 0)
    def _():
        carry_ref[...] = jnp.zeros(carry_ref.shape, F32)

    x = h_ref[...]
    ms = jnp.mean(x * x, axis=-1, keepdims=True)
    tn = (x * lax.rsqrt(ms + EPS) * nw_ref[...]).astype(BF16)
    tn_ref[...] = _pack_bf16_pairs(tn)
    tm = x.shape[0]
    lane = lax.broadcasted_iota(I32, (tm, LANES), 1)
    logits = _dot(tn, rw_ref[...]) + rb_ref[...]
    work = jnp.where(lane < N_EXPERTS, logits, -jnp.inf)
    vals, hits = [], []
    for _ in range(TOP_K):
        mx = jnp.max(work, axis=1, keepdims=True)
        idx = jnp.min(jnp.where(work == mx, lane, LANES), axis=1, keepdims=True)
        hit = lane == idx
        work = jnp.where(hit, -jnp.inf, work)
        vals.append(mx)
        hits.append(hit)
    es = [jnp.exp(v - vals[0]) for v in vals]
    tot = sum(es)
    onehot = sum(jnp.where(h, 1.0, 0.0) for h in hits)
    before = _dot(ltri_ref[...], onehot.astype(BF16)) + carry_ref[...]
    lane_f = lane.astype(F32)
    ids = jnp.zeros((tm, LANES), F32)
    gates = jnp.zeros((tm, LANES), F32)
    ranks = jnp.zeros((tm, LANES), F32)
    for k in range(TOP_K):
        slot = lane == k
        ids = jnp.where(slot, jnp.sum(jnp.where(hits[k], lane_f, 0.0), axis=1, keepdims=True), ids)
        gates = jnp.where(slot, es[k] / tot, gates)
        ranks = jnp.where(slot, jnp.sum(jnp.where(hits[k], before, 0.0), axis=1, keepdims=True), ranks)
    ids_ref[...] = ids.astype(I32)
    gates_ref[...] = gates
    rank_ref[...] = ranks.astype(I32)
    carry_ref[...] = carry_ref[...] + jnp.sum(onehot, axis=0, keepdims=True)
    cnt_ref[...] = carry_ref[...]


def _router(h1, norm_w, router_w, router_b, tm):
    t, d = h1.shape
    rw = jnp.pad(router_w, ((0, 0), (0, LANES - N_EXPERTS))).astype(BF16)
    rb = jnp.pad(router_b.astype(F32), (0, LANES - N_EXPERTS)).reshape(1, LANES)
    ltri = jnp.tril(jnp.ones((tm, tm), F32), -1).astype(BF16)
    row = lambda i: (i, 0)
    const = lambda i: (0, 0)
    return pl.pallas_call(
        _router_body,
        out_shape=(jax.ShapeDtypeStruct((t, d // 2), I32), jax.ShapeDtypeStruct((t, LANES), I32),
                   jax.ShapeDtypeStruct((t, LANES), F32), jax.ShapeDtypeStruct((t, LANES), I32),
                   jax.ShapeDtypeStruct((1, LANES), F32)),
        grid=(t // tm,),
        in_specs=[pl.BlockSpec((tm, d), row), pl.BlockSpec((1, d), const), pl.BlockSpec((d, LANES), const),
                  pl.BlockSpec((1, LANES), const), pl.BlockSpec((tm, tm), const)],
        out_specs=(pl.BlockSpec((tm, d // 2), row), pl.BlockSpec((tm, LANES), row), pl.BlockSpec((tm, LANES), row),
                   pl.BlockSpec((tm, LANES), row), pl.BlockSpec((1, LANES), const)),
        scratch_shapes=[pltpu.VMEM((1, LANES), F32)],
        compiler_params=_params("arbitrary"),
        name="moe_router",
    )(h1, norm_w.reshape(1, d), rw, rb, ltri)


def _row_copy(src, src_row, dst, dst_row, sem):
    return pltpu.make_async_copy(src.at[pl.ds(src_row, 1)], dst.at[pl.ds(dst_row, 1)], sem)


def _dispatch_body(pos_ref, pad_ref, tn_ref, xs_ref, zeros_ref, sem, zero_sem, *, tb, pad_rows, n_fills):
    base = pl.program_id(0) * tb * TOP_K

    @pl.when(pl.program_id(0) == 0)
    def _():
        zeros_ref[...] = jnp.zeros(zeros_ref.shape, zeros_ref.dtype)
        fill = lambda e: pltpu.make_async_copy(
            zeros_ref, xs_ref.at[pl.ds(pl.multiple_of(pad_ref[e], pad_rows), pad_rows)], zero_sem)
        for e in range(n_fills):
            pl.when(pad_ref[e] >= 0)(lambda e=e: fill(e).start())
        for e in range(n_fills):
            pl.when(pad_ref[e] >= 0)(lambda e=e: fill(e).wait())

    def issue(tok, carry):
        for k in range(TOP_K):
            _row_copy(tn_ref, tok, xs_ref, pos_ref[base + tok * TOP_K + k], sem).start()
        return carry

    def drain(tok, carry):
        for _ in range(TOP_K):
            _row_copy(tn_ref, 0, xs_ref, 0, sem).wait()
        return carry

    lax.fori_loop(0, tb, issue, 0, unroll=4)
    lax.fori_loop(0, tb, drain, 0, unroll=4)


def _dispatch(pos_flat, pad_start, tn, n_rows, tb, pad_rows):
    t, d = tn.shape
    grid_spec = pltpu.PrefetchScalarGridSpec(
        num_scalar_prefetch=2, grid=(t // tb,),
        in_specs=[pl.BlockSpec((tb, d), lambda i, pos, pad: (i, 0))],
        out_specs=pl.BlockSpec(memory_space=pl.ANY),
        scratch_shapes=[pltpu.VMEM((pad_rows, d), tn.dtype), pltpu.SemaphoreType.DMA, pltpu.SemaphoreType.DMA])
    return pl.pallas_call(
        functools.partial(_dispatch_body, tb=tb, pad_rows=pad_rows, n_fills=pad_start.shape[0]),
        out_shape=jax.ShapeDtypeStruct((n_rows, d), tn.dtype),
        grid_spec=grid_spec,
        compiler_params=_params("arbitrary"),
        name="moe_dispatch",
    )(pos_flat, pad_start, tn)


def _swiglu(hg, hl):
    glu = jnp.minimum(hg, SWIGLU_LIMIT)
    lin = jnp.clip(hl, -SWIGLU_LIMIT, SWIGLU_LIMIT)
    return glu * _sigmoid(SWIGLU_ALPHA * glu) * (lin + 1.0)


def _expert_changed(te_ref, i):
    return (i == 0) | (te_ref[i] != te_ref[jnp.maximum(i - 1, 0)])


def _gemm1_body(te_ref, tv_ref, x_ref, wg_ref, wl_ref, bg_ref, bl_ref, h_ref, wgb_ref, wlb_ref):
    i = pl.program_id(1)

    @pl.when(_expert_changed(te_ref, i))
    def _():
        wgb_ref[...] = wg_ref[0].astype(BF16)
        wlb_ref[...] = wl_ref[0].astype(BF16)

    @pl.when(tv_ref[i] == 1)
    def _():
        x_hi, x_lo = _unpack_bf16_pairs(x_ref[...])
        half = x_hi.shape[1]
        hg = _dot(x_hi, wgb_ref[0:half, :]) + _dot(x_lo, wgb_ref[half:, :]) + bg_ref[0]
        hl = _dot(x_hi, wlb_ref[0:half, :]) + _dot(x_lo, wlb_ref[half:, :]) + bl_ref[0]
        h_ref[...] = _swiglu(hg, hl).astype(h_ref.dtype)

    @pl.when(tv_ref[i] == 0)
    def _():
        h_ref[...] = jnp.zeros(h_ref.shape, h_ref.dtype)


def _gemm2_body(te_ref, tv_ref, h_ref, w_ref, b_ref, y_ref, wb_ref):
    i = pl.program_id(1)

    @pl.when(_expert_changed(te_ref, i))
    def _():
        wb_ref[...] = w_ref[0].astype(BF16)

    @pl.when(tv_ref[i] == 1)
    def _():
        y_ref[...] = _dot(h_ref[...], wb_ref[...]) + b_ref[0]

    @pl.when(tv_ref[i] == 0)
    def _():
        y_ref[...] = jnp.zeros(y_ref.shape, y_ref.dtype)


def _expert_gemms(xs, tile_expert, tile_valid, w1, b1, w2, b2, *, tm, tf, tn):
    n_rows = xs.shape[0]
    n_exp, d, f2 = w1.shape
    f = f2 // 2
    tf, tn = min(tf, f), min(tn, d)
    nf = f // tf
    n_tiles = n_rows // tm
    b1r = b1.reshape(n_exp, 1, f2)
    b2r = b2.reshape(n_exp, 1, d)
    grid1 = pltpu.PrefetchScalarGridSpec(
        num_scalar_prefetch=2, grid=(nf, n_tiles),
        in_specs=[pl.BlockSpec((tm, d // 2), lambda j, i, te, tv: (i, 0)),
                  pl.BlockSpec((1, d, tf), lambda j, i, te, tv: (te[i], 0, j)),
                  pl.BlockSpec((1, d, tf), lambda j, i, te, tv: (te[i], 0, nf + j)),
                  pl.BlockSpec((1, 1, tf), lambda j, i, te, tv: (te[i], 0, j)),
                  pl.BlockSpec((1, 1, tf), lambda j, i, te, tv: (te[i], 0, nf + j))],
        out_specs=pl.BlockSpec((tm, tf), lambda j, i, te, tv: (i, j)),
        scratch_shapes=[pltpu.VMEM((d, tf), BF16), pltpu.VMEM((d, tf), BF16)])
    hidden = pl.pallas_call(
        _gemm1_body, out_shape=jax.ShapeDtypeStruct((n_rows, f), BF16), grid_spec=grid1,
        compiler_params=_params("arbitrary", "arbitrary"), name="moe_gemm1",
    )(tile_expert, tile_valid, xs, w1, w1, b1r, b1r)
    grid2 = pltpu.PrefetchScalarGridSpec(
        num_scalar_prefetch=2, grid=(d // tn, n_tiles),
        in_specs=[pl.BlockSpec((tm, f), lambda j, i, te, tv: (i, 0)),
                  pl.BlockSpec((1, f, tn), lambda j, i, te, tv: (te[i], 0, j)),
                  pl.BlockSpec((1, 1, tn), lambda j, i, te, tv: (te[i], 0, j))],
        out_specs=pl.BlockSpec((tm, tn), lambda j, i, te, tv: (i, j)),
        scratch_shapes=[pltpu.VMEM((f, tn), BF16)])
    return pl.pallas_call(
        _gemm2_body, out_shape=jax.ShapeDtypeStruct((n_rows, d), F32), grid_spec=grid2,
        compiler_params=_params("arbitrary", "arbitrary"), name="moe_gemm2",
    )(tile_expert, tile_valid, hidden, w2, b2r)


def _combine_body(pos_ref, y_ref, gates_ref, h_ref, nw_ref, o_ref, buf_ref, sem, *, tb, normalize):
    base = pl.program_id(0) * tb * TOP_K

    def issue(tok, carry):
        for k in range(TOP_K):
            _row_copy(y_ref, pos_ref[base + tok * TOP_K + k], buf_ref.at[k], tok, sem).start()
        return carry

    def drain(tok, carry):
        for _ in range(TOP_K):
            _row_copy(y_ref, 0, buf_ref.at[0], 0, sem).wait()
        return carry

    lax.fori_loop(0, tb, issue, 0, unroll=4)
    lax.fori_loop(0, tb, drain, 0, unroll=4)
    gates = gates_ref[...]
    out = h_ref[...]
    for k in range(TOP_K):
        out = out + gates[:, k:k + 1] * buf_ref[k]
    if normalize:
        ms = jnp.mean(out * out, axis=-1, keepdims=True)
        out = out * lax.rsqrt(ms + EPS) * nw_ref[...]
    o_ref[...] = out


def _combine(pos_flat, y, gates, h1, final_w, tb, normalize):
    t, d = h1.shape
    grid_spec = pltpu.PrefetchScalarGridSpec(
        num_scalar_prefetch=1, grid=(t // tb,),
        in_specs=[pl.BlockSpec(memory_space=pl.ANY),
                  pl.BlockSpec((tb, LANES), lambda i, pos: (i, 0)),
                  pl.BlockSpec((tb, d), lambda i, pos: (i, 0)),
                  pl.BlockSpec((1, d), lambda i, pos: (0, 0))],
        out_specs=pl.BlockSpec((tb, d), lambda i, pos: (i, 0)),
        scratch_shapes=[pltpu.VMEM((TOP_K, tb, d), F32), pltpu.SemaphoreType.DMA])
    return pl.pallas_call(
        functools.partial(_combine_body, tb=tb, normalize=normalize),
        out_shape=jax.ShapeDtypeStruct((t, d), F32),
        grid_spec=grid_spec,
        compiler_params=_params("arbitrary"),
        name="moe_combine",
    )(pos_flat, y, gates, h1, final_w.reshape(1, d))


def _moe(h1, norm_w, router_w, router_b, w1, b1, w2, b2, final_w, normalize, *, tm_router, tm_expert,
         tf, tn, tb_dispatch, tb):
    t, d = h1.shape
    tn_tokens, ids, gates, rank, cnt = _router(h1, norm_w, router_w, router_b, tm_router)
    counts = cnt[0, :N_EXPERTS].astype(I32)
    padded = (counts + tm_expert - 1) // tm_expert * tm_expert
    seg_end = jnp.cumsum(padded)
    seg_start = seg_end - padded
    expert_of = ids[:, :TOP_K, None] == jnp.arange(N_EXPERTS, dtype=I32)
    pos = (jnp.sum(jnp.where(expert_of, seg_start, 0), axis=-1) + rank[:, :TOP_K]).reshape(-1)
    n_rows = t * TOP_K + N_EXPERTS * tm_expert
    n_tiles = n_rows // tm_expert
    tile_ids = jnp.arange(n_tiles, dtype=I32)
    tiles_done = jnp.sum((tile_ids[:, None] >= (seg_end // tm_expert)[None, :]).astype(I32), axis=1)
    tile_expert = jnp.minimum(tiles_done, N_EXPERTS - 1)
    tile_valid = (tile_ids < seg_end[-1] // tm_expert).astype(I32)
    expert_pad = jnp.where(padded > 0, seg_end - tm_expert, -1)
    tail_pad = seg_end[-1] + jnp.arange(N_EXPERTS, dtype=I32) * tm_expert
    tail_pad = jnp.where(tail_pad < n_rows, tail_pad, -1)
    xs = _dispatch(pos, jnp.concatenate([expert_pad, tail_pad]), tn_tokens, n_rows, tb_dispatch, tm_expert)
    y = _expert_gemms(xs, tile_expert, tile_valid, w1, b1, w2, b2, tm=tm_expert, tf=tf, tn=tn)
    return _combine(pos, y, gates, h1, final_w, tb, normalize)


def _layer(x2, batch, seq, w_in, conv_w, conv_b, dt_bias, a_log, d_skip, ssd_norm_w, w_ssd_branch, w_attn_branch,
           w_out, norm_mix_w, norm_ffn_w, router_w, router_b, w1, b1, w2, b2, final_w, normalize):
    t, d = x2.shape
    ssd_inner = dt_bias.shape[0] * SSD_HEAD_DIM
    conv_dim = conv_w.shape[1]
    attn_inner = w_attn_branch.shape[0]
    idx_q = IDX_HEADS * IDX_DIM
    tm = min(ROW_TILE, seq)

    sizes = (ssd_inner, conv_dim, dt_bias.shape[0], attn_inner, attn_inner, attn_inner,
             idx_q, IDX_DIM, IDX_HEADS, d, d)
    offs = [0]
    for s in sizes:
        offs.append(offs[-1] + s)
    col = lambda i: w_in[:, offs[i]:offs[i + 1]]
    w_z, w_xbc, w_dt, w_q, w_k, w_v, w_qi, w_ki, w_wi, w_gs, w_ga = (col(i) for i in range(len(sizes)))
    zeros_k = jnp.zeros_like(w_ki)
    w_ki2 = jnp.concatenate([w_ki, zeros_k, zeros_k, w_ki], axis=1)
    n_misc = w_dt.shape[1] + w_wi.shape[1]
    w_misc = jnp.pad(jnp.concatenate([w_dt, w_wi], axis=1), ((0, 0), (0, LANES - n_misc)))
    b16 = lambda w: w.astype(BF16)

    xn = _rmsnorm(x2, norm_mix_w, BF16, tm)
    mm = functools.partial(_matmul, tm=tm)
    z = mm(xn, b16(w_z), F32, "proj_z")
    xbc = mm(xn, b16(w_xbc), F32, "proj_xbc")
    gate_logits = mm(xn, b16(jnp.concatenate([w_gs, w_ga], axis=1)), F32, "proj_gates")
    misc = mm(xn, b16(w_misc), F32, "proj_misc")
    v = mm(xn, b16(w_v), BF16, "proj_v")
    attn_tabs = _rope_tables(seq, ATTN_HEAD_DIM // 4, ATTN_HEAD_DIM)
    idx_tabs = _rope_tables(seq, IDX_DIM // 4, IDX_DIM)
    qk = _rope_matmul(xn, b16(jnp.concatenate([w_q, w_k], axis=1)), attn_tabs, ATTN_HEAD_DIM // 4, seq,
                      "proj_qk", tm=tm)
    qi = _rope_matmul(xn, b16(w_qi), idx_tabs, IDX_DIM // 4, seq, "proj_qi", tm=tm)
    ki = _rope_matmul(xn, b16(w_ki2), idx_tabs, IDX_DIM // 4, seq, "proj_ki", tm=tm)

    y_ssd = _ssd(xbc, z, misc, conv_w, conv_b, dt_bias, a_log, d_skip, ssd_norm_w, batch, seq)
    mask = _indexer_mask(qi, ki, misc, batch, seq, w_lane=w_dt.shape[1])
    y_attn = _attention(qk, v, mask, batch, seq, tq=ATTN_Q_TILE, tk=ATTN_K_TILE)

    g_spec = lambda off: lambda tm_, tn_: pl.BlockSpec((tm_, tn_), lambda j, i: (i, off * (d // tn_) + j))
    same = lambda tm_, tn_: pl.BlockSpec((tm_, tn_), lambda j, i: (i, j))
    part = mm(y_ssd, b16(w_ssd_branch), F32, "merge_ssd", epilogue=_ep_gate, extras=(gate_logits,),
              extra_specs=(g_spec(0),))
    merged = mm(y_attn, b16(w_attn_branch), BF16, "merge_attn", epilogue=_ep_gate_add,
                extras=(gate_logits, part), extra_specs=(g_spec(1), same))
    h1 = mm(merged, b16(w_out), F32, "out_proj", epilogue=_ep_residual, extras=(x2,), extra_specs=(same,))

    return _moe(h1, norm_ffn_w, router_w, router_b, w1, b1, w2, b2, final_w, normalize,
                tm_router=min(ROUTER_ROW_TILE, t), tm_expert=EXPERT_ROW_TILE, tf=EXPERT_HIDDEN_TILE,
                tn=EXPERT_OUT_TILE, tb_dispatch=min(DISPATCH_TOKENS, t), tb=min(COMBINE_TOKENS, t))


def kernel(x, w_in, conv_w, conv_b, dt_bias, a_log, d_skip, ssd_norm_w, w_ssd_branch, w_attn_branch, w_out,
           norm_mix_w, norm_ffn_w, router_w, router_b, w_expert_in, b_expert_in, w_expert_out, b_expert_out,
           final_norm_w):
    batch, seq, d = x.shape
    depth = w_in.shape[0]
    per_layer = (w_in, conv_w, conv_b, dt_bias, a_log, d_skip, ssd_norm_w, w_ssd_branch, w_attn_branch, w_out,
                 norm_mix_w, norm_ffn_w, router_w, router_b, w_expert_in, b_expert_in, w_expert_out, b_expert_out)
    h = x.reshape(batch * seq, d)
    for layer in range(depth):
        h = _layer(h, batch, seq, *(p[layer] for p in per_layer), final_norm_w, layer == depth - 1)
    return h.reshape(batch, seq, d)
```
